```python
import jax, jax.numpy as jnp
from jax import lax
import numpy as np

D_MODEL = 1024
BATCH = 4
SEQ = 4096
DEPTH = 1
DEC_BATCH = 16
DEC_SEQ = 64
PAST_LEN = 1024

CHUNK = 64
GLA_HEADS = 4
GLA_DK = D_MODEL // (2 * GLA_HEADS)
GLA_DV = D_MODEL // GLA_HEADS
GLA_RANK = 16
GLA_GATE_TAU = 16.0
DSA_HEADS = 8
DSA_KV_HEADS = 2
DSA_HEAD_DIM = 64
IDX_HEADS = 4
IDX_DIM = 64
IDX_SCALE = (IDX_HEADS * IDX_DIM) ** -0.5
TOPK_MAX = 256
QUERY_BLOCK = 128
FFN_DIM = 2816
CONV_W = 3
EPS = 1e-6
IN_SIZES = (GLA_HEADS * GLA_DK, GLA_HEADS * GLA_DK, GLA_HEADS * GLA_DV, GLA_RANK, GLA_HEADS * GLA_DV,
            DSA_HEADS * DSA_HEAD_DIM, DSA_KV_HEADS * DSA_HEAD_DIM, DSA_KV_HEADS * DSA_HEAD_DIM,
            IDX_HEADS * IDX_DIM, IDX_DIM, IDX_HEADS, 2 * D_MODEL)
D_IN = sum(IN_SIZES)

kernel_name = 'gla_dsa_convffn_stream_step'


def rmsnorm(x, g):
    xf = x.astype(jnp.float32)
    y = xf * lax.rsqrt(jnp.mean(xf * xf, axis=-1, keepdims=True) + EPS)
    return (y * g.astype(jnp.float32)).astype(x.dtype)


def gla_recurrence(q, k, v, log_a, s0):
    bsz, L, H, dk = q.shape
    dv = v.shape[-1]
    C = min(CHUNK, L)
    N = L // C

    def to_blocks(t):
        return t.astype(jnp.float32).reshape(bsz, N, C, H, t.shape[-1]).transpose(1, 0, 2, 3, 4)

    qc, kc, vc, gc = to_blocks(q), to_blocks(k), to_blocks(v), to_blocks(log_a)
    qc = qc * (dk ** -0.5)
    causal = jnp.tril(jnp.ones((C, C), dtype=bool))

    def step(S, inp):
        qi, ki, vi, gi = inp
        b = jnp.cumsum(gi, axis=1)
        b_last = b[:, -1]
        q_dec = qi * jnp.exp(b)
        k_inv = ki * jnp.exp(-b)
        scores = jnp.where(causal, jnp.einsum('bchd,bshd->bhcs', q_dec, k_inv), 0.0)
        o = jnp.einsum('bhcs,bshv->bchv', scores, vi) + jnp.einsum('bchd,bhdv->bchv', q_dec, S)
        k_rem = ki * jnp.exp(b_last[:, None] - b)
        S_new = jnp.exp(b_last)[..., None] * S + jnp.einsum('bchd,bchv->bhdv', k_rem, vi)
        return S_new, o

    s_fin, o = lax.scan(step, s0.astype(jnp.float32), (qc, kc, vc, gc))
    o = o.transpose(1, 0, 2, 3, 4).reshape(bsz, L, H, dv)
    return o, s_fin


def sparse_attend(q, q_idx, w_idx, q_pos, k_all, v_all, kidx_all, k_pos, topk):
    bsz, nq, hq, hd = q.shape
    f32 = jnp.float32
    dots = jnp.einsum('bqhd,bld->bqhl', q_idx.astype(f32), kidx_all.astype(f32))
    index_score = jnp.einsum('bqh,bqhl->bql', w_idx.astype(f32) * IDX_SCALE, jax.nn.relu(dots))
    admissible = (k_pos[None, :] // CHUNK) <= (q_pos[:, None] // CHUNK)
    index_score = jnp.where(admissible[None], index_score, -jnp.inf)
    _, sel = lax.top_k(index_score, topk)
    valid = (k_pos[sel] // CHUNK) <= (q_pos[None, :, None] // CHUNK)
    gather = jax.vmap(lambda t, i: t[i])
    k_sel = gather(k_all, sel).astype(f32)
    v_sel = gather(v_all, sel).astype(f32)
    qg = q.astype(f32).reshape(bsz, nq, DSA_KV_HEADS, hq // DSA_KV_HEADS, hd)
    logits = jnp.einsum('bqgrd,bqkgd->bqgrk', qg, k_sel) * (hd ** -0.5)
    logits = jnp.where(valid[:, :, None, None, :], logits, -jnp.inf)
    p = jax.nn.softmax(logits, axis=-1)
    o = jnp.einsum('bqgrk,bqkgd->bqgrd', p, v_sel)
    return o.reshape(bsz, nq, hq * hd)


def dsa_mixer(q, q_idx, w_idx, k_all, v_all, kidx_all, q_start):
    bsz, nq = q.shape[:2]
    n_keys = k_all.shape[1]
    topk = min(TOPK_MAX, n_keys // 4)
    k_pos = jnp.arange(n_keys)
    q_pos = q_start + jnp.arange(nq)
    qb = QUERY_BLOCK if nq % QUERY_BLOCK == 0 else nq
    nb = nq // qb

    def blockify(t):
        return t.reshape(bsz, nb, qb, *t.shape[2:]).swapaxes(0, 1)

    def one_block(args):
        qq, qi, ww, pp = args
        return sparse_attend(qq, qi, ww, pp, k_all, v_all, kidx_all, k_pos, topk)

    o = lax.map(one_block, (blockify(q), blockify(q_idx), blockify(w_idx), q_pos.reshape(nb, qb)))
    return o.swapaxes(0, 1).reshape(bsz, nq, -1)


def conv_ffn(xn, conv_prev, w_up, conv_w, conv_b, w_down):
    L = xn.shape[1]
    h = xn @ w_up
    hp = jnp.concatenate([conv_prev.astype(h.dtype), h], axis=1)
    hc = conv_b + sum(conv_w[j] * hp[:, j:j + L] for j in range(CONV_W))
    a, b = jnp.split(hc, 2, axis=-1)
    return (jax.nn.gelu(a) * b) @ w_down, hp[:, L:]


def run_layer(x, k_past, v_past, kidx_past, gla_s0, conv_prev,
              attn_pre_norm, w_in, w_gla_gate_up, b_gla_gate, gla_head_norm,
              w_branch_gla, w_branch_dsa, w_out, attn_post_norm,
              ffn_pre_norm, w_up, conv_w, conv_b, w_down, ffn_post_norm):
    bsz, L, _ = x.shape
    past = k_past.shape[1]
    xn = rmsnorm(x, attn_pre_norm)
    h = xn @ w_in
    splits = [int(s) for s in np.cumsum(IN_SIZES)[:-1]]
    (g_q, g_k, g_v, g_low, g_r, a_q, a_k, a_v, i_q, i_k, i_w, gate_logits) = jnp.split(h, splits, axis=-1)
    log_a = jax.nn.log_sigmoid((g_low @ w_gla_gate_up + b_gla_gate).astype(jnp.float32)) / GLA_GATE_TAU
    gla_o, gla_s = gla_recurrence(g_q.reshape(bsz, L, GLA_HEADS, GLA_DK),
                                  g_k.reshape(bsz, L, GLA_HEADS, GLA_DK),
                                  g_v.reshape(bsz, L, GLA_HEADS, GLA_DV),
                                  log_a.reshape(bsz, L, GLA_HEADS, GLA_DK), gla_s0)
    gla_o = rmsnorm(gla_o, gla_head_norm).reshape(bsz, L, GLA_HEADS * GLA_DV).astype(x.dtype) * jax.nn.silu(g_r)
    k_new = a_k.reshape(bsz, L, DSA_KV_HEADS, DSA_HEAD_DIM)
    v_new = a_v.reshape(bsz, L, DSA_KV_HEADS, DSA_HEAD_DIM)
    k_all = jnp.concatenate([k_past.astype(x.dtype), k_new], axis=1)
    v_all = jnp.concatenate([v_past.astype(x.dtype), v_new], axis=1)
    kidx_all = jnp.concatenate([kidx_past.astype(x.dtype), i_k], axis=1)
    dsa_o = dsa_mixer(a_q.reshape(bsz, L, DSA_HEADS, DSA_HEAD_DIM),
                      i_q.reshape(bsz, L, IDX_HEADS, IDX_DIM), i_w,
                      k_all, v_all, kidx_all, past).astype(x.dtype)
    gate_a, gate_b = jnp.split(jax.nn.sigmoid(gate_logits), 2, axis=-1)
    mixed = (gate_a * (gla_o @ w_branch_gla) + gate_b * (dsa_o @ w_branch_dsa)) @ w_out
    x = x + rmsnorm(mixed, attn_post_norm)
    f, conv_new = conv_ffn(rmsnorm(x, ffn_pre_norm), conv_prev, w_up, conv_w, conv_b, w_down)
    x = x + rmsnorm(f, ffn_post_norm)
    return x, k_new, v_new, i_k, gla_s, conv_new


def setup_inputs(seed: int = 0) -> dict:
    key = jax.random.key(seed)
    ks = jax.random.split(key, 24)
    nrm = jax.random.normal
    f32 = jnp.float32
    G, hd = DSA_KV_HEADS, DSA_HEAD_DIM

    def gain(k, n):
        return 1.0 + 0.02 * nrm(k, (DEPTH, n), f32)

    return {
        'x_prompt': nrm(ks[0], (BATCH, SEQ, D_MODEL), f32),
        'x_sample': nrm(ks[1], (DEC_BATCH, DEC_SEQ, D_MODEL), f32),
        'cache_k': nrm(ks[2], (DEPTH, DEC_BATCH, PAST_LEN, G, hd), f32),
        'cache_v': nrm(ks[3], (DEPTH, DEC_BATCH, PAST_LEN, G, hd), f32),
        'cache_k_idx': nrm(ks[4], (DEPTH, DEC_BATCH, PAST_LEN, IDX_DIM), f32),
        'state_gla': nrm(ks[5], (DEPTH, DEC_BATCH, GLA_HEADS, GLA_DK, GLA_DV), f32),
        'state_ffn_conv': nrm(ks[6], (DEPTH, DEC_BATCH, CONV_W - 1, 2 * FFN_DIM), f32),
        'attn_pre_norm': gain(ks[7], D_MODEL),
        'w_in': nrm(ks[8], (DEPTH, D_MODEL, D_IN), f32) * D_MODEL ** -0.5,
        'w_gla_gate_up': nrm(ks[9], (DEPTH, GLA_RANK, GLA_HEADS * GLA_DK), f32) * GLA_RANK ** -0.5,
        'b_gla_gate': 0.1 * nrm(ks[10], (DEPTH, GLA_HEADS * GLA_DK), f32),
        'gla_head_norm': gain(ks[11], GLA_DV),
        'w_branch_gla': nrm(ks[12], (DEPTH, GLA_HEADS * GLA_DV, D_MODEL), f32) * (GLA_HEADS * GLA_DV) ** -0.5,
        'w_branch_dsa': nrm(ks[13], (DEPTH, DSA_HEADS * hd, D_MODEL), f32) * (DSA_HEADS * hd) ** -0.5,
        'w_out': nrm(ks[14], (DEPTH, D_MODEL, D_MODEL), f32) * D_MODEL ** -0.5,
        'attn_post_norm': gain(ks[15], D_MODEL),
        'ffn_pre_norm': gain(ks[16], D_MODEL),
        'w_up': nrm(ks[17], (DEPTH, D_MODEL, 2 * FFN_DIM), f32) * D_MODEL ** -0.5,
        'conv_w': nrm(ks[18], (DEPTH, CONV_W, 2 * FFN_DIM), f32) * CONV_W ** -0.5,
        'conv_b': 0.02 * nrm(ks[19], (DEPTH, 2 * FFN_DIM), f32),
        'w_down': nrm(ks[20], (DEPTH, FFN_DIM, D_MODEL), f32) * FFN_DIM ** -0.5,
        'ffn_post_norm': gain(ks[21], D_MODEL),
    }


def reference(x_prompt, x_sample, cache_k, cache_v, cache_k_idx, state_gla, state_ffn_conv,
              attn_pre_norm, w_in, w_gla_gate_up, b_gla_gate, gla_head_norm,
              w_branch_gla, w_branch_dsa, w_out, attn_post_norm,
              ffn_pre_norm, w_up, conv_w, conv_b, w_down, ffn_post_norm):
    bsz = x_prompt.shape[0]
    dt = x_prompt.dtype
    y_prompt, y_sample = x_prompt, x_sample
    prompt_new, sample_new = [], []
    for l in range(DEPTH):
        lw = (attn_pre_norm[l], w_in[l], w_gla_gate_up[l], b_gla_gate[l], gla_head_norm[l],
              w_branch_gla[l], w_branch_dsa[l], w_out[l], attn_post_norm[l],
              ffn_pre_norm[l], w_up[l], conv_w[l], conv_b[l], w_down[l], ffn_post_norm[l])
        y_prompt, kp, vp, ip, sp, cp = run_layer(
            y_prompt,
            jnp.zeros((bsz, 0, DSA_KV_HEADS, DSA_HEAD_DIM), dt),
            jnp.zeros((bsz, 0, DSA_KV_HEADS, DSA_HEAD_DIM), dt),
            jnp.zeros((bsz, 0, IDX_DIM), dt),
            jnp.zeros((bsz, GLA_HEADS, GLA_DK, GLA_DV), jnp.float32),
            jnp.zeros((bsz, CONV_W - 1, 2 * FFN_DIM), dt),
            *lw)
        prompt_new.append((kp, vp, ip, sp, cp))
        y_sample, ksn, vsn, isn, ssn, csn = run_layer(
            y_sample, cache_k[l], cache_v[l], cache_k_idx[l], state_gla[l], state_ffn_conv[l], *lw)
        sample_new.append((ksn, vsn, isn, ssn, csn))
    k_p, v_p, kidx_p, gla_p, conv_p = (jnp.stack(t, axis=0) for t in zip(*prompt_new))
    k_s, v_s, kidx_s, gla_s, conv_s = (jnp.stack(t, axis=0) for t in zip(*sample_new))
    return (y_prompt, y_sample, k_p, v_p, kidx_p, gla_p, conv_p, k_s, v_s, kidx_s, gla_s, conv_s)
```

```python
import functools

import numpy as np
import jax
import jax.numpy as jnp
from jax import lax
from jax.experimental import pallas as pl
from jax.experimental.pallas import tpu as pltpu

F32 = jnp.float32
BF16 = jnp.bfloat16
I32 = jnp.int32
HIGHEST = lax.Precision.HIGHEST

CHUNK = 64
GLA_HEADS = 4
GLA_DK = 128
GLA_DV = 256
GLA_RANK = 16
GLA_GATE_TAU = 16.0
DSA_HEADS = 8
DSA_KV_HEADS = 2
DSA_HEAD_DIM = 64
IDX_HEADS = 4
IDX_DIM = 64
IDX_SCALE = (IDX_HEADS * IDX_DIM) ** -0.5
TOPK_MAX = 256
CONV_W = 3
EPS = 1e-6

LANES = 128
SUBLANES = 8
VMEM_LIMIT = 56 * 1024 * 1024

MISC_GLOW = 0
MISC_IW = 16

KEY_CHUNK = 256
INT_MIN = -(2 ** 31)
NT_DIMS = (((1,), (1,)), ((), ()))
TN_DIMS = (((0,), (0,)), ((), ()))


def _dot(a, b, **kw):
    return jnp.dot(a, b, preferred_element_type=F32, **kw)


def _dot_nt(a, b):
    return lax.dot_general(a, b, NT_DIMS, preferred_element_type=F32)


def _split_bf16(x):
    hi = x.astype(BF16)
    lo = (x - hi.astype(F32)).astype(BF16)
    return hi, lo


def _rmsnorm(x, g):
    return x * lax.rsqrt(jnp.mean(x * x, axis=-1, keepdims=True) + EPS) * g


def _params(*sem):
    return pltpu.CompilerParams(dimension_semantics=sem, vmem_limit_bytes=VMEM_LIMIT)


MAIN_SIZES = (GLA_HEADS * GLA_DK, GLA_HEADS * GLA_DK, GLA_HEADS * GLA_DV, GLA_HEADS * GLA_DV,
              DSA_HEADS * DSA_HEAD_DIM, DSA_KV_HEADS * DSA_HEAD_DIM, DSA_KV_HEADS * DSA_HEAD_DIM)
HP_WIDTH = 512


def _proj_kernel(x_ref, g_ref, wm_ref, whi_ref, wlo_ref,
                 gq_ref, gk_ref, gv_ref, gr_ref, aq_ref, ak_ref, av_ref, gate_ref, iq_ref, ik_ref, misc_ref):
    xn = _rmsnorm(x_ref[...], g_ref[...])
    xh, xl = _split_bf16(xn)
    outs = (gq_ref, gk_ref, gv_ref, gr_ref, aq_ref, ak_ref, av_ref, gate_ref)
    off = 0
    for o_ref in outs:
        n = o_ref.shape[-1]
        o_ref[...] = _dot(xh, wm_ref[:, off:off + n]).astype(o_ref.dtype)
        off += n
    whi = whi_ref[...]
    hp = _dot(xh, whi) + _dot(xl, whi) + _dot(xh, wlo_ref[...])
    iq_ref[...] = hp[:, 0:256]
    ik_ref[...] = hp[:, 256:320]
    misc_ref[...] = hp[:, 384:512]


def _proj_call(x, g, wm, whi, wlo, tm):
    t, d = x.shape
    gate_w = wm.shape[1] - sum(MAIN_SIZES)
    widths = MAIN_SIZES + (gate_w,)
    dtypes = (BF16, BF16, BF16, BF16, BF16, F32, F32, BF16)
    out_shape = [jax.ShapeDtypeStruct((t, n), dt) for n, dt in zip(widths, dtypes)]
    out_shape += [jax.ShapeDtypeStruct((t, n), F32) for n in (256, 64, 128)]
    row = lambda n: pl.BlockSpec((tm, n), lambda i: (i, 0))
    full = lambda a: pl.BlockSpec(a.shape, lambda i: (0, 0))
    return pl.pallas_call(
        _proj_kernel,
        grid=(t // tm,),
        in_specs=[row(d), full(g), full(wm), full(whi), full(wlo)],
        out_specs=[row(s.shape[1]) for s in out_shape],
        out_shape=out_shape,
        compiler_params=_params("parallel"),
        name="proj",
    )(x, g, wm, whi, wlo)


def _gla_kernel(gq_ref, gk_ref, gv_ref, gr_ref, misc_ref, wup_ref, bup_ref, hn_ref, s0_ref,
                o_ref, sfin_ref, st_ref, *, nchunks):
    j = pl.program_id(1)

    @pl.when(j == 0)
    def _():
        for h in range(GLA_HEADS):
            st_ref[h] = s0_ref[h].T

    row = lax.broadcasted_iota(I32, (CHUNK, CHUNK), 0)
    col = lax.broadcasted_iota(I32, (CHUNK, CHUNK), 1)
    causal = row >= col
    tri = causal.astype(F32)

    def chunk(c, carry):
        r0 = pl.multiple_of(c * CHUNK, CHUNK)
        rows = pl.ds(r0, CHUNK)
        z = _dot(misc_ref[rows, :], wup_ref[...], precision=HIGHEST) + bup_ref[...]
        log_a = (jnp.minimum(z, 0.0) - jnp.log(1.0 + jnp.exp(-jnp.abs(z)))) * (1.0 / GLA_GATE_TAU)
        b = _dot(tri, log_a, precision=HIGHEST)
        b_last = b[CHUNK - 1:CHUNK, :]
        q = gq_ref[rows, :].astype(F32)
        k = gk_ref[rows, :].astype(F32)
        q_dec = (q * (GLA_DK ** -0.5) * jnp.exp(b)).astype(BF16)
        k_inv = (k * jnp.exp(-b)).astype(BF16)
        k_rem = (k * jnp.exp(b_last - b)).astype(BF16)
        d_last = jnp.exp(b_last)
        v = gv_ref[rows, :]
        gr = gr_ref[rows, :].astype(F32)
        hn = hn_ref[...]
        for h in range(GLA_HEADS):
            ks = slice(h * GLA_DK, (h + 1) * GLA_DK)
            vs = slice(h * GLA_DV, (h + 1) * GLA_DV)
            qh, vh = q_dec[:, ks], v[:, vs]
            scores = jnp.where(causal, _dot_nt(qh, k_inv[:, ks]), 0.0)
            st = st_ref[h]
            o = _dot(scores.astype(BF16), vh) + _dot_nt(qh, st.astype(BF16))
            ut = lax.dot_general(vh, k_rem[:, ks], TN_DIMS, preferred_element_type=F32)
            st_ref[h] = st * d_last[:, ks] + ut
            y = _rmsnorm(o, hn)
            g = gr[:, vs]
            o_ref[rows, vs] = (y * (g * jax.nn.sigmoid(g))).astype(o_ref.dtype)
        return carry

    lax.fori_loop(0, nchunks, chunk, 0)

    @pl.when(j == pl.num_programs(1) - 1)
    def _():
        for h in range(GLA_HEADS):
            sfin_ref[h] = st_ref[h].T


def _gla_call(gq, gk, gv, gr, misc, wup, bup, hn, s0, tb):
    bsz, seq, _ = gq.shape
    blk = lambda n: pl.BlockSpec((None, tb, n), lambda b, j: (b, j, 0))
    full2 = lambda a: pl.BlockSpec(a.shape, lambda b, j: (0, 0))
    st_spec = pl.BlockSpec((None, GLA_HEADS, GLA_DK, GLA_DV), lambda b, j: (b, 0, 0, 0))
    return pl.pallas_call(
        functools.partial(_gla_kernel, nchunks=tb // CHUNK),
        grid=(bsz, seq // tb),
        in_specs=[blk(gq.shape[2]), blk(gk.shape[2]), blk(gv.shape[2]), blk(gr.shape[2]), blk(misc.shape[2]),
                  full2(wup), full2(bup), full2(hn), st_spec],
        out_specs=[blk(gv.shape[2]), st_spec],
        out_shape=[jax.ShapeDtypeStruct((bsz, seq, gv.shape[2]), BF16),
                   jax.ShapeDtypeStruct(s0.shape, F32)],
        scratch_shapes=[pltpu.VMEM((GLA_HEADS, GLA_DV, GLA_DK), F32)],
        compiler_params=_params("parallel", "arbitrary"),
        name="gla",
    )(gq, gk, gv, gr, misc, wup, bup, hn, s0)


def _dsa_kernel(iq_ref, misc_ref, aq_ref, ik_ref, k_ref, v_ref, o_ref,
                ikcat_ref, kb_ref, vb_ref, keys_ref, bias_ref, s_ref,
                *, tq, tqp, lk, q_start, n_keys, topk):
    qi = pl.program_id(1)
    kc = KEY_CHUNK
    hd = DSA_HEAD_DIM

    @pl.when(qi == 0)
    def _():
        def body(c, carry):
            rows = pl.ds(pl.multiple_of(c * kc, kc), kc)
            hi, lo = _split_bf16(ik_ref[rows, :])
            ikcat_ref[rows, :] = jnp.concatenate([hi, hi, lo], axis=1)
            kb_ref[rows, :] = k_ref[rows, :].astype(BF16)
            vb_ref[rows, :] = v_ref[rows, :].astype(BF16)
            return carry
        lax.fori_loop(0, lk // kc, body, 0)

    last_q = q_start + (qi + 1) * tq - 1
    k_end = jnp.minimum(n_keys, ((last_q >> 6) + 1) * CHUNK)
    nk = (k_end + kc - 1) // kc

    def pad_rows(x):
        if tqp == tq:
            return x
        return jnp.concatenate([x, jnp.zeros((tqp - tq, x.shape[1]), x.dtype)], axis=0)

    iq_hi, iq_lo = _split_bf16(pad_rows(iq_ref[...]))
    iq_cat = []
    for h in range(IDX_HEADS):
        s = slice(h * IDX_DIM, (h + 1) * IDX_DIM)
        iq_cat.append(jnp.concatenate([iq_hi[:, s], iq_lo[:, s], iq_hi[:, s]], axis=1))
    wts = pad_rows(misc_ref[...]).T[MISC_IW:MISC_IW + IDX_HEADS, :] * IDX_SCALE
    q_chunk = (q_start + qi * tq + lax.broadcasted_iota(I32, (1, tqp), 1)) >> 6

    def admissible(r0):
        k_pos = r0 + lax.broadcasted_iota(I32, (kc, tqp), 0)
        return ((k_pos >> 6) <= q_chunk) & (k_pos < n_keys)

    def score_body(c, carry):
        r0 = pl.multiple_of(c * kc, kc)
        ik = ikcat_ref[pl.ds(r0, kc), :]
        acc = jnp.zeros((kc, tqp), F32)
        for h in range(IDX_HEADS):
            acc = acc + wts[h:h + 1, :] * jnp.maximum(_dot_nt(ik, iq_cat[h]), 0.0)
        score = jnp.where(admissible(r0), acc, -jnp.inf)
        bits = lax.bitcast_convert_type(score, I32)
        keys_ref[pl.ds(r0, kc), :] = jnp.where(bits < 0, bits ^ jnp.int32(0x7FFFFFFF), bits)
        return carry

    lax.fori_loop(0, nk, score_body, 0)

    grp = 32

    def count(pred):
        def body(c, acc):
            ch = keys_ref[pl.ds(pl.multiple_of(c * kc, kc), kc), :]
            m = jnp.where(pred(ch), 1.0, 0.0)
            for i in range(kc // grp):
                acc = acc + m[i * grp:(i + 1) * grp, :]
            return acc
        acc = lax.fori_loop(0, nk, body, jnp.zeros((grp, tqp), F32))
        return jnp.sum(acc, axis=0, keepdims=True)

    def bit_body(i, prefix):
        cand_u = prefix | (jnp.int32(1) << (31 - i))
        cand_s = cand_u ^ jnp.int32(INT_MIN)
        tot = count(lambda ch: ch >= cand_s)
        return jnp.where(tot >= topk, cand_u, prefix)

    prefix = lax.fori_loop(0, 32, bit_body, jnp.zeros((1, tqp), I32))
    thr = prefix ^ jnp.int32(INT_MIN)
    need = topk - count(lambda ch: ch > thr)

    tri = (lax.broadcasted_iota(I32, (kc, kc), 0) >= lax.broadcasted_iota(I32, (kc, kc), 1)).astype(BF16)

    def sel_body(c, running):
        r0 = pl.multiple_of(c * kc, kc)
        ch = keys_ref[pl.ds(r0, kc), :]
        eq = ch == thr
        rank = _dot(tri, jnp.where(eq, 1.0, 0.0).astype(BF16)) + running
        sel = ((ch > thr) | (eq & (rank <= need))) & admissible(r0)
        bias_t = jnp.where(sel, 0.0, -jnp.inf)
        bias_ref[:, pl.ds(r0, kc)] = bias_t.T[:tq, :]
        return rank[kc - 1:kc, :]

    lax.fori_loop(0, nk, sel_body, jnp.zeros((1, tqp), F32))

    for h in range(DSA_HEADS):
        g = h // (DSA_HEADS // DSA_KV_HEADS)
        gs = slice(g * hd, (g + 1) * hd)
        qh = aq_ref[:, h * hd:(h + 1) * hd] * jnp.asarray(hd ** -0.5, BF16)

        def pass1(c, mx):
            r0 = pl.multiple_of(c * kc, kc)
            s = _dot_nt(qh, kb_ref[pl.ds(r0, kc), gs]) + bias_ref[:, pl.ds(r0, kc)]
            s_ref[:, pl.ds(r0, kc)] = s
            for i in range(kc // LANES):
                mx = jnp.maximum(mx, s[:, i * LANES:(i + 1) * LANES])
            return mx

        mx = lax.fori_loop(0, nk, pass1, jnp.full((tq, LANES), -jnp.inf, F32))
        m = jnp.max(mx, axis=1, keepdims=True)

        def pass2(c, carry):
            l, acc = carry
            r0 = pl.multiple_of(c * kc, kc)
            p = jnp.exp(s_ref[:, pl.ds(r0, kc)] - m)
            for i in range(kc // LANES):
                l = l + p[:, i * LANES:(i + 1) * LANES]
            acc = acc + _dot(p.astype(BF16), vb_ref[pl.ds(r0, kc), gs])
            return l, acc

        l, acc = lax.fori_loop(0, nk, pass2, (jnp.zeros((tq, LANES), F32), jnp.zeros((tq, hd), F32)))
        o_ref[:, h * hd:(h + 1) * hd] = (acc / jnp.sum(l, axis=1, keepdims=True)).astype(o_ref.dtype)


def _dsa_call(iq, misc, aq, ik_all, k_all, v_all, *, q_start, n_keys, tq):
    bsz, lq, _ = iq.shape
    lk = ik_all.shape[1]
    assert lk % KEY_CHUNK == 0 and lq % tq == 0
    tqp = max(tq, LANES)
    topk = min(TOPK_MAX, n_keys // 4)
    qblk = lambda n: pl.BlockSpec((None, tq, n), lambda b, q: (b, q, 0))
    kblk = lambda n: pl.BlockSpec((None, lk, n), lambda b, q: (b, 0, 0))
    kern = functools.partial(_dsa_kernel, tq=tq, tqp=tqp, lk=lk, q_start=q_start, n_keys=n_keys, topk=topk)
    return pl.pallas_call(
        kern,
        grid=(bsz, lq // tq),
        in_specs=[qblk(iq.shape[2]), qblk(misc.shape[2]), qblk(aq.shape[2]),
                  kblk(ik_all.shape[2]), kblk(k_all.shape[2]), kblk(v_all.shape[2])],
        out_specs=qblk(aq.shape[2]),
        out_shape=jax.ShapeDtypeStruct(aq.shape, BF16),
        scratch_shapes=[pltpu.VMEM((lk, 3 * IDX_DIM), BF16),
                        pltpu.VMEM((lk, k_all.shape[2]), BF16),
                        pltpu.VMEM((lk, v_all.shape[2]), BF16),
                        pltpu.VMEM((lk, tqp), I32),
                        pltpu.VMEM((tq, lk), F32),
                        pltpu.VMEM((tq, lk), F32)],
        compiler_params=_params("parallel", "arbitrary"),
        name="dsa",
    )(iq, misc, aq, ik_all, k_all, v_all)


def _merge_kernel(x_ref, go_ref, do_ref, gate_ref, wa_ref, wb_ref, wo_ref, g_ref, y_ref):
    d = x_ref.shape[-1]
    a = _dot(go_ref[...], wa_ref[...])
    b = _dot(do_ref[...], wb_ref[...])
    gate = gate_ref[...].astype(F32)
    mixed = jax.nn.sigmoid(gate[:, :d]) * a + jax.nn.sigmoid(gate[:, d:]) * b
    y_ref[...] = x_ref[...] + _rmsnorm(_dot(mixed.astype(BF16), wo_ref[...]), g_ref[...])


def _merge_call(x, go, do, gate, wa, wb, wo, g, tm):
    t, d = x.shape
    row = lambda n: pl.BlockSpec((tm, n), lambda i: (i, 0))
    full = lambda a: pl.BlockSpec(a.shape, lambda i: (0, 0))
    return pl.pallas_call(
        _merge_kernel,
        grid=(t // tm,),
        in_specs=[row(d), row(go.shape[1]), row(do.shape[1]), row(gate.shape[1]),
                  full(wa), full(wb), full(wo), full(g)],
        out_specs=row(d),
        out_shape=jax.ShapeDtypeStruct((t, d), F32),
        compiler_params=_params("parallel"),
        name="merge",
    )(x, go, do, gate, wa, wb, wo, g)


def _gelu_tanh(x):
    return 0.5 * x * (1.0 + jnp.tanh(np.sqrt(2.0 / np.pi) * (x + 0.044715 * (x * x * x))))


def _ffn_kernel(x_ref, gpre_ref, wua_ref, wub_ref, cwa_ref, cwb_ref, cba_ref, cbb_ref, pva_ref, pvb_ref,
                wd_ref, gpost_ref, y_ref, cna_ref, cnb_ref, xn_ref, acc_ref, cara_ref, carb_ref, *, bps):
    i = pl.program_id(0)
    j = pl.program_id(1)
    tm, tn = x_ref.shape[0], wua_ref.shape[1]

    @pl.when(j == 0)
    def _():
        xn_ref[...] = _rmsnorm(x_ref[...], gpre_ref[...]).astype(BF16)
        acc_ref[...] = jnp.zeros_like(acc_ref)

    xn = xn_ref[...]
    first = (i % bps) == 0
    rows = lax.broadcasted_iota(I32, (tm, tn), 0)

    def conv_half(w_ref, cw_ref, cb_ref, pv_ref, car_ref, cn_ref):
        u = _dot(xn, w_ref[...])
        prev = pv_ref[...]
        car = car_ref[j]
        h0 = jnp.where(first, prev[0:1, :], car[SUBLANES - 2:SUBLANES - 1, :])
        h1 = jnp.where(first, prev[1:2, :], car[SUBLANES - 1:SUBLANES, :])
        u1 = jnp.where(rows == 0, h1, pltpu.roll(u, 1, 0))
        u2 = jnp.where(rows == 0, h0, jnp.where(rows == 1, h1, pltpu.roll(u, 2, 0)))
        cw = cw_ref[...]
        car_ref[j] = u[tm - SUBLANES:tm, :]
        cn_ref[...] = u[tm - (CONV_W - 1):tm, :]
        return cb_ref[...] + (cw[0:1, :] * u2 + cw[1:2, :] * u1 + cw[2:3, :] * u)

    a = conv_half(wua_ref, cwa_ref, cba_ref, pva_ref, cara_ref, cna_ref)
    b = conv_half(wub_ref, cwb_ref, cbb_ref, pvb_ref, carb_ref, cnb_ref)
    acc_ref[...] += _dot((_gelu_tanh(a) * b).astype(BF16), wd_ref[...])

    @pl.when(j == pl.num_programs(1) - 1)
    def _():
        y_ref[...] = x_ref[...] + _rmsnorm(acc_ref[...], gpost_ref[...])


def _ffn_call(x, gpre, w_up, conv_w, conv_b, conv_prev, w_down, gpost, *, tm, seq, tn):
    t, d = x.shape
    f = w_down.shape[0]
    nseq = t // seq
    bps = seq // tm
    nft = f // tn
    assert f % tn == 0 and seq % tm == 0
    xrow = pl.BlockSpec((tm, d), lambda i, j: (i, 0))
    vec = pl.BlockSpec((1, d), lambda i, j: (0, 0))
    col = lambda r, off: pl.BlockSpec((r, tn), lambda i, j: (0, j + off))
    prev = lambda off: pl.BlockSpec((None, CONV_W - 1, tn), lambda i, j: (i // bps, 0, j + off))
    cn_spec = pl.BlockSpec((None, CONV_W - 1, tn), lambda i, j: (i, 0, j))
    y, cna, cnb = pl.pallas_call(
        functools.partial(_ffn_kernel, bps=bps),
        grid=(t // tm, nft),
        in_specs=[xrow, vec, col(d, 0), col(d, nft), col(CONV_W, 0), col(CONV_W, nft), col(1, 0), col(1, nft),
                  prev(0), prev(nft), pl.BlockSpec((tn, d), lambda i, j: (j, 0)), vec],
        out_specs=[xrow, cn_spec, cn_spec],
        out_shape=[jax.ShapeDtypeStruct((t, d), F32),
                   jax.ShapeDtypeStruct((t // tm, CONV_W - 1, f), F32),
                   jax.ShapeDtypeStruct((t // tm, CONV_W - 1, f), F32)],
        scratch_shapes=[pltpu.VMEM((tm, d), BF16), pltpu.VMEM((tm, d), F32),
                        pltpu.VMEM((nft, SUBLANES, tn), F32), pltpu.VMEM((nft, SUBLANES, tn), F32)],
        compiler_params=_params("arbitrary", "arbitrary"),
        name="ffn",
    )(x, gpre, w_up, w_up, conv_w, conv_w, conv_b, conv_b, conv_prev, conv_prev, w_down, gpost)
    conv_new = jnp.concatenate([cna, cnb], axis=-1).reshape(nseq, bps, CONV_W - 1, 2 * f)[:, bps - 1]
    return y, conv_new


def _prep_weights(w_in, w_gate_up, d_model):
    sizes = (GLA_HEADS * GLA_DK, GLA_HEADS * GLA_DK, GLA_HEADS * GLA_DV, GLA_RANK, GLA_HEADS * GLA_DV,
             DSA_HEADS * DSA_HEAD_DIM, DSA_KV_HEADS * DSA_HEAD_DIM, DSA_KV_HEADS * DSA_HEAD_DIM,
             IDX_HEADS * IDX_DIM, IDX_DIM, IDX_HEADS, 2 * d_model)
    offs = np.concatenate([[0], np.cumsum(sizes)])
    g_q, g_k, g_v, g_low, g_r, a_q, a_k, a_v, i_q, i_k, i_w, gate = (
        w_in[:, int(offs[n]):int(offs[n + 1])] for n in range(len(sizes)))
    w_main = jnp.concatenate([g_q, g_k, g_v, g_r, a_q, a_k, a_v, gate], axis=1).astype(BF16)
    zeros = lambda n: jnp.zeros((w_in.shape[0], n), w_in.dtype)
    misc = jnp.concatenate([g_low, i_w, zeros(LANES - GLA_RANK - IDX_HEADS)], axis=1)
    assert MISC_GLOW == 0 and MISC_IW == GLA_RANK
    w_hp = jnp.concatenate([i_q, i_k, zeros(LANES - IDX_DIM), misc], axis=1)
    assert w_hp.shape[1] == HP_WIDTH
    w_hp_hi, w_hp_lo = _split_bf16(w_hp)
    wup = jnp.concatenate([w_gate_up, jnp.zeros((LANES - GLA_RANK, w_gate_up.shape[1]), w_gate_up.dtype)], axis=0)
    return w_main, w_hp_hi, w_hp_lo, wup


def _run_layer(x, k_past, v_past, kidx_past, gla_s0, conv_prev, w, *, tm, gla_tb, dsa_tq, ffn_tm):
    bsz, seq, d = x.shape
    past = k_past.shape[1]
    t = bsz * seq
    x2 = x.reshape(t, d)
    gq, gk, gv, gr, aq, ak, av, gate, iq, ik, misc = _proj_call(
        x2, w["attn_pre_norm"], w["w_main"], w["w_hp_hi"], w["w_hp_lo"], tm)
    per_seq = lambda a: a.reshape(bsz, seq, a.shape[-1])

    gla_o, gla_s = _gla_call(per_seq(gq), per_seq(gk), per_seq(gv), per_seq(gr), per_seq(misc),
                             w["wup"], w["b_gla_gate"], w["gla_head_norm"], gla_s0, gla_tb)

    n_keys = past + seq
    lk = -(-n_keys // KEY_CHUNK) * KEY_CHUNK

    def keys_all(past_rows, new_rows):
        parts = [per_seq(new_rows)]
        if past:
            parts.insert(0, past_rows.reshape(bsz, past, new_rows.shape[-1]).astype(F32))
        if lk > n_keys:
            parts.append(jnp.zeros((bsz, lk - n_keys, new_rows.shape[-1]), F32))
        return jnp.concatenate(parts, axis=1) if len(parts) > 1 else parts[0]

    dsa_o = _dsa_call(per_seq(iq), per_seq(misc), per_seq(aq),
                      keys_all(kidx_past, ik), keys_all(k_past, ak), keys_all(v_past, av),
                      q_start=past, n_keys=n_keys, tq=dsa_tq)

    x1 = _merge_call(x2, gla_o.reshape(t, -1), dsa_o.reshape(t, -1), gate,
                     w["w_branch_gla"], w["w_branch_dsa"], w["w_out"], w["attn_post_norm"], tm)
    y, conv_new = _ffn_call(x1, w["ffn_pre_norm"], w["w_up"], w["conv_w"], w["conv_b"], conv_prev,
                            w["w_down"], w["ffn_post_norm"], tm=ffn_tm, seq=seq, tn=256)
    k_new = ak.reshape(bsz, seq, DSA_KV_HEADS, DSA_HEAD_DIM)
    v_new = av.reshape(bsz, seq, DSA_KV_HEADS, DSA_HEAD_DIM)
    return y.reshape(bsz, seq, d), k_new, v_new, per_seq(ik), gla_s, conv_new


def kernel(x_prompt, x_sample, cache_k, cache_v, cache_k_idx, state_gla, state_ffn_conv, attn_pre_norm, w_in, w_gla_gate_up, b_gla_gate, gla_head_norm, w_branch_gla, w_branch_dsa, w_out, attn_post_norm, ffn_pre_norm, w_up, conv_w, conv_b, w_down, ffn_post_norm):
    depth = w_in.shape[0]
    bsz, _, d = x_prompt.shape
    dt = x_prompt.dtype
    y_prompt, y_sample = x_prompt, x_sample
    prompt_new, sample_new = [], []
    row = lambda a: a.reshape(1, -1)
    for l in range(depth):
        w_main, w_hp_hi, w_hp_lo, wup = _prep_weights(w_in[l], w_gla_gate_up[l], d)
        w = dict(attn_pre_norm=row(attn_pre_norm[l]), w_main=w_main, w_hp_hi=w_hp_hi, w_hp_lo=w_hp_lo, wup=wup,
                 b_gla_gate=row(b_gla_gate[l]), gla_head_norm=row(gla_head_norm[l]),
                 w_branch_gla=w_branch_gla[l].astype(BF16), w_branch_dsa=w_branch_dsa[l].astype(BF16),
                 w_out=w_out[l].astype(BF16), attn_post_norm=row(attn_post_norm[l]),
                 ffn_pre_norm=row(ffn_pre_norm[l]), w_up=w_up[l].astype(BF16), conv_w=conv_w[l],
                 conv_b=row(conv_b[l]), w_down=w_down[l].astype(BF16), ffn_post_norm=row(ffn_post_norm[l]))
        ffn2 = conv_w.shape[-1]
        y_prompt, *new = _run_layer(
            y_prompt,
            jnp.zeros((bsz, 0, DSA_KV_HEADS * DSA_HEAD_DIM), dt),
            jnp.zeros((bsz, 0, DSA_KV_HEADS * DSA_HEAD_DIM), dt),
            jnp.zeros((bsz, 0, IDX_DIM), dt),
            jnp.zeros((bsz, GLA_HEADS, GLA_DK, GLA_DV), F32),
            jnp.zeros((bsz, CONV_W - 1, ffn2), dt),
            w, tm=256, gla_tb=512, dsa_tq=256, ffn_tm=1024)
        prompt_new.append(new)
        y_sample, *new = _run_layer(
            y_sample, cache_k[l], cache_v[l], cache_k_idx[l], state_gla[l], state_ffn_conv[l],
            w, tm=256, gla_tb=x_sample.shape[1], dsa_tq=x_sample.shape[1], ffn_tm=x_sample.shape[1])
        sample_new.append(new)
    k_p, v_p, kidx_p, gla_p, conv_p = (jnp.stack(t, axis=0) for t in zip(*prompt_new))
    k_s, v_s, kidx_s, gla_s, conv_s = (jnp.stack(t, axis=0) for t in zip(*sample_new))
    return (y_prompt, y_sample, k_p, v_p, kidx_p, gla_p, conv_p, k_s, v_s, kidx_s, gla_s, conv_s)
```

```python
import functools

import numpy as np
import jax
import jax.numpy as jnp
from jax import lax
from jax.experimental import pallas as pl
from jax.experimental.pallas import tpu as pltpu

F32 = jnp.float32
BF16 = jnp.bfloat16
I32 = jnp.int32
I16 = jnp.int16
HIGHEST = lax.Precision.HIGHEST

CHUNK = 64
GLA_HEADS = 4
GLA_DK = 128
GLA_DV = 256
GLA_RANK = 16
GLA_GATE_TAU = 16.0
DSA_HEADS = 8
DSA_KV_HEADS = 2
DSA_HEAD_DIM = 64
IDX_HEADS = 4
IDX_DIM = 64
IDX_SCALE = (IDX_HEADS * IDX_DIM) ** -0.5
TOPK_MAX = 256
CONV_W = 3
EPS = 1e-6

LANES = 128
SUBLANES = 8
VMEM_LIMIT = 56 * 1024 * 1024

MISC_GLOW = 0
MISC_IW = 16

KEY_CHUNK = 256
KEY_UNROLL = 2
INT16_MIN = -(2 ** 15)
NT_DIMS = (((1,), (1,)), ((), ()))
TN_DIMS = (((0,), (0,)), ((), ()))


def _dot(a, b, **kw):
    return jnp.dot(a, b, preferred_element_type=F32, **kw)


def _dot_nt(a, b):
    return lax.dot_general(a, b, NT_DIMS, preferred_element_type=F32)


def _split_bf16(x):
    hi = x.astype(BF16)
    lo = (x - hi.astype(F32)).astype(BF16)
    return hi, lo


def _rmsnorm(x, g):
    return x * lax.rsqrt(jnp.mean(x * x, axis=-1, keepdims=True) + EPS) * g


def _params(*sem):
    return pltpu.CompilerParams(dimension_semantics=sem, vmem_limit_bytes=VMEM_LIMIT)


MAIN_SIZES = (GLA_HEADS * GLA_DK, GLA_HEADS * GLA_DK, GLA_HEADS * GLA_DV, GLA_HEADS * GLA_DV,
              DSA_HEADS * DSA_HEAD_DIM, DSA_KV_HEADS * DSA_HEAD_DIM, DSA_KV_HEADS * DSA_HEAD_DIM)
HP_WIDTH = 512
PACK_WIDTHS = (3 * IDX_DIM, DSA_KV_HEADS * DSA_HEAD_DIM, 2 * DSA_KV_HEADS * DSA_HEAD_DIM)


def _pack_keys(ik, k, v):
    hi, lo = _split_bf16(ik)
    ikcat = jnp.concatenate([hi, hi, lo], axis=1)
    vb = v.astype(BF16)
    ones = jnp.ones((v.shape[0], DSA_HEAD_DIM), BF16)
    parts = []
    for g in range(DSA_KV_HEADS):
        parts += [vb[:, g * DSA_HEAD_DIM:(g + 1) * DSA_HEAD_DIM], ones]
    return ikcat, k.astype(BF16), jnp.concatenate(parts, axis=1)


def _proj_kernel(x_ref, g_ref, wm_ref, whi_ref, wlo_ref,
                 gq_ref, gk_ref, gv_ref, gr_ref, aq_ref, ak_ref, av_ref, gate_ref, iq_ref, ik_ref, misc_ref,
                 ikcat_ref, kb_ref, vaug_ref):
    xn = _rmsnorm(x_ref[...], g_ref[...])
    xh, xl = _split_bf16(xn)
    outs = (gq_ref, gk_ref, gv_ref, gr_ref, aq_ref, ak_ref, av_ref, gate_ref)
    off = 0
    vals = []
    for o_ref in outs:
        n = o_ref.shape[-1]
        vals.append(_dot(xh, wm_ref[:, off:off + n]))
        o_ref[...] = vals[-1].astype(o_ref.dtype)
        off += n
    ak, av = vals[5], vals[6]
    whi = whi_ref[...]
    hp = _dot(xh, whi) + _dot(xl, whi) + _dot(xh, wlo_ref[...])
    iq_ref[...] = hp[:, 0:256]
    ik_ref[...] = hp[:, 256:320]
    misc_ref[...] = hp[:, 384:512]
    ikcat_ref[...], kb_ref[...], vaug_ref[...] = _pack_keys(hp[:, 256:320], ak, av)


def _proj_call(x, g, wm, whi, wlo, tm):
    t, d = x.shape
    gate_w = wm.shape[1] - sum(MAIN_SIZES)
    widths = MAIN_SIZES + (gate_w,)
    dtypes = (BF16, BF16, BF16, BF16, BF16, F32, F32, BF16)
    out_shape = [jax.ShapeDtypeStruct((t, n), dt) for n, dt in zip(widths, dtypes)]
    out_shape += [jax.ShapeDtypeStruct((t, n), F32) for n in (256, 64, 128)]
    out_shape += [jax.ShapeDtypeStruct((t, n), BF16) for n in PACK_WIDTHS]
    row = lambda n: pl.BlockSpec((tm, n), lambda i: (i, 0))
    full = lambda a: pl.BlockSpec(a.shape, lambda i: (0, 0))
    return pl.pallas_call(
        _proj_kernel,
        grid=(t // tm,),
        in_specs=[row(d), full(g), full(wm), full(whi), full(wlo)],
        out_specs=[row(s.shape[1]) for s in out_shape],
        out_shape=out_shape,
        compiler_params=_params("parallel"),
        name="proj",
    )(x, g, wm, whi, wlo)


def _pack_kernel(ik_ref, k_ref, v_ref, ikcat_ref, kb_ref, vaug_ref):
    ikcat_ref[...], kb_ref[...], vaug_ref[...] = _pack_keys(ik_ref[...], k_ref[...], v_ref[...])


def _pack_call(ik, k, v, tm):
    t = ik.shape[0]
    row = lambda n: pl.BlockSpec((tm, n), lambda i: (i, 0))
    return pl.pallas_call(
        _pack_kernel,
        grid=(t // tm,),
        in_specs=[row(ik.shape[1]), row(k.shape[1]), row(v.shape[1])],
        out_specs=[row(n) for n in PACK_WIDTHS],
        out_shape=[jax.ShapeDtypeStruct((t, n), BF16) for n in PACK_WIDTHS],
        compiler_params=_params("parallel"),
        name="pack",
    )(ik, k, v)


def _gla_kernel(gq_ref, gk_ref, gv_ref, gr_ref, misc_ref, wup_ref, bup_ref, hn_ref, s0_ref,
                o_ref, sfin_ref, st_ref, *, nchunks):
    j = pl.program_id(1)

    @pl.when(j == 0)
    def _():
        for h in range(GLA_HEADS):
            st_ref[h] = s0_ref[h].T

    row = lax.broadcasted_iota(I32, (CHUNK, CHUNK), 0)
    col = lax.broadcasted_iota(I32, (CHUNK, CHUNK), 1)
    causal = row >= col
    tri = causal.astype(F32)

    def chunk(c, carry):
        r0 = pl.multiple_of(c * CHUNK, CHUNK)
        rows = pl.ds(r0, CHUNK)
        z = _dot(misc_ref[rows, :], wup_ref[...], precision=HIGHEST) + bup_ref[...]
        log_a = (jnp.minimum(z, 0.0) - jnp.log(1.0 + jnp.exp(-jnp.abs(z)))) * (1.0 / GLA_GATE_TAU)
        b = _dot(tri, log_a, precision=HIGHEST)
        b_last = b[CHUNK - 1:CHUNK, :]
        q = gq_ref[rows, :].astype(F32)
        k = gk_ref[rows, :].astype(F32)
        q_dec = (q * (GLA_DK ** -0.5) * jnp.exp(b)).astype(BF16)
        k_inv = (k * jnp.exp(-b)).astype(BF16)
        k_rem = (k * jnp.exp(b_last - b)).astype(BF16)
        d_last = jnp.exp(b_last)
        v = gv_ref[rows, :]
        gr = gr_ref[rows, :].astype(F32)
        hn = hn_ref[...]
        for h in range(GLA_HEADS):
            ks = slice(h * GLA_DK, (h + 1) * GLA_DK)
            vs = slice(h * GLA_DV, (h + 1) * GLA_DV)
            qh, vh = q_dec[:, ks], v[:, vs]
            scores = jnp.where(causal, _dot_nt(qh, k_inv[:, ks]), 0.0)
            st = st_ref[h]
            o = _dot(scores.astype(BF16), vh) + _dot_nt(qh, st.astype(BF16))
            ut = lax.dot_general(vh, k_rem[:, ks], TN_DIMS, preferred_element_type=F32)
            st_ref[h] = st * d_last[:, ks] + ut
            y = _rmsnorm(o, hn)
            g = gr[:, vs]
            o_ref[rows, vs] = (y * (g * jax.nn.sigmoid(g))).astype(o_ref.dtype)
        return carry

    lax.fori_loop(0, nchunks, chunk, 0)

    @pl.when(j == pl.num_programs(1) - 1)
    def _():
        for h in range(GLA_HEADS):
            sfin_ref[h] = st_ref[h].T


def _gla_call(gq, gk, gv, gr, misc, wup, bup, hn, s0, tb):
    bsz, seq, _ = gq.shape
    blk = lambda n: pl.BlockSpec((None, tb, n), lambda b, j: (b, j, 0))
    full2 = lambda a: pl.BlockSpec(a.shape, lambda b, j: (0, 0))
    st_spec = pl.BlockSpec((None, GLA_HEADS, GLA_DK, GLA_DV), lambda b, j: (b, 0, 0, 0))
    return pl.pallas_call(
        functools.partial(_gla_kernel, nchunks=tb // CHUNK),
        grid=(bsz, seq // tb),
        in_specs=[blk(gq.shape[2]), blk(gk.shape[2]), blk(gv.shape[2]), blk(gr.shape[2]), blk(misc.shape[2]),
                  full2(wup), full2(bup), full2(hn), st_spec],
        out_specs=[blk(gv.shape[2]), st_spec],
        out_shape=[jax.ShapeDtypeStruct((bsz, seq, gv.shape[2]), BF16),
                   jax.ShapeDtypeStruct(s0.shape, F32)],
        scratch_shapes=[pltpu.VMEM((GLA_HEADS, GLA_DV, GLA_DK), F32)],
        compiler_params=_params("parallel", "arbitrary"),
        name="gla",
    )(gq, gk, gv, gr, misc, wup, bup, hn, s0)


def _dsa_kernel(iq_ref, misc_ref, aq_ref, ikcat_ref, kb_ref, vaug_ref, o_ref,
                keys_ref, hi_ref, lo_ref, bias_ref, s_ref, mx_ref, acc_ref,
                *, tq, tqp, q_start, n_keys, topk):
    qi = pl.program_id(1)
    kc = KEY_CHUNK
    hd = DSA_HEAD_DIM
    hpg = DSA_HEADS // DSA_KV_HEADS

    first_q = q_start + qi * tq
    last_q = first_q + tq - 1
    span = KEY_UNROLL * kc
    nt = (jnp.minimum(n_keys, ((last_q >> 6) + 1) * CHUNK) + span - 1) // span
    nfull = jnp.minimum(n_keys, ((first_q >> 6) + 1) * CHUNK) // span

    def trips(body, masked):
        def trip(t, carry):
            for u in range(KEY_UNROLL):
                carry = body(pl.multiple_of((t * KEY_UNROLL + u) * kc, kc), masked, carry)
            return carry
        return trip

    def for_chunks(body, init):
        return lax.fori_loop(nfull, nt, trips(body, True), lax.fori_loop(0, nfull, trips(body, False), init))

    def for_all_chunks(body, init):
        return lax.fori_loop(0, nt, trips(lambda r0, _, carry: body(r0, carry), None), init)

    def pad_rows(x):
        if tqp == tq:
            return x
        return jnp.concatenate([x, jnp.zeros((tqp - tq, x.shape[1]), x.dtype)], axis=0)

    iq_hi, iq_lo = _split_bf16(pad_rows(iq_ref[...]))
    iq_cat = []
    for h in range(IDX_HEADS):
        s = slice(h * IDX_DIM, (h + 1) * IDX_DIM)
        iq_cat.append(jnp.concatenate([iq_hi[:, s], iq_lo[:, s], iq_hi[:, s]], axis=1))
    wts = pad_rows(misc_ref[...]).T[MISC_IW:MISC_IW + IDX_HEADS, :] * IDX_SCALE
    q_chunk = (first_q + lax.broadcasted_iota(I32, (1, tqp), 1)) >> 6

    def admissible(r0):
        k_pos = r0 + lax.broadcasted_iota(I32, (kc, tqp), 0)
        return ((k_pos >> 6) <= q_chunk) & (k_pos < n_keys)

    def score_body(r0, masked, carry):
        rows = pl.ds(r0, kc)
        ik = ikcat_ref[rows, :]
        acc = jnp.zeros((kc, tqp), F32)
        for h in range(IDX_HEADS):
            acc = acc + wts[h:h + 1, :] * jnp.maximum(_dot_nt(ik, iq_cat[h]), 0.0)
        if masked:
            acc = jnp.where(admissible(r0), acc, -jnp.inf)
        bits = lax.bitcast_convert_type(acc, I32)
        key = jnp.where(bits < 0, bits ^ jnp.int32(0x7FFFFFFF), bits)
        keys_ref[rows, :] = key
        hi_ref[rows, :] = (key >> 16).astype(I16)
        lo_ref[rows, :] = ((key & 0xFFFF) + INT16_MIN).astype(I16)
        return carry

    for_chunks(score_body, 0)

    grp = 32
    one, zero = jnp.asarray(1, BF16), jnp.asarray(0, BF16)

    def count16(ref, pred):
        def body(r0, acc):
            m = jnp.where(pred(ref[pl.ds(r0, kc), :]), one, zero)
            for i in range(kc // grp):
                acc = acc + m[i * grp:(i + 1) * grp, :]
            return acc
        acc = for_all_chunks(body, jnp.zeros((grp, tqp), BF16))
        return jnp.sum(acc.astype(F32), axis=0, keepdims=True)

    def search16(ref, kth):
        def bit_body(i, prefix):
            cand = prefix | (jnp.int32(1) << (15 - i))
            cand16 = (cand + INT16_MIN).astype(I16)
            tot = count16(ref, lambda ch: ch >= cand16)
            return jnp.where(tot >= kth, cand, prefix)
        return lax.fori_loop(0, 16, bit_body, jnp.zeros((1, tqp), I32))

    kth = jnp.full((1, tqp), topk, F32)
    a_u = search16(hi_ref, kth)
    a16 = (a_u + INT16_MIN).astype(I16)
    kth_lo = kth - count16(hi_ref, lambda ch: ch > a16)

    def bucket_body(r0, carry):
        rows = pl.ds(r0, kc)
        lo_ref[rows, :] = jnp.where(hi_ref[rows, :] == a16, lo_ref[rows, :], jnp.asarray(INT16_MIN, I16))
        return carry

    for_all_chunks(bucket_body, 0)
    b_u = search16(lo_ref, kth_lo)
    b16 = (b_u + INT16_MIN).astype(I16)
    thr = ((a_u + INT16_MIN) << 16) | b_u
    need = kth_lo - count16(lo_ref, lambda ch: ch > b16)

    tri = (lax.broadcasted_iota(I32, (kc, kc), 0) >= lax.broadcasted_iota(I32, (kc, kc), 1)).astype(BF16)

    def sel_body(r0, masked, running):
        ch = keys_ref[pl.ds(r0, kc), :]
        eq = ch == thr
        rank = _dot(tri, jnp.where(eq, 1.0, 0.0).astype(BF16)) + running
        sel = (ch > thr) | (eq & (rank <= need))
        if masked:
            sel = sel & admissible(r0)
        bias_ref[:, pl.ds(r0, kc)] = jnp.where(sel, 0.0, -jnp.inf).T[:tq, :]
        return rank[kc - 1:kc, :]

    for_chunks(sel_body, jnp.zeros((1, tqp), F32))

    mx_ref[...] = jnp.full(mx_ref.shape, -jnp.inf, F32)
    acc_ref[...] = jnp.zeros(acc_ref.shape, F32)

    def pass1(g):
        qg = jnp.concatenate([aq_ref[:, h * hd:(h + 1) * hd] for h in range(g * hpg, (g + 1) * hpg)], axis=0)
        qg = qg * jnp.asarray(hd ** -0.5, BF16)

        def body(r0, carry):
            cols = pl.ds(r0, kc)
            s = _dot_nt(qg, kb_ref[cols, g * hd:(g + 1) * hd])
            bias = bias_ref[:, cols]
            for r in range(hpg):
                rows = slice(r * tq, (r + 1) * tq)
                sr = s[rows, :] + bias
                s_ref[rows, cols] = sr
                mx = mx_ref[g, rows, :]
                for i in range(kc // LANES):
                    mx = jnp.maximum(mx, sr[:, i * LANES:(i + 1) * LANES])
                mx_ref[g, rows, :] = mx
            return carry
        return body

    def pass2(g):
        m = jnp.max(mx_ref[g], axis=1, keepdims=True)

        def body(r0, carry):
            cols = pl.ds(r0, kc)
            p = jnp.exp(s_ref[:, cols] - m)
            acc_ref[g] += _dot(p.astype(BF16), vaug_ref[cols, 2 * g * hd:2 * (g + 1) * hd])
            return carry
        return body

    for_all_chunks(pass1(0), 0)
    for g in range(1, DSA_KV_HEADS):
        exp_prev, logits_next = pass2(g - 1), pass1(g)
        for_all_chunks(lambda r0, carry: logits_next(r0, exp_prev(r0, carry)), 0)
    for_all_chunks(pass2(DSA_KV_HEADS - 1), 0)

    for g in range(DSA_KV_HEADS):
        acc = acc_ref[g]
        out = acc[:, :hd] / acc[:, hd:hd + 1]
        for r in range(hpg):
            h = g * hpg + r
            o_ref[:, h * hd:(h + 1) * hd] = out[r * tq:(r + 1) * tq, :].astype(o_ref.dtype)


def _dsa_call(iq, misc, aq, ikcat, kb, vaug, *, q_start, n_keys, tq):
    bsz, lq, _ = iq.shape
    lk = ikcat.shape[1]
    assert lk % (KEY_UNROLL * KEY_CHUNK) == 0 and lq % tq == 0 and lk // 32 <= 256
    tqp = max(tq, LANES)
    topk = min(TOPK_MAX, n_keys // 4)
    rows_g = (DSA_HEADS // DSA_KV_HEADS) * tq
    qblk = lambda n: pl.BlockSpec((None, tq, n), lambda b, q: (b, q, 0))
    kblk = lambda n: pl.BlockSpec((None, lk, n), lambda b, q: (b, 0, 0))
    kern = functools.partial(_dsa_kernel, tq=tq, tqp=tqp, q_start=q_start, n_keys=n_keys, topk=topk)
    return pl.pallas_call(
        kern,
        grid=(bsz, lq // tq),
        in_specs=[qblk(iq.shape[2]), qblk(misc.shape[2]), qblk(aq.shape[2]),
                  kblk(ikcat.shape[2]), kblk(kb.shape[2]), kblk(vaug.shape[2])],
        out_specs=qblk(aq.shape[2]),
        out_shape=jax.ShapeDtypeStruct(aq.shape, BF16),
        scratch_shapes=[pltpu.VMEM((lk, tqp), I32),
                        pltpu.VMEM((lk, tqp), I16),
                        pltpu.VMEM((lk, tqp), I16),
                        pltpu.VMEM((tq, lk), F32),
                        pltpu.VMEM((rows_g, lk), F32),
                        pltpu.VMEM((DSA_KV_HEADS, rows_g, LANES), F32),
                        pltpu.VMEM((DSA_KV_HEADS, rows_g, 2 * DSA_HEAD_DIM), F32)],
        compiler_params=_params("parallel", "arbitrary"),
        name="dsa",
    )(iq, misc, aq, ikcat, kb, vaug)


def _merge_kernel(x_ref, go_ref, do_ref, gate_ref, wa_ref, wb_ref, wo_ref, g_ref, y_ref):
    d = x_ref.shape[-1]
    a = _dot(go_ref[...], wa_ref[...])
    b = _dot(do_ref[...], wb_ref[...])
    gate = gate_ref[...].astype(F32)
    mixed = jax.nn.sigmoid(gate[:, :d]) * a + jax.nn.sigmoid(gate[:, d:]) * b
    y_ref[...] = x_ref[...] + _rmsnorm(_dot(mixed.astype(BF16), wo_ref[...]), g_ref[...])


def _merge_call(x, go, do, gate, wa, wb, wo, g, tm):
    t, d = x.shape
    row = lambda n: pl.BlockSpec((tm, n), lambda i: (i, 0))
    full = lambda a: pl.BlockSpec(a.shape, lambda i: (0, 0))
    return pl.pallas_call(
        _merge_kernel,
        grid=(t // tm,),
        in_specs=[row(d), row(go.shape[1]), row(do.shape[1]), row(gate.shape[1]),
                  full(wa), full(wb), full(wo), full(g)],
        out_specs=row(d),
        out_shape=jax.ShapeDtypeStruct((t, d), F32),
        compiler_params=_params("parallel"),
        name="merge",
    )(x, go, do, gate, wa, wb, wo, g)


def _gelu_tanh(x):
    return 0.5 * x * (1.0 + jnp.tanh(np.sqrt(2.0 / np.pi) * (x + 0.044715 * (x * x * x))))


def _ffn_kernel(x_ref, gpre_ref, wua_ref, wub_ref, cwa_ref, cwb_ref, cba_ref, cbb_ref, pva_ref, pvb_ref,
                wd_ref, gpost_ref, y_ref, cna_ref, cnb_ref, xn_ref, acc_ref, cara_ref, carb_ref, *, bps):
    i = pl.program_id(0)
    j = pl.program_id(1)
    tm, tn = x_ref.shape[0], wua_ref.shape[1]

    @pl.when(j == 0)
    def _():
        xn_ref[...] = _rmsnorm(x_ref[...], gpre_ref[...]).astype(BF16)
        acc_ref[...] = jnp.zeros_like(acc_ref)

    xn = xn_ref[...]
    first = (i % bps) == 0
    rows = lax.broadcasted_iota(I32, (tm, tn), 0)

    def conv_half(w_ref, cw_ref, cb_ref, pv_ref, car_ref, cn_ref):
        u = _dot(xn, w_ref[...])
        prev = pv_ref[...]
        car = car_ref[j]
        h0 = jnp.where(first, prev[0:1, :], car[SUBLANES - 2:SUBLANES - 1, :])
        h1 = jnp.where(first, prev[1:2, :], car[SUBLANES - 1:SUBLANES, :])
        u1 = jnp.where(rows == 0, h1, pltpu.roll(u, 1, 0))
        u2 = jnp.where(rows == 0, h0, jnp.where(rows == 1, h1, pltpu.roll(u, 2, 0)))
        cw = cw_ref[...]
        car_ref[j] = u[tm - SUBLANES:tm, :]
        cn_ref[...] = u[tm - (CONV_W - 1):tm, :]
        return cb_ref[...] + (cw[0:1, :] * u2 + cw[1:2, :] * u1 + cw[2:3, :] * u)

    a = conv_half(wua_ref, cwa_ref, cba_ref, pva_ref, cara_ref, cna_ref)
    b = conv_half(wub_ref, cwb_ref, cbb_ref, pvb_ref, carb_ref, cnb_ref)
    acc_ref[...] += _dot((_gelu_tanh(a) * b).astype(BF16), wd_ref[...])

    @pl.when(j == pl.num_programs(1) - 1)
    def _():
        y_ref[...] = x_ref[...] + _rmsnorm(acc_ref[...], gpost_ref[...])


def _ffn_call(x, gpre, w_up, conv_w, conv_b, conv_prev, w_down, gpost, *, tm, seq, tn):
    t, d = x.shape
    f = w_down.shape[0]
    nseq = t // seq
    bps = seq // tm
    nft = f // tn
    assert f % tn == 0 and seq % tm == 0
    xrow = pl.BlockSpec((tm, d), lambda i, j: (i, 0))
    vec = pl.BlockSpec((1, d), lambda i, j: (0, 0))
    col = lambda r, off: pl.BlockSpec((r, tn), lambda i, j: (0, j + off))
    prev = lambda off: pl.BlockSpec((None, CONV_W - 1, tn), lambda i, j: (i // bps, 0, j + off))
    cn_spec = pl.BlockSpec((None, CONV_W - 1, tn), lambda i, j: (i, 0, j))
    y, cna, cnb = pl.pallas_call(
        functools.partial(_ffn_kernel, bps=bps),
        grid=(t // tm, nft),
        in_specs=[xrow, vec, col(d, 0), col(d, nft), col(CONV_W, 0), col(CONV_W, nft), col(1, 0), col(1, nft),
                  prev(0), prev(nft), pl.BlockSpec((tn, d), lambda i, j: (j, 0)), vec],
        out_specs=[xrow, cn_spec, cn_spec],
        out_shape=[jax.ShapeDtypeStruct((t, d), F32),
                   jax.ShapeDtypeStruct((t // tm, CONV_W - 1, f), F32),
                   jax.ShapeDtypeStruct((t // tm, CONV_W - 1, f), F32)],
        scratch_shapes=[pltpu.VMEM((tm, d), BF16), pltpu.VMEM((tm, d), F32),
                        pltpu.VMEM((nft, SUBLANES, tn), F32), pltpu.VMEM((nft, SUBLANES, tn), F32)],
        compiler_params=_params("arbitrary", "arbitrary"),
        name="ffn",
    )(x, gpre, w_up, w_up, conv_w, conv_w, conv_b, conv_b, conv_prev, conv_prev, w_down, gpost)
    conv_new = jnp.concatenate([cna, cnb], axis=-1).reshape(nseq, bps, CONV_W - 1, 2 * f)[:, bps - 1]
    return y, conv_new


def _prep_weights(w_in, w_gate_up, d_model):
    sizes = (GLA_HEADS * GLA_DK, GLA_HEADS * GLA_DK, GLA_HEADS * GLA_DV, GLA_RANK, GLA_HEADS * GLA_DV,
             DSA_HEADS * DSA_HEAD_DIM, DSA_KV_HEADS * DSA_HEAD_DIM, DSA_KV_HEADS * DSA_HEAD_DIM,
             IDX_HEADS * IDX_DIM, IDX_DIM, IDX_HEADS, 2 * d_model)
    offs = np.concatenate([[0], np.cumsum(sizes)])
    g_q, g_k, g_v, g_low, g_r, a_q, a_k, a_v, i_q, i_k, i_w, gate = (
        w_in[:, int(offs[n]):int(offs[n + 1])] for n in range(len(sizes)))
    w_main = jnp.concatenate([g_q, g_k, g_v, g_r, a_q, a_k, a_v, gate], axis=1).astype(BF16)
    zeros = lambda n: jnp.zeros((w_in.shape[0], n), w_in.dtype)
    misc = jnp.concatenate([g_low, i_w, zeros(LANES - GLA_RANK - IDX_HEADS)], axis=1)
    assert MISC_GLOW == 0 and MISC_IW == GLA_RANK
    w_hp = jnp.concatenate([i_q, i_k, zeros(LANES - IDX_DIM), misc], axis=1)
    assert w_hp.shape[1] == HP_WIDTH
    w_hp_hi, w_hp_lo = _split_bf16(w_hp)
    wup = jnp.concatenate([w_gate_up, jnp.zeros((LANES - GLA_RANK, w_gate_up.shape[1]), w_gate_up.dtype)], axis=0)
    return w_main, w_hp_hi, w_hp_lo, wup


def _run_layer(x, k_past, v_past, kidx_past, gla_s0, conv_prev, w, *, tm, gla_tb, dsa_tq, ffn_tm):
    bsz, seq, d = x.shape
    past = k_past.shape[1]
    t = bsz * seq
    x2 = x.reshape(t, d)
    gq, gk, gv, gr, aq, ak, av, gate, iq, ik, misc, ikcat, kb, vaug = _proj_call(
        x2, w["attn_pre_norm"], w["w_main"], w["w_hp_hi"], w["w_hp_lo"], tm)
    per_seq = lambda a: a.reshape(bsz, seq, a.shape[-1])

    gla_o, gla_s = _gla_call(per_seq(gq), per_seq(gk), per_seq(gv), per_seq(gr), per_seq(misc),
                             w["wup"], w["b_gla_gate"], w["gla_head_norm"], gla_s0, gla_tb)

    n_keys = past + seq
    key_span = KEY_UNROLL * KEY_CHUNK
    lk = -(-n_keys // key_span) * key_span
    packed = [per_seq(a) for a in (ikcat, kb, vaug)]
    if past:
        flat = lambda a: a.reshape(bsz * past, -1).astype(F32)
        packed_past = _pack_call(flat(kidx_past), flat(k_past), flat(v_past), tm)
        packed = [jnp.concatenate([p.reshape(bsz, past, -1), n], axis=1) for p, n in zip(packed_past, packed)]
    if lk > n_keys:
        packed = [jnp.pad(a, ((0, 0), (0, lk - n_keys), (0, 0))) for a in packed]

    dsa_o = _dsa_call(per_seq(iq), per_seq(misc), per_seq(aq), *packed,
                      q_start=past, n_keys=n_keys, tq=dsa_tq)

    x1 = _merge_call(x2, gla_o.reshape(t, -1), dsa_o.reshape(t, -1), gate,
                     w["w_branch_gla"], w["w_branch_dsa"], w["w_out"], w["attn_post_norm"], tm)
    y, conv_new = _ffn_call(x1, w["ffn_pre_norm"], w["w_up"], w["conv_w"], w["conv_b"], conv_prev,
                            w["w_down"], w["ffn_post_norm"], tm=ffn_tm, seq=seq, tn=256)
    k_new = ak.reshape(bsz, seq, DSA_KV_HEADS, DSA_HEAD_DIM)
    v_new = av.reshape(bsz, seq, DSA_KV_HEADS, DSA_HEAD_DIM)
    return y.reshape(bsz, seq, d), k_new, v_new, per_seq(ik), gla_s, conv_new


def kernel(x_prompt, x_sample, cache_k, cache_v, cache_k_idx, state_gla, state_ffn_conv, attn_pre_norm, w_in, w_gla_gate_up, b_gla_gate, gla_head_norm, w_branch_gla, w_branch_dsa, w_out, attn_post_norm, ffn_pre_norm, w_up, conv_w, conv_b, w_down, ffn_post_norm):
    depth = w_in.shape[0]
    bsz, _, d = x_prompt.shape
    dt = x_prompt.dtype
    y_prompt, y_sample = x_prompt, x_sample
    prompt_new, sample_new = [], []
    row = lambda a: a.reshape(1, -1)
    for l in range(depth):
        w_main, w_hp_hi, w_hp_lo, wup = _prep_weights(w_in[l], w_gla_gate_up[l], d)
        w = dict(attn_pre_norm=row(attn_pre_norm[l]), w_main=w_main, w_hp_hi=w_hp_hi, w_hp_lo=w_hp_lo, wup=wup,
                 b_gla_gate=row(b_gla_gate[l]), gla_head_norm=row(gla_head_norm[l]),
                 w_branch_gla=w_branch_gla[l].astype(BF16), w_branch_dsa=w_branch_dsa[l].astype(BF16),
                 w_out=w_out[l].astype(BF16), attn_post_norm=row(attn_post_norm[l]),
                 ffn_pre_norm=row(ffn_pre_norm[l]), w_up=w_up[l].astype(BF16), conv_w=conv_w[l],
                 conv_b=row(conv_b[l]), w_down=w_down[l].astype(BF16), ffn_post_norm=row(ffn_post_norm[l]))
        ffn2 = conv_w.shape[-1]
        y_prompt, *new = _run_layer(
            y_prompt,
            jnp.zeros((bsz, 0, DSA_KV_HEADS * DSA_HEAD_DIM), dt),
            jnp.zeros((bsz, 0, DSA_KV_HEADS * DSA_HEAD_DIM), dt),
            jnp.zeros((bsz, 0, IDX_DIM), dt),
            jnp.zeros((bsz, GLA_HEADS, GLA_DK, GLA_DV), F32),
            jnp.zeros((bsz, CONV_W - 1, ffn2), dt),
            w, tm=256, gla_tb=512, dsa_tq=256, ffn_tm=1024)
        prompt_new.append(new)
        y_sample, *new = _run_layer(
            y_sample, cache_k[l], cache_v[l], cache_k_idx[l], state_gla[l], state_ffn_conv[l],
            w, tm=256, gla_tb=x_sample.shape[1], dsa_tq=x_sample.shape[1], ffn_tm=x_sample.shape[1])
        sample_new.append(new)
    k_p, v_p, kidx_p, gla_p, conv_p = (jnp.stack(t, axis=0) for t in zip(*prompt_new))
    k_s, v_s, kidx_s, gla_s, conv_s = (jnp.stack(t, axis=0) for t in zip(*sample_new))
    return (y_prompt, y_sample, k_p, v_p, kidx_p, gla_p, conv_p, k_s, v_s, kidx_s, gla_s, conv_s)
```

```python
import functools

import numpy as np
import jax
import jax.numpy as jnp
from jax import lax
from jax.experimental import pallas as pl
from jax.experimental.pallas import tpu as pltpu

F32 = jnp.float32
BF16 = jnp.bfloat16
I32 = jnp.int32
I16 = jnp.int16
HIGHEST = lax.Precision.HIGHEST

CHUNK = 64
GLA_HEADS = 4
GLA_DK = 128
GLA_DV = 256
GLA_RANK = 16
GLA_GATE_TAU = 16.0
DSA_HEADS = 8
DSA_KV_HEADS = 2
DSA_HEAD_DIM = 64
IDX_HEADS = 4
IDX_DIM = 64
IDX_SCALE = (IDX_HEADS * IDX_DIM) ** -0.5
TOPK_MAX = 256
CONV_W = 3
EPS = 1e-6

LANES = 128
SUBLANES = 8
VMEM_LIMIT = 56 * 1024 * 1024

MISC_GLOW = 0
MISC_IW = 16

KEY_CHUNK = 256
KEY_UNROLL = 2
INT16_MIN = -(2 ** 15)
NT_DIMS = (((1,), (1,)), ((), ()))
TN_DIMS = (((0,), (0,)), ((), ()))


def _dot(a, b, **kw):
    return jnp.dot(a, b, preferred_element_type=F32, **kw)


def _dot_nt(a, b):
    return lax.dot_general(a, b, NT_DIMS, preferred_element_type=F32)


def _split_bf16(x):
    hi = x.astype(BF16)
    lo = (x - hi.astype(F32)).astype(BF16)
    return hi, lo


def _rmsnorm(x, g):
    return x * lax.rsqrt(jnp.mean(x * x, axis=-1, keepdims=True) + EPS) * g


def _params(*sem):
    return pltpu.CompilerParams(dimension_semantics=sem, vmem_limit_bytes=VMEM_LIMIT)


MAIN_SIZES = (GLA_HEADS * GLA_DK, GLA_HEADS * GLA_DK, GLA_HEADS * GLA_DV, GLA_HEADS * GLA_DV,
              DSA_HEADS * DSA_HEAD_DIM, DSA_KV_HEADS * DSA_HEAD_DIM, DSA_KV_HEADS * DSA_HEAD_DIM)
HP_WIDTH = 512
PACK_WIDTHS = (3 * IDX_DIM, DSA_KV_HEADS * DSA_HEAD_DIM, 2 * DSA_KV_HEADS * DSA_HEAD_DIM)


def _pack_keys(ik, k, v):
    hi, lo = _split_bf16(ik)
    ikcat = jnp.concatenate([hi, hi, lo], axis=1)
    vb = v.astype(BF16)
    ones = jnp.ones((v.shape[0], DSA_HEAD_DIM), BF16)
    parts = []
    for g in range(DSA_KV_HEADS):
        parts += [vb[:, g * DSA_HEAD_DIM:(g + 1) * DSA_HEAD_DIM], ones]
    return ikcat, k.astype(BF16), jnp.concatenate(parts, axis=1)


def _proj_kernel(x_ref, g_ref, wm_ref, whi_ref, wlo_ref,
                 gq_ref, gk_ref, gv_ref, gr_ref, aq_ref, ak_ref, av_ref, gate_ref, iq_ref, ik_ref, misc_ref,
                 ikcat_ref, kb_ref, vaug_ref):
    xn = _rmsnorm(x_ref[...], g_ref[...])
    xh, xl = _split_bf16(xn)
    outs = (gq_ref, gk_ref, gv_ref, gr_ref, aq_ref, ak_ref, av_ref, gate_ref)
    off = 0
    vals = []
    for o_ref in outs:
        n = o_ref.shape[-1]
        vals.append(_dot(xh, wm_ref[:, off:off + n]))
        o_ref[...] = vals[-1].astype(o_ref.dtype)
        off += n
    ak, av = vals[5], vals[6]
    whi = whi_ref[...]
    hp = _dot(xh, whi) + _dot(xl, whi) + _dot(xh, wlo_ref[...])
    iq_ref[...] = hp[:, 0:256]
    ik_ref[...] = hp[:, 256:320]
    misc_ref[...] = hp[:, 384:512]
    ikcat_ref[...], kb_ref[...], vaug_ref[...] = _pack_keys(hp[:, 256:320], ak, av)


def _proj_call(x, g, wm, whi, wlo, tm):
    t, d = x.shape
    gate_w = wm.shape[1] - sum(MAIN_SIZES)
    widths = MAIN_SIZES + (gate_w,)
    dtypes = (BF16, BF16, BF16, BF16, BF16, F32, F32, BF16)
    out_shape = [jax.ShapeDtypeStruct((t, n), dt) for n, dt in zip(widths, dtypes)]
    out_shape += [jax.ShapeDtypeStruct((t, n), F32) for n in (256, 64, 128)]
    out_shape += [jax.ShapeDtypeStruct((t, n), BF16) for n in PACK_WIDTHS]
    row = lambda n: pl.BlockSpec((tm, n), lambda i: (i, 0))
    full = lambda a: pl.BlockSpec(a.shape, lambda i: (0, 0))
    return pl.pallas_call(
        _proj_kernel,
        grid=(t // tm,),
        in_specs=[row(d), full(g), full(wm), full(whi), full(wlo)],
        out_specs=[row(s.shape[1]) for s in out_shape],
        out_shape=out_shape,
        compiler_params=_params("parallel"),
        name="proj",
    )(x, g, wm, whi, wlo)


def _pack_kernel(ik_ref, k_ref, v_ref, ikcat_ref, kb_ref, vaug_ref):
    ikcat_ref[...], kb_ref[...], vaug_ref[...] = _pack_keys(ik_ref[...], k_ref[...], v_ref[...])


def _pack_call(ik, k, v, tm):
    t = ik.shape[0]
    row = lambda n: pl.BlockSpec((tm, n), lambda i: (i, 0))
    return pl.pallas_call(
        _pack_kernel,
        grid=(t // tm,),
        in_specs=[row(ik.shape[1]), row(k.shape[1]), row(v.shape[1])],
        out_specs=[row(n) for n in PACK_WIDTHS],
        out_shape=[jax.ShapeDtypeStruct((t, n), BF16) for n in PACK_WIDTHS],
        compiler_params=_params("parallel"),
        name="pack",
    )(ik, k, v)


def _gla_kernel(gq_ref, gk_ref, gv_ref, gr_ref, misc_ref, wup_ref, bup_ref, hn_ref, s0_ref,
                o_ref, sfin_ref, st_ref, *, nchunks):
    j = pl.program_id(1)

    @pl.when(j == 0)
    def _():
        for h in range(GLA_HEADS):
            st_ref[h] = s0_ref[h].T

    row = lax.broadcasted_iota(I32, (CHUNK, CHUNK), 0)
    col = lax.broadcasted_iota(I32, (CHUNK, CHUNK), 1)
    causal = row >= col
    tri = causal.astype(F32)

    def chunk(c, carry):
        r0 = pl.multiple_of(c * CHUNK, CHUNK)
        rows = pl.ds(r0, CHUNK)
        z = _dot(misc_ref[rows, :], wup_ref[...], precision=HIGHEST) + bup_ref[...]
        log_a = (jnp.minimum(z, 0.0) - jnp.log(1.0 + jnp.exp(-jnp.abs(z)))) * (1.0 / GLA_GATE_TAU)
        b = _dot(tri, log_a, precision=HIGHEST)
        b_last = b[CHUNK - 1:CHUNK, :]
        q = gq_ref[rows, :].astype(F32)
        k = gk_ref[rows, :].astype(F32)
        q_dec = (q * (GLA_DK ** -0.5) * jnp.exp(b)).astype(BF16)
        k_inv = (k * jnp.exp(-b)).astype(BF16)
        k_rem = (k * jnp.exp(b_last - b)).astype(BF16)
        d_last = jnp.exp(b_last)
        v = gv_ref[rows, :]
        gr = gr_ref[rows, :].astype(F32)
        hn = hn_ref[...]
        for h in range(GLA_HEADS):
            ks = slice(h * GLA_DK, (h + 1) * GLA_DK)
            vs = slice(h * GLA_DV, (h + 1) * GLA_DV)
            qh, vh = q_dec[:, ks], v[:, vs]
            scores = jnp.where(causal, _dot_nt(qh, k_inv[:, ks]), 0.0)
            st = st_ref[h]
            o = _dot(scores.astype(BF16), vh) + _dot_nt(qh, st.astype(BF16))
            ut = lax.dot_general(vh, k_rem[:, ks], TN_DIMS, preferred_element_type=F32)
            st_ref[h] = st * d_last[:, ks] + ut
            y = _rmsnorm(o, hn)
            g = gr[:, vs]
            o_ref[rows, vs] = (y * (g * jax.nn.sigmoid(g))).astype(o_ref.dtype)
        return carry

    lax.fori_loop(0, nchunks, chunk, 0)

    @pl.when(j == pl.num_programs(1) - 1)
    def _():
        for h in range(GLA_HEADS):
            sfin_ref[h] = st_ref[h].T


def _gla_call(gq, gk, gv, gr, misc, wup, bup, hn, s0, tb):
    bsz, seq, _ = gq.shape
    blk = lambda n: pl.BlockSpec((None, tb, n), lambda b, j: (b, j, 0))
    full2 = lambda a: pl.BlockSpec(a.shape, lambda b, j: (0, 0))
    st_spec = pl.BlockSpec((None, GLA_HEADS, GLA_DK, GLA_DV), lambda b, j: (b, 0, 0, 0))
    return pl.pallas_call(
        functools.partial(_gla_kernel, nchunks=tb // CHUNK),
        grid=(bsz, seq // tb),
        in_specs=[blk(gq.shape[2]), blk(gk.shape[2]), blk(gv.shape[2]), blk(gr.shape[2]), blk(misc.shape[2]),
                  full2(wup), full2(bup), full2(hn), st_spec],
        out_specs=[blk(gv.shape[2]), st_spec],
        out_shape=[jax.ShapeDtypeStruct((bsz, seq, gv.shape[2]), BF16),
                   jax.ShapeDtypeStruct(s0.shape, F32)],
        scratch_shapes=[pltpu.VMEM((GLA_HEADS, GLA_DV, GLA_DK), F32)],
        compiler_params=_params("parallel", "arbitrary"),
        name="gla",
    )(gq, gk, gv, gr, misc, wup, bup, hn, s0)


def _dsa_kernel(iq_ref, misc_ref, aq_ref, ikcat_ref, kb_ref, vaug_ref, o_ref,
                keys_ref, hi_ref, lo_ref, bias_ref, s_ref, mx_ref, acc_ref,
                *, tq, tqp, q_start, n_keys, topk):
    qi = pl.program_id(1)
    kc = KEY_CHUNK
    hd = DSA_HEAD_DIM
    hpg = DSA_HEADS // DSA_KV_HEADS

    first_q = q_start + qi * tq
    last_q = first_q + tq - 1
    span = KEY_UNROLL * kc
    nt = (jnp.minimum(n_keys, ((last_q >> 6) + 1) * CHUNK) + span - 1) // span
    nfull = jnp.minimum(n_keys, ((first_q >> 6) + 1) * CHUNK) // span

    def trips(body, masked):
        def trip(t, carry):
            for u in range(KEY_UNROLL):
                carry = body(pl.multiple_of((t * KEY_UNROLL + u) * kc, kc), masked, carry)
            return carry
        return trip

    def for_chunks(body, init):
        return lax.fori_loop(nfull, nt, trips(body, True), lax.fori_loop(0, nfull, trips(body, False), init))

    def for_all_chunks(body, init):
        return lax.fori_loop(0, nt, trips(lambda r0, _, carry: body(r0, carry), None), init)

    def pad_rows(x):
        if tqp == tq:
            return x
        return jnp.concatenate([x, jnp.zeros((tqp - tq, x.shape[1]), x.dtype)], axis=0)

    iq_hi, iq_lo = _split_bf16(pad_rows(iq_ref[...]))
    iq_cat = []
    for h in range(IDX_HEADS):
        s = slice(h * IDX_DIM, (h + 1) * IDX_DIM)
        iq_cat.append(jnp.concatenate([iq_hi[:, s], iq_lo[:, s], iq_hi[:, s]], axis=1))
    wts = pad_rows(misc_ref[...]).T[MISC_IW:MISC_IW + IDX_HEADS, :] * IDX_SCALE
    q_chunk = (first_q + lax.broadcasted_iota(I32, (1, tqp), 1)) >> 6

    def admissible(r0):
        k_pos = r0 + lax.broadcasted_iota(I32, (kc, tqp), 0)
        return ((k_pos >> 6) <= q_chunk) & (k_pos < n_keys)

    def score_body(r0, masked, carry):
        rows = pl.ds(r0, kc)
        ik = ikcat_ref[rows, :]
        acc = jnp.zeros((kc, tqp), F32)
        for h in range(IDX_HEADS):
            acc = acc + wts[h:h + 1, :] * jnp.maximum(_dot_nt(ik, iq_cat[h]), 0.0)
        if masked:
            acc = jnp.where(admissible(r0), acc, -jnp.inf)
        bits = lax.bitcast_convert_type(acc, I32)
        key = jnp.where(bits < 0, bits ^ jnp.int32(0x7FFFFFFF), bits)
        keys_ref[rows, :] = key
        hi_ref[rows, :] = (key >> 16).astype(I16)
        lo_ref[rows, :] = ((key & 0xFFFF) + INT16_MIN).astype(I16)
        return carry

    for_chunks(score_body, 0)

    grp = 32
    one, zero = jnp.asarray(1, BF16), jnp.asarray(0, BF16)

    def count16(ref, pred):
        def body(r0, acc):
            m = jnp.where(pred(ref[pl.ds(r0, kc), :]), one, zero)
            for i in range(kc // grp):
                acc = acc + m[i * grp:(i + 1) * grp, :]
            return acc
        acc = for_all_chunks(body, jnp.zeros((grp, tqp), BF16))
        return jnp.sum(acc.astype(F32), axis=0, keepdims=True)

    def search16(ref, kth):
        def bit_body(i, prefix):
            cand = prefix | (jnp.int32(1) << (15 - i))
            cand16 = (cand + INT16_MIN).astype(I16)
            tot = count16(ref, lambda ch: ch >= cand16)
            return jnp.where(tot >= kth, cand, prefix)
        return lax.fori_loop(0, 16, bit_body, jnp.zeros((1, tqp), I32))

    kth = jnp.full((1, tqp), topk, F32)
    a_u = search16(hi_ref, kth)
    a16 = (a_u + INT16_MIN).astype(I16)
    kth_lo = kth - count16(hi_ref, lambda ch: ch > a16)

    def bucket_body(r0, carry):
        rows = pl.ds(r0, kc)
        lo_ref[rows, :] = jnp.where(hi_ref[rows, :] == a16, lo_ref[rows, :], jnp.asarray(INT16_MIN, I16))
        return carry

    for_all_chunks(bucket_body, 0)
    b_u = search16(lo_ref, kth_lo)
    b16 = (b_u + INT16_MIN).astype(I16)
    thr = ((a_u + INT16_MIN) << 16) | b_u
    need = kth_lo - count16(lo_ref, lambda ch: ch > b16)

    tri = (lax.broadcasted_iota(I32, (kc, kc), 0) >= lax.broadcasted_iota(I32, (kc, kc), 1)).astype(BF16)

    def sel_body(r0, masked, running):
        ch = keys_ref[pl.ds(r0, kc), :]
        eq = ch == thr
        rank = _dot(tri, jnp.where(eq, 1.0, 0.0).astype(BF16)) + running
        sel = (ch > thr) | (eq & (rank <= need))
        if masked:
            sel = sel & admissible(r0)
        bias_ref[:, pl.ds(r0, kc)] = jnp.where(sel, 0.0, -jnp.inf).T[:tq, :]
        return rank[kc - 1:kc, :]

    for_chunks(sel_body, jnp.zeros((1, tqp), F32))

    mx_ref[...] = jnp.full(mx_ref.shape, -jnp.inf, F32)
    acc_ref[...] = jnp.zeros(acc_ref.shape, F32)

    def pass1(g):
        qg = jnp.concatenate([aq_ref[:, h * hd:(h + 1) * hd] for h in range(g * hpg, (g + 1) * hpg)], axis=0)
        qg = qg * jnp.asarray(hd ** -0.5, BF16)

        def body(r0, carry):
            cols = pl.ds(r0, kc)
            s = _dot_nt(qg, kb_ref[cols, g * hd:(g + 1) * hd])
            bias = bias_ref[:, cols]
            for r in range(hpg):
                rows = slice(r * tq, (r + 1) * tq)
                sr = s[rows, :] + bias
                s_ref[rows, cols] = sr
                mx = mx_ref[g, rows, :]
                for i in range(kc // LANES):
                    mx = jnp.maximum(mx, sr[:, i * LANES:(i + 1) * LANES])
                mx_ref[g, rows, :] = mx
            return carry
        return body

    def pass2(g):
        m = jnp.max(mx_ref[g], axis=1, keepdims=True)

        def body(r0, carry):
            cols = pl.ds(r0, kc)
            p = jnp.exp(s_ref[:, cols] - m)
            acc_ref[g] += _dot(p.astype(BF16), vaug_ref[cols, 2 * g * hd:2 * (g + 1) * hd])
            return carry
        return body

    for_all_chunks(pass1(0), 0)
    for g in range(1, DSA_KV_HEADS):
        exp_prev, logits_next = pass2(g - 1), pass1(g)
        for_all_chunks(lambda r0, carry: logits_next(r0, exp_prev(r0, carry)), 0)
    for_all_chunks(pass2(DSA_KV_HEADS - 1), 0)

    for g in range(DSA_KV_HEADS):
        acc = acc_ref[g]
        out = acc[:, :hd] / acc[:, hd:hd + 1]
        for r in range(hpg):
            h = g * hpg + r
            o_ref[:, h * hd:(h + 1) * hd] = out[r * tq:(r + 1) * tq, :].astype(o_ref.dtype)


def _dsa_call(iq, misc, aq, ikcat, kb, vaug, *, q_start, n_keys, tq):
    bsz, lq, _ = iq.shape
    lk = ikcat.shape[1]
    assert lk % (KEY_UNROLL * KEY_CHUNK) == 0 and lq % tq == 0 and lk // 32 <= 256
    tqp = max(tq, LANES)
    topk = min(TOPK_MAX, n_keys // 4)
    rows_g = (DSA_HEADS // DSA_KV_HEADS) * tq
    qblk = lambda n: pl.BlockSpec((None, tq, n), lambda b, q: (b, q, 0))
    kblk = lambda n: pl.BlockSpec((None, lk, n), lambda b, q: (b, 0, 0))
    kern = functools.partial(_dsa_kernel, tq=tq, tqp=tqp, q_start=q_start, n_keys=n_keys, topk=topk)
    return pl.pallas_call(
        kern,
        grid=(bsz, lq // tq),
        in_specs=[qblk(iq.shape[2]), qblk(misc.shape[2]), qblk(aq.shape[2]),
                  kblk(ikcat.shape[2]), kblk(kb.shape[2]), kblk(vaug.shape[2])],
        out_specs=qblk(aq.shape[2]),
        out_shape=jax.ShapeDtypeStruct(aq.shape, BF16),
        scratch_shapes=[pltpu.VMEM((lk, tqp), I32),
                        pltpu.VMEM((lk, tqp), I16),
                        pltpu.VMEM((lk, tqp), I16),
                        pltpu.VMEM((tq, lk), F32),
                        pltpu.VMEM((rows_g, lk), F32),
                        pltpu.VMEM((DSA_KV_HEADS, rows_g, LANES), F32),
                        pltpu.VMEM((DSA_KV_HEADS, rows_g, 2 * DSA_HEAD_DIM), F32)],
        compiler_params=_params("parallel", "arbitrary"),
        name="dsa",
    )(iq, misc, aq, ikcat, kb, vaug)


def _merge_kernel(x_ref, go_ref, do_ref, gate_ref, wa_ref, wb_ref, wo_ref, g_ref, y_ref):
    d = x_ref.shape[-1]
    a = _dot(go_ref[...], wa_ref[...])
    b = _dot(do_ref[...], wb_ref[...])
    gate = gate_ref[...].astype(F32)
    mixed = jax.nn.sigmoid(gate[:, :d]) * a + jax.nn.sigmoid(gate[:, d:]) * b
    y_ref[...] = x_ref[...] + _rmsnorm(_dot(mixed.astype(BF16), wo_ref[...]), g_ref[...])


def _merge_call(x, go, do, gate, wa, wb, wo, g, tm):
    t, d = x.shape
    row = lambda n: pl.BlockSpec((tm, n), lambda i: (i, 0))
    full = lambda a: pl.BlockSpec(a.shape, lambda i: (0, 0))
    return pl.pallas_call(
        _merge_kernel,
        grid=(t // tm,),
        in_specs=[row(d), row(go.shape[1]), row(do.shape[1]), row(gate.shape[1]),
                  full(wa), full(wb), full(wo), full(g)],
        out_specs=row(d),
        out_shape=jax.ShapeDtypeStruct((t, d), F32),
        compiler_params=_params("parallel"),
        name="merge",
    )(x, go, do, gate, wa, wb, wo, g)


FFN_ROW_CHUNK = 256
FFN_SLAB_ROWS = 64
GELU_C0 = float(np.sqrt(2.0 / np.pi))
GELU_C1 = 0.044715 * GELU_C0


def _ffn_kernel(x_ref, gpre_ref, wu_ref, cw_ref, cb_ref, pv_ref, wdh_ref, gpost_ref,
                y_ref, cn_ref, xn_ref, acc_ref, car_ref, act_ref, u_ref, *, bps, tn):
    i = pl.program_id(0)
    tm = x_ref.shape[0]
    f = wdh_ref.shape[0]
    xn_ref[...] = _rmsnorm(x_ref[...], gpre_ref[...]).astype(BF16)
    acc_ref[...] = jnp.zeros_like(acc_ref)
    first = (i % bps) == 0
    top = lax.broadcasted_iota(I32, (SUBLANES, tn), 0)
    rc = min(tm, FFN_ROW_CHUNK)

    def up(cols, r):
        return _dot(xn_ref[r * rc:(r + 1) * rc, :], wu_ref[:, cols])

    def conv_slab(cols, u, halo):
        r1, r2 = pltpu.roll(u, 1, 0), pltpu.roll(u, 2, 0)
        top1 = jnp.where(top == 0, halo[1:2, :], r1[:SUBLANES, :])
        top2 = jnp.where(top == 0, halo[0:1, :], jnp.where(top == 1, halo[1:2, :], r2[:SUBLANES, :]))
        u1 = jnp.concatenate([top1, r1[SUBLANES:, :]], axis=0)
        u2 = jnp.concatenate([top2, r2[SUBLANES:, :]], axis=0)
        cw = cw_ref[:, cols]
        return cb_ref[:, cols] + (cw[0:1, :] * u2 + cw[1:2, :] * u1 + cw[2:3, :] * u), u[u.shape[0] - 2:, :]

    def keep_tail(cols, slot, half):
        car_ref[:, cols] = u_ref[slot, half, rc - SUBLANES:rc, :]
        cn_ref[:, cols] = u_ref[slot, half, rc - (CONV_W - 1):rc, :]

    def block_halo(cols):
        return jnp.where(first, pv_ref[:, cols], car_ref[SUBLANES - 2:SUBLANES, cols])

    nft = f // tn
    nrc = tm // rc

    def tile(j, aslot, with_down):
        cols_a = pl.ds(pl.multiple_of(j * tn, tn), tn)
        cols_b = pl.ds(pl.multiple_of(f + j * tn, tn), tn)
        halo_a, halo_b = block_halo(cols_a), block_halo(cols_b)

        def up_to(slot, r):
            u_ref[slot, 0] = up(cols_a, r)
            u_ref[slot, 1] = up(cols_b, r)

        up_to(0, 0)
        for r in range(nrc):
            rows = slice(r * rc, (r + 1) * rc)
            slot = r % 2
            if r + 1 < nrc:
                up_to(1 - slot, r + 1)
            if with_down:
                project_down(j - 1, 1 - aslot, rows)
            if r + 1 == nrc:
                keep_tail(cols_a, slot, 0)
                keep_tail(cols_b, slot, 1)
            sl = min(rc, FFN_SLAB_ROWS)
            for s in range(0, rc, sl):
                a, halo_a = conv_slab(cols_a, u_ref[slot, 0, s:s + sl, :], halo_a)
                b, halo_b = conv_slab(cols_b, u_ref[slot, 1, s:s + sl, :], halo_b)
                act = (a * (1.0 + jnp.tanh(a * (GELU_C0 + GELU_C1 * (a * a))))) * b
                act_ref[aslot, r * rc + s:r * rc + s + sl, :] = act.astype(BF16)

    def project_down(j, aslot, rows):
        acc_ref[rows, :] += _dot(act_ref[aslot, rows, :], wdh_ref[pl.ds(pl.multiple_of(j * tn, tn), tn), :])

    assert nft % 2 == 1
    tile(0, 0, False)

    def trip(p, carry):
        tile(2 * p + 1, 1, True)
        tile(2 * p + 2, 0, True)
        return carry

    lax.fori_loop(0, nft // 2, trip, 0)
    for r in range(nrc):
        project_down(nft - 1, 0, slice(r * rc, (r + 1) * rc))
    y_ref[...] = x_ref[...] + _rmsnorm(acc_ref[...], gpost_ref[...])


def _ffn_call(x, gpre, w_up, conv_w, conv_b, conv_prev, w_down_half, gpost, *, tm, seq, tn):
    t, d = x.shape
    f = w_down_half.shape[0]
    nseq = t // seq
    bps = seq // tm
    assert f % tn == 0 and seq % tm == 0
    xrow = pl.BlockSpec((tm, d), lambda i: (i, 0))
    const = lambda a: pl.BlockSpec(a.shape, lambda i: (0, 0), pipeline_mode=pl.Buffered(1))
    y, cn = pl.pallas_call(
        functools.partial(_ffn_kernel, bps=bps, tn=tn),
        grid=(t // tm,),
        in_specs=[xrow, const(gpre), const(w_up), const(conv_w), const(conv_b),
                  pl.BlockSpec((None, CONV_W - 1, 2 * f), lambda i: (i // bps, 0, 0)),
                  const(w_down_half), const(gpost)],
        out_specs=[xrow, pl.BlockSpec((None, CONV_W - 1, 2 * f), lambda i: (i, 0, 0))],
        out_shape=[jax.ShapeDtypeStruct((t, d), F32),
                   jax.ShapeDtypeStruct((t // tm, CONV_W - 1, 2 * f), F32)],
        scratch_shapes=[pltpu.VMEM((tm, d), BF16), pltpu.VMEM((tm, d), F32),
                        pltpu.VMEM((SUBLANES, 2 * f), F32), pltpu.VMEM((2, tm, tn), BF16),
                        pltpu.VMEM((2, 2, min(tm, FFN_ROW_CHUNK), tn), F32)],
        compiler_params=_params("arbitrary"),
        name="ffn",
    )(x, gpre, w_up, conv_w, conv_b, conv_prev, w_down_half, gpost)
    return y, cn.reshape(nseq, bps, CONV_W - 1, 2 * f)[:, bps - 1]


def _prep_weights(w_in, w_gate_up, d_model):
    sizes = (GLA_HEADS * GLA_DK, GLA_HEADS * GLA_DK, GLA_HEADS * GLA_DV, GLA_RANK, GLA_HEADS * GLA_DV,
             DSA_HEADS * DSA_HEAD_DIM, DSA_KV_HEADS * DSA_HEAD_DIM, DSA_KV_HEADS * DSA_HEAD_DIM,
             IDX_HEADS * IDX_DIM, IDX_DIM, IDX_HEADS, 2 * d_model)
    offs = np.concatenate([[0], np.cumsum(sizes)])
    g_q, g_k, g_v, g_low, g_r, a_q, a_k, a_v, i_q, i_k, i_w, gate = (
        w_in[:, int(offs[n]):int(offs[n + 1])] for n in range(len(sizes)))
    w_main = jnp.concatenate([g_q, g_k, g_v, g_r, a_q, a_k, a_v, gate], axis=1).astype(BF16)
    zeros = lambda n: jnp.zeros((w_in.shape[0], n), w_in.dtype)
    misc = jnp.concatenate([g_low, i_w, zeros(LANES - GLA_RANK - IDX_HEADS)], axis=1)
    assert MISC_GLOW == 0 and MISC_IW == GLA_RANK
    w_hp = jnp.concatenate([i_q, i_k, zeros(LANES - IDX_DIM), misc], axis=1)
    assert w_hp.shape[1] == HP_WIDTH
    w_hp_hi, w_hp_lo = _split_bf16(w_hp)
    wup = jnp.concatenate([w_gate_up, jnp.zeros((LANES - GLA_RANK, w_gate_up.shape[1]), w_gate_up.dtype)], axis=0)
    return w_main, w_hp_hi, w_hp_lo, wup


def _run_layer(x, k_past, v_past, kidx_past, gla_s0, conv_prev, w, *, tm, gla_tb, dsa_tq, ffn_tm):
    bsz, seq, d = x.shape
    past = k_past.shape[1]
    t = bsz * seq
    x2 = x.reshape(t, d)
    gq, gk, gv, gr, aq, ak, av, gate, iq, ik, misc, ikcat, kb, vaug = _proj_call(
        x2, w["attn_pre_norm"], w["w_main"], w["w_hp_hi"], w["w_hp_lo"], tm)
    per_seq = lambda a: a.reshape(bsz, seq, a.shape[-1])

    gla_o, gla_s = _gla_call(per_seq(gq), per_seq(gk), per_seq(gv), per_seq(gr), per_seq(misc),
                             w["wup"], w["b_gla_gate"], w["gla_head_norm"], gla_s0, gla_tb)

    n_keys = past + seq
    key_span = KEY_UNROLL * KEY_CHUNK
    lk = -(-n_keys // key_span) * key_span
    packed = [per_seq(a) for a in (ikcat, kb, vaug)]
    if past:
        flat = lambda a: a.reshape(bsz * past, -1).astype(F32)
        packed_past = _pack_call(flat(kidx_past), flat(k_past), flat(v_past), tm)
        packed = [jnp.concatenate([p.reshape(bsz, past, -1), n], axis=1) for p, n in zip(packed_past, packed)]
    if lk > n_keys:
        packed = [jnp.pad(a, ((0, 0), (0, lk - n_keys), (0, 0))) for a in packed]

    dsa_o = _dsa_call(per_seq(iq), per_seq(misc), per_seq(aq), *packed,
                      q_start=past, n_keys=n_keys, tq=dsa_tq)

    x1 = _merge_call(x2, gla_o.reshape(t, -1), dsa_o.reshape(t, -1), gate,
                     w["w_branch_gla"], w["w_branch_dsa"], w["w_out"], w["attn_post_norm"], tm)
    y, conv_new = _ffn_call(x1, w["ffn_pre_norm"], w["w_up"], w["conv_w"], w["conv_b"], conv_prev,
                            w["w_down_half"], w["ffn_post_norm"], tm=ffn_tm, seq=seq, tn=256)
    k_new = ak.reshape(bsz, seq, DSA_KV_HEADS, DSA_HEAD_DIM)
    v_new = av.reshape(bsz, seq, DSA_KV_HEADS, DSA_HEAD_DIM)
    return y.reshape(bsz, seq, d), k_new, v_new, per_seq(ik), gla_s, conv_new


def kernel(x_prompt, x_sample, cache_k, cache_v, cache_k_idx, state_gla, state_ffn_conv, attn_pre_norm, w_in, w_gla_gate_up, b_gla_gate, gla_head_norm, w_branch_gla, w_branch_dsa, w_out, attn_post_norm, ffn_pre_norm, w_up, conv_w, conv_b, w_down, ffn_post_norm):
    depth = w_in.shape[0]
    bsz, _, d = x_prompt.shape
    dt = x_prompt.dtype
    y_prompt, y_sample = x_prompt, x_sample
    prompt_new, sample_new = [], []
    row = lambda a: a.reshape(1, -1)
    for l in range(depth):
        w_main, w_hp_hi, w_hp_lo, wup = _prep_weights(w_in[l], w_gla_gate_up[l], d)
        w = dict(attn_pre_norm=row(attn_pre_norm[l]), w_main=w_main, w_hp_hi=w_hp_hi, w_hp_lo=w_hp_lo, wup=wup,
                 b_gla_gate=row(b_gla_gate[l]), gla_head_norm=row(gla_head_norm[l]),
                 w_branch_gla=w_branch_gla[l].astype(BF16), w_branch_dsa=w_branch_dsa[l].astype(BF16),
                 w_out=w_out[l].astype(BF16), attn_post_norm=row(attn_post_norm[l]),
                 ffn_pre_norm=row(ffn_pre_norm[l]), w_up=w_up[l].astype(BF16), conv_w=conv_w[l],
                 conv_b=row(conv_b[l]), w_down_half=(0.5 * w_down[l]).astype(BF16),
                 ffn_post_norm=row(ffn_post_norm[l]))
        ffn2 = conv_w.shape[-1]
        y_prompt, *new = _run_layer(
            y_prompt,
            jnp.zeros((bsz, 0, DSA_KV_HEADS * DSA_HEAD_DIM), dt),
            jnp.zeros((bsz, 0, DSA_KV_HEADS * DSA_HEAD_DIM), dt),
            jnp.zeros((bsz, 0, IDX_DIM), dt),
            jnp.zeros((bsz, GLA_HEADS, GLA_DK, GLA_DV), F32),
            jnp.zeros((bsz, CONV_W - 1, ffn2), dt),
            w, tm=256, gla_tb=512, dsa_tq=256, ffn_tm=1024)
        prompt_new.append(new)
        y_sample, *new = _run_layer(
            y_sample, cache_k[l], cache_v[l], cache_k_idx[l], state_gla[l], state_ffn_conv[l],
            w, tm=256, gla_tb=x_sample.shape[1], dsa_tq=x_sample.shape[1], ffn_tm=x_sample.shape[1])
        sample_new.append(new)
    k_p, v_p, kidx_p, gla_p, conv_p = (jnp.stack(t, axis=0) for t in zip(*prompt_new))
    k_s, v_s, kidx_s, gla_s, conv_s = (jnp.stack(t, axis=0) for t in zip(*sample_new))
    return (y_prompt, y_sample, k_p, v_p, kidx_p, gla_p, conv_p, k_s, v_s, kidx_s, gla_s, conv_s)
```

```python
import functools

import numpy as np
import jax
import jax.numpy as jnp
from jax import lax
from jax.experimental import pallas as pl
from jax.experimental.pallas import tpu as pltpu

F32 = jnp.float32
BF16 = jnp.bfloat16
I32 = jnp.int32
I16 = jnp.int16
HIGHEST = lax.Precision.HIGHEST

CHUNK = 64
GLA_HEADS = 4
GLA_DK = 128
GLA_DV = 256
GLA_RANK = 16
GLA_GATE_TAU = 16.0
DSA_HEADS = 8
DSA_KV_HEADS = 2
DSA_HEAD_DIM = 64
IDX_HEADS = 4
IDX_DIM = 64
IDX_SCALE = (IDX_HEADS * IDX_DIM) ** -0.5
TOPK_MAX = 256
CONV_W = 3
EPS = 1e-6

LANES = 128
SUBLANES = 8
VMEM_LIMIT = 56 * 1024 * 1024

MISC_GLOW = 0
MISC_IW = 16

KEY_CHUNK = 256
KEY_UNROLL = 2
GLA_UNROLL = 8
INT16_MIN = -(2 ** 15)
NT_DIMS = (((1,), (1,)), ((), ()))
TN_DIMS = (((0,), (0,)), ((), ()))


def _dot(a, b, **kw):
    return jnp.dot(a, b, preferred_element_type=F32, **kw)


def _dot_nt(a, b):
    return lax.dot_general(a, b, NT_DIMS, preferred_element_type=F32)


def _split_bf16(x):
    hi = x.astype(BF16)
    lo = (x - hi.astype(F32)).astype(BF16)
    return hi, lo


def _rmsnorm(x, g):
    return x * lax.rsqrt(jnp.mean(x * x, axis=-1, keepdims=True) + EPS) * g


def _params(*sem):
    return pltpu.CompilerParams(dimension_semantics=sem, vmem_limit_bytes=VMEM_LIMIT)


MAIN_SIZES = (GLA_HEADS * GLA_DK, GLA_HEADS * GLA_DK, GLA_HEADS * GLA_DV, GLA_HEADS * GLA_DV,
              DSA_HEADS * DSA_HEAD_DIM, DSA_KV_HEADS * DSA_HEAD_DIM, DSA_KV_HEADS * DSA_HEAD_DIM)
HP_WIDTH = 512
PACK_WIDTHS = (3 * IDX_DIM, DSA_KV_HEADS * DSA_HEAD_DIM, 2 * DSA_KV_HEADS * DSA_HEAD_DIM)


def _pack_keys(ik, k, v):
    hi, lo = _split_bf16(ik)
    ikcat = jnp.concatenate([hi, hi, lo], axis=1)
    vb = v.astype(BF16)
    ones = jnp.ones((v.shape[0], DSA_HEAD_DIM), BF16)
    parts = []
    for g in range(DSA_KV_HEADS):
        parts += [vb[:, g * DSA_HEAD_DIM:(g + 1) * DSA_HEAD_DIM], ones]
    return ikcat, k.astype(BF16), jnp.concatenate(parts, axis=1)


def _proj_kernel(x_ref, g_ref, wm_ref, whi_ref, wlo_ref,
                 gq_ref, gk_ref, gv_ref, gr_ref, aq_ref, ak_ref, av_ref, gate_ref, iq_ref, ik_ref, misc_ref,
                 ikcat_ref, kb_ref, vaug_ref):
    xn = _rmsnorm(x_ref[...], g_ref[...])
    xh, xl = _split_bf16(xn)
    outs = (gq_ref, gk_ref, gv_ref, gr_ref, aq_ref, ak_ref, av_ref, gate_ref)
    off = 0
    vals = []
    for o_ref in outs:
        n = o_ref.shape[-1]
        vals.append(_dot(xh, wm_ref[:, off:off + n]))
        o_ref[...] = vals[-1].astype(o_ref.dtype)
        off += n
    ak, av = vals[5], vals[6]
    whi = whi_ref[...]
    hp = _dot(xh, whi) + _dot(xl, whi) + _dot(xh, wlo_ref[...])
    iq_ref[...] = hp[:, 0:256]
    ik_ref[...] = hp[:, 256:320]
    misc_ref[...] = hp[:, 384:512]
    ikcat_ref[...], kb_ref[...], vaug_ref[...] = _pack_keys(hp[:, 256:320], ak, av)


def _proj_call(x, g, wm, whi, wlo, tm):
    t, d = x.shape
    gate_w = wm.shape[1] - sum(MAIN_SIZES)
    widths = MAIN_SIZES + (gate_w,)
    dtypes = (BF16, BF16, BF16, BF16, BF16, F32, F32, BF16)
    out_shape = [jax.ShapeDtypeStruct((t, n), dt) for n, dt in zip(widths, dtypes)]
    out_shape += [jax.ShapeDtypeStruct((t, n), F32) for n in (256, 64, 128)]
    out_shape += [jax.ShapeDtypeStruct((t, n), BF16) for n in PACK_WIDTHS]
    row = lambda n: pl.BlockSpec((tm, n), lambda i: (i, 0))
    full = lambda a: pl.BlockSpec(a.shape, lambda i: (0, 0))
    return pl.pallas_call(
        _proj_kernel,
        grid=(t // tm,),
        in_specs=[row(d), full(g), full(wm), full(whi), full(wlo)],
        out_specs=[row(s.shape[1]) for s in out_shape],
        out_shape=out_shape,
        compiler_params=_params("parallel"),
        name="proj",
    )(x, g, wm, whi, wlo)


def _pack_kernel(ik_ref, k_ref, v_ref, ikcat_ref, kb_ref, vaug_ref):
    ikcat_ref[...], kb_ref[...], vaug_ref[...] = _pack_keys(ik_ref[...], k_ref[...], v_ref[...])


def _pack_call(ik, k, v, tm):
    t = ik.shape[0]
    row = lambda n: pl.BlockSpec((tm, n), lambda i: (i, 0))
    return pl.pallas_call(
        _pack_kernel,
        grid=(t // tm,),
        in_specs=[row(ik.shape[1]), row(k.shape[1]), row(v.shape[1])],
        out_specs=[row(n) for n in PACK_WIDTHS],
        out_shape=[jax.ShapeDtypeStruct((t, n), BF16) for n in PACK_WIDTHS],
        compiler_params=_params("parallel"),
        name="pack",
    )(ik, k, v)


def _gla_kernel(gq_ref, gk_ref, gv_ref, gr_ref, misc_ref, wup_ref, bup_ref, hn_ref, s0_ref,
                o_ref, sfin_ref, st_ref, qd_ref, ki_ref, kr_ref, dl_ref, *, nchunks):
    j = pl.program_id(1)

    @pl.when(j == 0)
    def _():
        for h in range(GLA_HEADS):
            st_ref[h] = s0_ref[h].T

    row = lax.broadcasted_iota(I32, (CHUNK, CHUNK), 0)
    col = lax.broadcasted_iota(I32, (CHUNK, CHUNK), 1)
    causal = row >= col
    tri = causal.astype(BF16)

    m_hi, m_lo = _split_bf16(misc_ref[...])
    z = _dot(jnp.concatenate([m_hi, m_lo, m_hi], axis=1), wup_ref[...]) + bup_ref[...]
    log_a = (jnp.minimum(z, 0.0) - jnp.log(1.0 + jnp.exp(-jnp.abs(z)))) * (1.0 / GLA_GATE_TAU)
    la_hi, la_lo = _split_bf16(log_a)
    tri2 = jnp.concatenate([tri, tri], axis=1)
    for c in range(nchunks):
        rows = slice(c * CHUNK, (c + 1) * CHUNK)
        b = _dot(tri2, jnp.concatenate([la_hi[rows, :], la_lo[rows, :]], axis=0))
        b_last = b[CHUNK - 1:CHUNK, :]
        q = gq_ref[rows, :].astype(F32)
        k = gk_ref[rows, :].astype(F32)
        qd_ref[rows, :] = (q * (GLA_DK ** -0.5) * jnp.exp(b)).astype(BF16)
        ki_ref[rows, :] = (k * jnp.exp(-b)).astype(BF16)
        kr_ref[rows, :] = (k * jnp.exp(b_last - b)).astype(BF16)
        dl_ref[c] = jnp.broadcast_to(jnp.exp(b_last), dl_ref.shape[1:])

    def chunk(c, carry):
        rows = pl.ds(pl.multiple_of(c * CHUNK, CHUNK), CHUNK)
        q_dec, k_inv, k_rem = qd_ref[rows, :], ki_ref[rows, :], kr_ref[rows, :]
        d_last = dl_ref[c][0:1, :]
        v = gv_ref[rows, :]
        gr = gr_ref[rows, :].astype(F32)
        hn = hn_ref[...]
        for h in range(GLA_HEADS):
            ks = slice(h * GLA_DK, (h + 1) * GLA_DK)
            vs = slice(h * GLA_DV, (h + 1) * GLA_DV)
            qh, vh = q_dec[:, ks], v[:, vs]
            scores = jnp.where(causal, _dot_nt(qh, k_inv[:, ks]), 0.0)
            st = st_ref[h]
            o = _dot(scores.astype(BF16), vh) + _dot_nt(qh, st.astype(BF16))
            ut = lax.dot_general(vh, k_rem[:, ks], TN_DIMS, preferred_element_type=F32)
            st_ref[h] = st * d_last[:, ks] + ut
            y = _rmsnorm(o, hn)
            g = gr[:, vs]
            o_ref[rows, vs] = (y * (g * jax.nn.sigmoid(g))).astype(o_ref.dtype)
        return carry

    lax.fori_loop(0, nchunks, chunk, 0, unroll=min(nchunks, GLA_UNROLL))

    @pl.when(j == pl.num_programs(1) - 1)
    def _():
        for h in range(GLA_HEADS):
            sfin_ref[h] = st_ref[h].T


def _gla_call(gq, gk, gv, gr, misc, wup, bup, hn, s0, tb):
    bsz, seq, _ = gq.shape
    blk = lambda n: pl.BlockSpec((None, tb, n), lambda b, j: (b, j, 0))
    full2 = lambda a: pl.BlockSpec(a.shape, lambda b, j: (0, 0))
    st_spec = pl.BlockSpec((None, GLA_HEADS, GLA_DK, GLA_DV), lambda b, j: (b, 0, 0, 0))
    return pl.pallas_call(
        functools.partial(_gla_kernel, nchunks=tb // CHUNK),
        grid=(bsz, seq // tb),
        in_specs=[blk(gq.shape[2]), blk(gk.shape[2]), blk(gv.shape[2]), blk(gr.shape[2]), blk(misc.shape[2]),
                  full2(wup), full2(bup), full2(hn), st_spec],
        out_specs=[blk(gv.shape[2]), st_spec],
        out_shape=[jax.ShapeDtypeStruct((bsz, seq, gv.shape[2]), BF16),
                   jax.ShapeDtypeStruct(s0.shape, F32)],
        scratch_shapes=[pltpu.VMEM((GLA_HEADS, GLA_DV, GLA_DK), F32),
                        pltpu.VMEM((tb, gq.shape[2]), BF16), pltpu.VMEM((tb, gq.shape[2]), BF16),
                        pltpu.VMEM((tb, gq.shape[2]), BF16),
                        pltpu.VMEM((tb // CHUNK, SUBLANES, gq.shape[2]), F32)],
        compiler_params=_params("parallel", "arbitrary"),
        name="gla",
    )(gq, gk, gv, gr, misc, wup, bup, hn, s0)


def _dsa_kernel(iq_ref, misc_ref, aq_ref, ikcat_ref, kb_ref, vaug_ref, o_ref,
                keys_ref, hi_ref, lo_ref, bias_ref, s_ref, mx_ref, acc_ref,
                *, tq, tqp, q_start, n_keys, topk, single_block):
    qi = 0 if single_block else pl.program_id(1)
    kc = KEY_CHUNK
    hd = DSA_HEAD_DIM
    hpg = DSA_HEADS // DSA_KV_HEADS

    first_q = q_start + qi * tq
    last_q = first_q + tq - 1
    span = KEY_UNROLL * kc
    lowest = min if single_block else jnp.minimum
    nt = (lowest(n_keys, ((last_q >> 6) + 1) * CHUNK) + span - 1) // span
    nfull = lowest(n_keys, ((first_q >> 6) + 1) * CHUNK) // span

    def loop(lo, hi, trip, carry):
        if not single_block:
            return lax.fori_loop(lo, hi, trip, carry)
        for t in range(lo, hi):
            carry = trip(t, carry)
        return carry

    def trips(body, masked):
        def trip(t, carry):
            for u in range(KEY_UNROLL):
                carry = body(pl.multiple_of((t * KEY_UNROLL + u) * kc, kc), masked, carry)
            return carry
        return trip

    def for_chunks(body, init):
        return loop(nfull, nt, trips(body, True), loop(0, nfull, trips(body, False), init))

    def for_all_chunks(body, init):
        return loop(0, nt, trips(lambda r0, _, carry: body(r0, carry), None), init)

    def pad_rows(x):
        if tqp == tq:
            return x
        return jnp.concatenate([x, jnp.zeros((tqp - tq, x.shape[1]), x.dtype)], axis=0)

    iq_hi, iq_lo = _split_bf16(pad_rows(iq_ref[...]))
    iq_cat = []
    for h in range(IDX_HEADS):
        s = slice(h * IDX_DIM, (h + 1) * IDX_DIM)
        iq_cat.append(jnp.concatenate([iq_hi[:, s], iq_lo[:, s], iq_hi[:, s]], axis=1))
    wts = pad_rows(misc_ref[...]).T[MISC_IW:MISC_IW + IDX_HEADS, :] * IDX_SCALE
    q_chunk = (first_q + lax.broadcasted_iota(I32, (1, tqp), 1)) >> 6

    def admissible(r0):
        k_pos = r0 + lax.broadcasted_iota(I32, (kc, tqp), 0)
        return ((k_pos >> 6) <= q_chunk) & (k_pos < n_keys)

    def score_body(r0, masked, carry):
        rows = pl.ds(r0, kc)
        ik = ikcat_ref[rows, :]
        acc = jnp.zeros((kc, tqp), F32)
        for h in range(IDX_HEADS):
            acc = acc + wts[h:h + 1, :] * jnp.maximum(_dot_nt(ik, iq_cat[h]), 0.0)
        if masked:
            acc = jnp.where(admissible(r0), acc, -jnp.inf)
        bits = lax.bitcast_convert_type(acc, I32)
        key = jnp.where(bits < 0, bits ^ jnp.int32(0x7FFFFFFF), bits)
        keys_ref[rows, :] = key
        hi_ref[rows, :] = (key >> 16).astype(I16)
        lo_ref[rows, :] = ((key & 0xFFFF) + INT16_MIN).astype(I16)
        return carry

    for_chunks(score_body, 0)

    grp = 32
    one, zero = jnp.asarray(1, BF16), jnp.asarray(0, BF16)

    def count16(ref, pred):
        def body(r0, acc):
            m = jnp.where(pred(ref[pl.ds(r0, kc), :]), one, zero)
            for i in range(kc // grp):
                acc = acc + m[i * grp:(i + 1) * grp, :]
            return acc
        acc = for_all_chunks(body, jnp.zeros((grp, tqp), BF16))
        return jnp.sum(acc.astype(F32), axis=0, keepdims=True)

    def search16(ref, kth):
        def bit_body(i, prefix):
            cand = prefix | (jnp.int32(1) << (15 - i))
            cand16 = (cand + INT16_MIN).astype(I16)
            tot = count16(ref, lambda ch: ch >= cand16)
            return jnp.where(tot >= kth, cand, prefix)
        return lax.fori_loop(0, 16, bit_body, jnp.zeros((1, tqp), I32))

    kth = jnp.full((1, tqp), topk, F32)
    a_u = search16(hi_ref, kth)
    a16 = (a_u + INT16_MIN).astype(I16)
    kth_lo = kth - count16(hi_ref, lambda ch: ch > a16)

    def bucket_body(r0, carry):
        rows = pl.ds(r0, kc)
        lo_ref[rows, :] = jnp.where(hi_ref[rows, :] == a16, lo_ref[rows, :], jnp.asarray(INT16_MIN, I16))
        return carry

    for_all_chunks(bucket_body, 0)
    b_u = search16(lo_ref, kth_lo)
    b16 = (b_u + INT16_MIN).astype(I16)
    thr = ((a_u + INT16_MIN) << 16) | b_u
    need = kth_lo - count16(lo_ref, lambda ch: ch > b16)

    tri = (lax.broadcasted_iota(I32, (kc, kc), 0) >= lax.broadcasted_iota(I32, (kc, kc), 1)).astype(BF16)

    def sel_body(r0, masked, running):
        ch = keys_ref[pl.ds(r0, kc), :]
        eq = ch == thr
        rank = _dot(tri, jnp.where(eq, 1.0, 0.0).astype(BF16)) + running
        sel = (ch > thr) | (eq & (rank <= need))
        if masked:
            sel = sel & admissible(r0)
        bias_ref[:, pl.ds(r0, kc)] = jnp.where(sel, 0.0, -jnp.inf).T[:tq, :]
        return rank[kc - 1:kc, :]

    for_chunks(sel_body, jnp.zeros((1, tqp), F32))

    mx_ref[...] = jnp.full(mx_ref.shape, -jnp.inf, F32)
    acc_ref[...] = jnp.zeros(acc_ref.shape, F32)

    def pass1(g):
        qg = jnp.concatenate([aq_ref[:, h * hd:(h + 1) * hd] for h in range(g * hpg, (g + 1) * hpg)], axis=0)
        qg = qg * jnp.asarray(hd ** -0.5, BF16)

        def body(r0, carry):
            cols = pl.ds(r0, kc)
            s = _dot_nt(qg, kb_ref[cols, g * hd:(g + 1) * hd])
            bias = bias_ref[:, cols]
            for r in range(hpg):
                rows = slice(r * tq, (r + 1) * tq)
                sr = s[rows, :] + bias
                s_ref[rows, cols] = sr
                mx = mx_ref[g, rows, :]
                for i in range(kc // LANES):
                    mx = jnp.maximum(mx, sr[:, i * LANES:(i + 1) * LANES])
                mx_ref[g, rows, :] = mx
            return carry
        return body

    def pass2(g):
        m = jnp.max(mx_ref[g], axis=1, keepdims=True)

        def body(r0, carry):
            cols = pl.ds(r0, kc)
            p = jnp.exp(s_ref[:, cols] - m)
            acc_ref[g] += _dot(p.astype(BF16), vaug_ref[cols, 2 * g * hd:2 * (g + 1) * hd])
            return carry
        return body

    for_all_chunks(pass1(0), 0)
    for g in range(1, DSA_KV_HEADS):
        exp_prev, logits_next = pass2(g - 1), pass1(g)
        for_all_chunks(lambda r0, carry: logits_next(r0, exp_prev(r0, carry)), 0)
    for_all_chunks(pass2(DSA_KV_HEADS - 1), 0)

    for g in range(DSA_KV_HEADS):
        acc = acc_ref[g]
        out = acc[:, :hd] / acc[:, hd:hd + 1]
        for r in range(hpg):
            h = g * hpg + r
            o_ref[:, h * hd:(h + 1) * hd] = out[r * tq:(r + 1) * tq, :].astype(o_ref.dtype)


def _dsa_call(iq, misc, aq, ikcat, kb, vaug, *, q_start, n_keys, tq):
    bsz, lq, _ = iq.shape
    lk = ikcat.shape[1]
    assert lk % (KEY_UNROLL * KEY_CHUNK) == 0 and lq % tq == 0 and lk // 32 <= 256
    tqp = max(tq, LANES)
    topk = min(TOPK_MAX, n_keys // 4)
    rows_g = (DSA_HEADS // DSA_KV_HEADS) * tq
    qblk = lambda n: pl.BlockSpec((None, tq, n), lambda b, q: (b, q, 0))
    kblk = lambda n: pl.BlockSpec((None, lk, n), lambda b, q: (b, 0, 0))
    kern = functools.partial(_dsa_kernel, tq=tq, tqp=tqp, q_start=q_start, n_keys=n_keys, topk=topk,
                             single_block=(lq == tq))
    return pl.pallas_call(
        kern,
        grid=(bsz, lq // tq),
        in_specs=[qblk(iq.shape[2]), qblk(misc.shape[2]), qblk(aq.shape[2]),
                  kblk(ikcat.shape[2]), kblk(kb.shape[2]), kblk(vaug.shape[2])],
        out_specs=qblk(aq.shape[2]),
        out_shape=jax.ShapeDtypeStruct(aq.shape, BF16),
        scratch_shapes=[pltpu.VMEM((lk, tqp), I32),
                        pltpu.VMEM((lk, tqp), I16),
                        pltpu.VMEM((lk, tqp), I16),
                        pltpu.VMEM((tq, lk), F32),
                        pltpu.VMEM((rows_g, lk), F32),
                        pltpu.VMEM((DSA_KV_HEADS, rows_g, LANES), F32),
                        pltpu.VMEM((DSA_KV_HEADS, rows_g, 2 * DSA_HEAD_DIM), F32)],
        compiler_params=_params("parallel", "arbitrary"),
        name="dsa",
    )(iq, misc, aq, ikcat, kb, vaug)


def _merge_kernel(x_ref, go_ref, do_ref, gate_ref, wa_ref, wb_ref, wo_ref, g_ref, y_ref):
    d = x_ref.shape[-1]
    a = _dot(go_ref[...], wa_ref[...])
    b = _dot(do_ref[...], wb_ref[...])
    gate = gate_ref[...].astype(F32)
    mixed = jax.nn.sigmoid(gate[:, :d]) * a + jax.nn.sigmoid(gate[:, d:]) * b
    y_ref[...] = x_ref[...] + _rmsnorm(_dot(mixed.astype(BF16), wo_ref[...]), g_ref[...])


def _merge_call(x, go, do, gate, wa, wb, wo, g, tm):
    t, d = x.shape
    row = lambda n: pl.BlockSpec((tm, n), lambda i: (i, 0))
    full = lambda a: pl.BlockSpec(a.shape, lambda i: (0, 0))
    return pl.pallas_call(
        _merge_kernel,
        grid=(t // tm,),
        in_specs=[row(d), row(go.shape[1]), row(do.shape[1]), row(gate.shape[1]),
                  full(wa), full(wb), full(wo), full(g)],
        out_specs=row(d),
        out_shape=jax.ShapeDtypeStruct((t, d), F32),
        compiler_params=_params("parallel"),
        name="merge",
    )(x, go, do, gate, wa, wb, wo, g)


FFN_ROW_CHUNK = 256
FFN_SLAB_ROWS = 64
GELU_C0 = float(np.sqrt(2.0 / np.pi))
GELU_C1 = 0.044715 * GELU_C0


def _ffn_kernel(x_ref, gpre_ref, wu_ref, cw_ref, cb_ref, pv_ref, wdh_ref, gpost_ref,
                y_ref, cn_ref, xn_ref, acc_ref, car_ref, act_ref, u_ref, *, bps, tn):
    i = pl.program_id(0)
    tm = x_ref.shape[0]
    f = wdh_ref.shape[0]
    xn_ref[...] = _rmsnorm(x_ref[...], gpre_ref[...]).astype(BF16)
    acc_ref[...] = jnp.zeros_like(acc_ref)
    first = (i % bps) == 0
    top = lax.broadcasted_iota(I32, (SUBLANES, tn), 0)
    rc = min(tm, FFN_ROW_CHUNK)

    def up(cols, r):
        return _dot(xn_ref[r * rc:(r + 1) * rc, :], wu_ref[:, cols])

    def conv_slab(cols, u, halo):
        r1, r2 = pltpu.roll(u, 1, 0), pltpu.roll(u, 2, 0)
        top1 = jnp.where(top == 0, halo[1:2, :], r1[:SUBLANES, :])
        top2 = jnp.where(top == 0, halo[0:1, :], jnp.where(top == 1, halo[1:2, :], r2[:SUBLANES, :]))
        u1 = jnp.concatenate([top1, r1[SUBLANES:, :]], axis=0)
        u2 = jnp.concatenate([top2, r2[SUBLANES:, :]], axis=0)
        cw = cw_ref[:, cols]
        return cb_ref[:, cols] + (cw[0:1, :] * u2 + cw[1:2, :] * u1 + cw[2:3, :] * u), u[u.shape[0] - 2:, :]

    def keep_tail(cols, slot, half):
        car_ref[:, cols] = u_ref[slot, half, rc - SUBLANES:rc, :]
        cn_ref[:, cols] = u_ref[slot, half, rc - (CONV_W - 1):rc, :]

    def block_halo(cols):
        return jnp.where(first, pv_ref[:, cols], car_ref[SUBLANES - 2:SUBLANES, cols])

    nft = f // tn
    nrc = tm // rc

    def tile(j, aslot, with_down):
        cols_a = pl.ds(pl.multiple_of(j * tn, tn), tn)
        cols_b = pl.ds(pl.multiple_of(f + j * tn, tn), tn)
        halo_a, halo_b = block_halo(cols_a), block_halo(cols_b)

        def up_to(slot, r):
            u_ref[slot, 0] = up(cols_a, r)
            u_ref[slot, 1] = up(cols_b, r)

        up_to(0, 0)
        for r in range(nrc):
            rows = slice(r * rc, (r + 1) * rc)
            slot = r % 2
            if r + 1 < nrc:
                up_to(1 - slot, r + 1)
            if with_down:
                project_down(j - 1, 1 - aslot, rows)
            if r + 1 == nrc:
                keep_tail(cols_a, slot, 0)
                keep_tail(cols_b, slot, 1)
            sl = min(rc, FFN_SLAB_ROWS)
            for s in range(0, rc, sl):
                a, halo_a = conv_slab(cols_a, u_ref[slot, 0, s:s + sl, :], halo_a)
                b, halo_b = conv_slab(cols_b, u_ref[slot, 1, s:s + sl, :], halo_b)
                act = (a * (1.0 + jnp.tanh(a * (GELU_C0 + GELU_C1 * (a * a))))) * b
                act_ref[aslot, r * rc + s:r * rc + s + sl, :] = act.astype(BF16)

    def project_down(j, aslot, rows):
        acc_ref[rows, :] += _dot(act_ref[aslot, rows, :], wdh_ref[pl.ds(pl.multiple_of(j * tn, tn), tn), :])

    assert nft % 2 == 1
    tile(0, 0, False)

    def trip(p, carry):
        tile(2 * p + 1, 1, True)
        tile(2 * p + 2, 0, True)
        return carry

    lax.fori_loop(0, nft // 2, trip, 0)
    for r in range(nrc):
        project_down(nft - 1, 0, slice(r * rc, (r + 1) * rc))
    y_ref[...] = x_ref[...] + _rmsnorm(acc_ref[...], gpost_ref[...])


def _ffn_call(x, gpre, w_up, conv_w, conv_b, conv_prev, w_down_half, gpost, *, tm, seq, tn):
    t, d = x.shape
    f = w_down_half.shape[0]
    nseq = t // seq
    bps = seq // tm
    assert f % tn == 0 and seq % tm == 0
    xrow = pl.BlockSpec((tm, d), lambda i: (i, 0))
    const = lambda a: pl.BlockSpec(a.shape, lambda i: (0, 0), pipeline_mode=pl.Buffered(1))
    y, cn = pl.pallas_call(
        functools.partial(_ffn_kernel, bps=bps, tn=tn),
        grid=(t // tm,),
        in_specs=[xrow, const(gpre), const(w_up), const(conv_w), const(conv_b),
                  pl.BlockSpec((None, CONV_W - 1, 2 * f), lambda i: (i // bps, 0, 0)),
                  const(w_down_half), const(gpost)],
        out_specs=[xrow, pl.BlockSpec((None, CONV_W - 1, 2 * f), lambda i: (i, 0, 0))],
        out_shape=[jax.ShapeDtypeStruct((t, d), F32),
                   jax.ShapeDtypeStruct((t // tm, CONV_W - 1, 2 * f), F32)],
        scratch_shapes=[pltpu.VMEM((tm, d), BF16), pltpu.VMEM((tm, d), F32),
                        pltpu.VMEM((SUBLANES, 2 * f), F32), pltpu.VMEM((2, tm, tn), BF16),
                        pltpu.VMEM((2, 2, min(tm, FFN_ROW_CHUNK), tn), F32)],
        compiler_params=_params("arbitrary"),
        name="ffn",
    )(x, gpre, w_up, conv_w, conv_b, conv_prev, w_down_half, gpost)
    return y, cn.reshape(nseq, bps, CONV_W - 1, 2 * f)[:, bps - 1]


def _prep_weights(w_in, w_gate_up, d_model):
    sizes = (GLA_HEADS * GLA_DK, GLA_HEADS * GLA_DK, GLA_HEADS * GLA_DV, GLA_RANK, GLA_HEADS * GLA_DV,
             DSA_HEADS * DSA_HEAD_DIM, DSA_KV_HEADS * DSA_HEAD_DIM, DSA_KV_HEADS * DSA_HEAD_DIM,
             IDX_HEADS * IDX_DIM, IDX_DIM, IDX_HEADS, 2 * d_model)
    offs = np.concatenate([[0], np.cumsum(sizes)])
    g_q, g_k, g_v, g_low, g_r, a_q, a_k, a_v, i_q, i_k, i_w, gate = (
        w_in[:, int(offs[n]):int(offs[n + 1])] for n in range(len(sizes)))
    w_main = jnp.concatenate([g_q, g_k, g_v, g_r, a_q, a_k, a_v, gate], axis=1).astype(BF16)
    zeros = lambda n: jnp.zeros((w_in.shape[0], n), w_in.dtype)
    misc = jnp.concatenate([g_low, i_w, zeros(LANES - GLA_RANK - IDX_HEADS)], axis=1)
    assert MISC_GLOW == 0 and MISC_IW == GLA_RANK
    w_hp = jnp.concatenate([i_q, i_k, zeros(LANES - IDX_DIM), misc], axis=1)
    assert w_hp.shape[1] == HP_WIDTH
    w_hp_hi, w_hp_lo = _split_bf16(w_hp)
    wup = jnp.concatenate([w_gate_up, jnp.zeros((LANES - GLA_RANK, w_gate_up.shape[1]), w_gate_up.dtype)], axis=0)
    wup_hi, wup_lo = _split_bf16(wup)
    return w_main, w_hp_hi, w_hp_lo, jnp.concatenate([wup_hi, wup_hi, wup_lo], axis=0)


def _run_layer(x, k_past, v_past, kidx_past, gla_s0, conv_prev, w, *, tm, gla_tb, dsa_tq, ffn_tm):
    bsz, seq, d = x.shape
    past = k_past.shape[1]
    t = bsz * seq
    x2 = x.reshape(t, d)
    gq, gk, gv, gr, aq, ak, av, gate, iq, ik, misc, ikcat, kb, vaug = _proj_call(
        x2, w["attn_pre_norm"], w["w_main"], w["w_hp_hi"], w["w_hp_lo"], tm)
    per_seq = lambda a: a.reshape(bsz, seq, a.shape[-1])

    gla_o, gla_s = _gla_call(per_seq(gq), per_seq(gk), per_seq(gv), per_seq(gr), per_seq(misc),
                             w["wup"], w["b_gla_gate"], w["gla_head_norm"], gla_s0, gla_tb)

    n_keys = past + seq
    key_span = KEY_UNROLL * KEY_CHUNK
    lk = -(-n_keys // key_span) * key_span
    packed = [per_seq(a) for a in (ikcat, kb, vaug)]
    if past:
        flat = lambda a: a.reshape(bsz * past, -1).astype(F32)
        packed_past = _pack_call(flat(kidx_past), flat(k_past), flat(v_past), tm)
        packed = [jnp.concatenate([p.reshape(bsz, past, -1), n], axis=1) for p, n in zip(packed_past, packed)]
    if lk > n_keys:
        packed = [jnp.pad(a, ((0, 0), (0, lk - n_keys), (0, 0))) for a in packed]

    dsa_o = _dsa_call(per_seq(iq), per_seq(misc), per_seq(aq), *packed,
                      q_start=past, n_keys=n_keys, tq=dsa_tq)

    x1 = _merge_call(x2, gla_o.reshape(t, -1), dsa_o.reshape(t, -1), gate,
                     w["w_branch_gla"], w["w_branch_dsa"], w["w_out"], w["attn_post_norm"], tm)
    y, conv_new = _ffn_call(x1, w["ffn_pre_norm"], w["w_up"], w["conv_w"], w["conv_b"], conv_prev,
                            w["w_down_half"], w["ffn_post_norm"], tm=ffn_tm, seq=seq, tn=256)
    k_new = ak.reshape(bsz, seq, DSA_KV_HEADS, DSA_HEAD_DIM)
    v_new = av.reshape(bsz, seq, DSA_KV_HEADS, DSA_HEAD_DIM)
    return y.reshape(bsz, seq, d), k_new, v_new, per_seq(ik), gla_s, conv_new


def kernel(x_prompt, x_sample, cache_k, cache_v, cache_k_idx, state_gla, state_ffn_conv, attn_pre_norm, w_in, w_gla_gate_up, b_gla_gate, gla_head_norm, w_branch_gla, w_branch_dsa, w_out, attn_post_norm, ffn_pre_norm, w_up, conv_w, conv_b, w_down, ffn_post_norm):
    depth = w_in.shape[0]
    bsz, _, d = x_prompt.shape
    dt = x_prompt.dtype
    y_prompt, y_sample = x_prompt, x_sample
    prompt_new, sample_new = [], []
    row = lambda a: a.reshape(1, -1)
    for l in range(depth):
        w_main, w_hp_hi, w_hp_lo, wup = _prep_weights(w_in[l], w_gla_gate_up[l], d)
        w = dict(attn_pre_norm=row(attn_pre_norm[l]), w_main=w_main, w_hp_hi=w_hp_hi, w_hp_lo=w_hp_lo, wup=wup,
                 b_gla_gate=row(b_gla_gate[l]), gla_head_norm=row(gla_head_norm[l]),
                 w_branch_gla=w_branch_gla[l].astype(BF16), w_branch_dsa=w_branch_dsa[l].astype(BF16),
                 w_out=w_out[l].astype(BF16), attn_post_norm=row(attn_post_norm[l]),
                 ffn_pre_norm=row(ffn_pre_norm[l]), w_up=w_up[l].astype(BF16), conv_w=conv_w[l],
                 conv_b=row(conv_b[l]), w_down_half=(0.5 * w_down[l]).astype(BF16),
                 ffn_post_norm=row(ffn_post_norm[l]))
        ffn2 = conv_w.shape[-1]
        y_prompt, *new = _run_layer(
            y_prompt,
            jnp.zeros((bsz, 0, DSA_KV_HEADS * DSA_HEAD_DIM), dt),
            jnp.zeros((bsz, 0, DSA_KV_HEADS * DSA_HEAD_DIM), dt),
            jnp.zeros((bsz, 0, IDX_DIM), dt),
            jnp.zeros((bsz, GLA_HEADS, GLA_DK, GLA_DV), F32),
            jnp.zeros((bsz, CONV_W - 1, ffn2), dt),
            w, tm=256, gla_tb=512, dsa_tq=256, ffn_tm=1024)
        prompt_new.append(new)
        y_sample, *new = _run_layer(
            y_sample, cache_k[l], cache_v[l], cache_k_idx[l], state_gla[l], state_ffn_conv[l],
            w, tm=256, gla_tb=x_sample.shape[1], dsa_tq=x_sample.shape[1], ffn_tm=x_sample.shape[1])
        sample_new.append(new)
    k_p, v_p, kidx_p, gla_p, conv_p = (jnp.stack(t, axis=0) for t in zip(*prompt_new))
    k_s, v_s, kidx_s, gla_s, conv_s = (jnp.stack(t, axis=0) for t in zip(*sample_new))
    return (y_prompt, y_sample, k_p, v_p, kidx_p, gla_p, conv_p, k_s, v_s, kidx_s, gla_s, conv_s)
```

```python
import functools

import numpy as np
import jax
import jax.numpy as jnp
from jax import lax
from jax.experimental import pallas as pl
from jax.experimental.pallas import tpu as pltpu

F32 = jnp.float32
BF16 = jnp.bfloat16
I32 = jnp.int32
I16 = jnp.int16
HIGHEST = lax.Precision.HIGHEST

CHUNK = 64
GLA_HEADS = 4
GLA_DK = 128
GLA_DV = 256
GLA_RANK = 16
GLA_GATE_TAU = 16.0
DSA_HEADS = 8
DSA_KV_HEADS = 2
DSA_HEAD_DIM = 64
IDX_HEADS = 4
IDX_DIM = 64
IDX_SCALE = (IDX_HEADS * IDX_DIM) ** -0.5
TOPK_MAX = 256
CONV_W = 3
EPS = 1e-6

LANES = 128
SUBLANES = 8
VMEM_LIMIT = 56 * 1024 * 1024

MISC_GLOW = 0
MISC_IW = 16

KEY_CHUNK = 256
KEY_UNROLL = 2
GLA_UNROLL = 8
INT16_MIN = -(2 ** 15)
MASKED = -1e30
NT_DIMS = (((1,), (1,)), ((), ()))
TN_DIMS = (((0,), (0,)), ((), ()))


def _dot(a, b, **kw):
    return jnp.dot(a, b, preferred_element_type=F32, **kw)


def _dot_nt(a, b):
    return lax.dot_general(a, b, NT_DIMS, preferred_element_type=F32)


def _split_bf16(x):
    hi = x.astype(BF16)
    lo = (x - hi.astype(F32)).astype(BF16)
    return hi, lo


def _rmsnorm(x, g):
    return x * lax.rsqrt(jnp.mean(x * x, axis=-1, keepdims=True) + EPS) * g


def _params(*sem):
    return pltpu.CompilerParams(dimension_semantics=sem, vmem_limit_bytes=VMEM_LIMIT)


MAIN_SIZES = (GLA_HEADS * GLA_DK, GLA_HEADS * GLA_DK, GLA_HEADS * GLA_DV, GLA_HEADS * GLA_DV,
              DSA_HEADS * DSA_HEAD_DIM, DSA_KV_HEADS * DSA_HEAD_DIM, DSA_KV_HEADS * DSA_HEAD_DIM)
HP_WIDTH = 512
PACK_WIDTHS = (3 * IDX_DIM, DSA_KV_HEADS * DSA_HEAD_DIM, 2 * DSA_KV_HEADS * DSA_HEAD_DIM)


def _pack_keys(ik, k, v):
    hi, lo = _split_bf16(ik)
    ikcat = jnp.concatenate([hi, hi, lo], axis=1)
    vb = v.astype(BF16)
    ones = jnp.ones((v.shape[0], DSA_HEAD_DIM), BF16)
    parts = []
    for g in range(DSA_KV_HEADS):
        parts += [vb[:, g * DSA_HEAD_DIM:(g + 1) * DSA_HEAD_DIM], ones]
    return ikcat, k.astype(BF16), jnp.concatenate(parts, axis=1)


def _proj_kernel(x_ref, g_ref, wm_ref, whi_ref, wlo_ref,
                 gq_ref, gk_ref, gv_ref, gr_ref, aq_ref, ak_ref, av_ref, gate_ref, iq_ref, ik_ref, misc_ref,
                 ikcat_ref, kb_ref, vaug_ref):
    xn = _rmsnorm(x_ref[...], g_ref[...])
    xh, xl = _split_bf16(xn)
    outs = (gq_ref, gk_ref, gv_ref, gr_ref, aq_ref, ak_ref, av_ref, gate_ref)
    off = 0
    vals = []
    for o_ref in outs:
        n = o_ref.shape[-1]
        vals.append(_dot(xh, wm_ref[:, off:off + n]))
        o_ref[...] = vals[-1].astype(o_ref.dtype)
        off += n
    ak, av = vals[5], vals[6]
    whi = whi_ref[...]
    hp = _dot(xh, whi) + _dot(xl, whi) + _dot(xh, wlo_ref[...])
    iq_ref[...] = hp[:, 0:256]
    ik_ref[...] = hp[:, 256:320]
    misc_ref[...] = hp[:, 384:512]
    ikcat_ref[...], kb_ref[...], vaug_ref[...] = _pack_keys(hp[:, 256:320], ak, av)


def _proj_call(x, g, wm, whi, wlo, tm):
    t, d = x.shape
    gate_w = wm.shape[1] - sum(MAIN_SIZES)
    widths = MAIN_SIZES + (gate_w,)
    dtypes = (BF16, BF16, BF16, BF16, BF16, F32, F32, BF16)
    out_shape = [jax.ShapeDtypeStruct((t, n), dt) for n, dt in zip(widths, dtypes)]
    out_shape += [jax.ShapeDtypeStruct((t, n), F32) for n in (256, 64, 128)]
    out_shape += [jax.ShapeDtypeStruct((t, n), BF16) for n in PACK_WIDTHS]
    row = lambda n: pl.BlockSpec((tm, n), lambda i: (i, 0))
    full = lambda a: pl.BlockSpec(a.shape, lambda i: (0, 0))
    return pl.pallas_call(
        _proj_kernel,
        grid=(t // tm,),
        in_specs=[row(d), full(g), full(wm), full(whi), full(wlo)],
        out_specs=[row(s.shape[1]) for s in out_shape],
        out_shape=out_shape,
        compiler_params=_params("parallel"),
        name="proj",
    )(x, g, wm, whi, wlo)


def _pack_kernel(ik_ref, k_ref, v_ref, ikcat_ref, kb_ref, vaug_ref):
    ikcat_ref[...], kb_ref[...], vaug_ref[...] = _pack_keys(ik_ref[...], k_ref[...], v_ref[...])


def _pack_call(ik, k, v, tm):
    t = ik.shape[0]
    row = lambda n: pl.BlockSpec((tm, n), lambda i: (i, 0))
    return pl.pallas_call(
        _pack_kernel,
        grid=(t // tm,),
        in_specs=[row(ik.shape[1]), row(k.shape[1]), row(v.shape[1])],
        out_specs=[row(n) for n in PACK_WIDTHS],
        out_shape=[jax.ShapeDtypeStruct((t, n), BF16) for n in PACK_WIDTHS],
        compiler_params=_params("parallel"),
        name="pack",
    )(ik, k, v)


def _gla_kernel(gq_ref, gk_ref, gv_ref, gr_ref, misc_ref, wup_ref, bup_ref, hn_ref, s0_ref,
                o_ref, sfin_ref, st_ref, qd_ref, ki_ref, kr_ref, dl_ref, *, nchunks):
    j = pl.program_id(1)

    @pl.when(j == 0)
    def _():
        for h in range(GLA_HEADS):
            st_ref[h] = s0_ref[h].T

    row = lax.broadcasted_iota(I32, (CHUNK, CHUNK), 0)
    col = lax.broadcasted_iota(I32, (CHUNK, CHUNK), 1)
    causal = row >= col
    tri = causal.astype(BF16)

    m_hi, m_lo = _split_bf16(misc_ref[...])
    z = _dot(jnp.concatenate([m_hi, m_lo, m_hi], axis=1), wup_ref[...]) + bup_ref[...]
    log_a = (jnp.minimum(z, 0.0) - jnp.log(1.0 + jnp.exp(-jnp.abs(z)))) * (1.0 / GLA_GATE_TAU)
    la_hi, la_lo = _split_bf16(log_a)
    tri2 = jnp.concatenate([tri, tri], axis=1)
    for c in range(nchunks):
        rows = slice(c * CHUNK, (c + 1) * CHUNK)
        b = _dot(tri2, jnp.concatenate([la_hi[rows, :], la_lo[rows, :]], axis=0))
        b_last = b[CHUNK - 1:CHUNK, :]
        q = gq_ref[rows, :].astype(F32)
        k = gk_ref[rows, :].astype(F32)
        qd_ref[rows, :] = (q * (GLA_DK ** -0.5) * jnp.exp(b)).astype(BF16)
        ki_ref[rows, :] = (k * jnp.exp(-b)).astype(BF16)
        kr_ref[rows, :] = (k * jnp.exp(b_last - b)).astype(BF16)
        dl_ref[c] = jnp.broadcast_to(jnp.exp(b_last), dl_ref.shape[1:])

    def chunk(c, carry):
        rows = pl.ds(pl.multiple_of(c * CHUNK, CHUNK), CHUNK)
        q_dec, k_inv, k_rem = qd_ref[rows, :], ki_ref[rows, :], kr_ref[rows, :]
        d_last = dl_ref[c][0:1, :]
        v = gv_ref[rows, :]
        gr = gr_ref[rows, :].astype(F32)
        hn = hn_ref[...]
        for h in range(GLA_HEADS):
            ks = slice(h * GLA_DK, (h + 1) * GLA_DK)
            vs = slice(h * GLA_DV, (h + 1) * GLA_DV)
            qh, vh = q_dec[:, ks], v[:, vs]
            scores = jnp.where(causal, _dot_nt(qh, k_inv[:, ks]), 0.0)
            st = st_ref[h]
            o = _dot(scores.astype(BF16), vh) + _dot_nt(qh, st.astype(BF16))
            ut = lax.dot_general(vh, k_rem[:, ks], TN_DIMS, preferred_element_type=F32)
            st_ref[h] = st * d_last[:, ks] + ut
            y = _rmsnorm(o, hn)
            g = gr[:, vs]
            o_ref[rows, vs] = (y * (g * jax.nn.sigmoid(g))).astype(o_ref.dtype)
        return carry

    lax.fori_loop(0, nchunks, chunk, 0, unroll=min(nchunks, GLA_UNROLL))

    @pl.when(j == pl.num_programs(1) - 1)
    def _():
        for h in range(GLA_HEADS):
            sfin_ref[h] = st_ref[h].T


def _gla_call(gq, gk, gv, gr, misc, wup, bup, hn, s0, tb):
    bsz, seq, _ = gq.shape
    blk = lambda n: pl.BlockSpec((None, tb, n), lambda b, j: (b, j, 0))
    full2 = lambda a: pl.BlockSpec(a.shape, lambda b, j: (0, 0))
    st_spec = pl.BlockSpec((None, GLA_HEADS, GLA_DK, GLA_DV), lambda b, j: (b, 0, 0, 0))
    return pl.pallas_call(
        functools.partial(_gla_kernel, nchunks=tb // CHUNK),
        grid=(bsz, seq // tb),
        in_specs=[blk(gq.shape[2]), blk(gk.shape[2]), blk(gv.shape[2]), blk(gr.shape[2]), blk(misc.shape[2]),
                  full2(wup), full2(bup), full2(hn), st_spec],
        out_specs=[blk(gv.shape[2]), st_spec],
        out_shape=[jax.ShapeDtypeStruct((bsz, seq, gv.shape[2]), BF16),
                   jax.ShapeDtypeStruct(s0.shape, F32)],
        scratch_shapes=[pltpu.VMEM((GLA_HEADS, GLA_DV, GLA_DK), F32),
                        pltpu.VMEM((tb, gq.shape[2]), BF16), pltpu.VMEM((tb, gq.shape[2]), BF16),
                        pltpu.VMEM((tb, gq.shape[2]), BF16),
                        pltpu.VMEM((tb // CHUNK, SUBLANES, gq.shape[2]), F32)],
        compiler_params=_params("parallel", "arbitrary"),
        name="gla",
    )(gq, gk, gv, gr, misc, wup, bup, hn, s0)


def _dsa_kernel(iq_ref, misc_ref, aq_ref, ikcat_ref, kb_ref, vaug_ref, o_ref,
                keys_ref, hi_ref, lo_ref, m_ref, acc_ref,
                *, tq, tqp, q_start, n_keys, topk, single_block):
    qi = 0 if single_block else pl.program_id(1)
    kc = KEY_CHUNK
    hd = DSA_HEAD_DIM
    hpg = DSA_HEADS // DSA_KV_HEADS

    first_q = q_start + qi * tq
    last_q = first_q + tq - 1
    span = KEY_UNROLL * kc
    lowest = min if single_block else jnp.minimum
    nt = (lowest(n_keys, ((last_q >> 6) + 1) * CHUNK) + span - 1) // span
    nfull = lowest(n_keys, ((first_q >> 6) + 1) * CHUNK) // span

    def loop(lo, hi, trip, carry):
        if not single_block:
            return lax.fori_loop(lo, hi, trip, carry)
        for t in range(lo, hi):
            carry = trip(t, carry)
        return carry

    def trips(body, masked):
        def trip(t, carry):
            for u in range(KEY_UNROLL):
                carry = body(pl.multiple_of((t * KEY_UNROLL + u) * kc, kc), masked, carry)
            return carry
        return trip

    def for_chunks(body, init):
        return loop(nfull, nt, trips(body, True), loop(0, nfull, trips(body, False), init))

    def for_all_chunks(body, init):
        return loop(0, nt, trips(lambda r0, _, carry: body(r0, carry), None), init)

    def pad_rows(x):
        if tqp == tq:
            return x
        return jnp.concatenate([x, jnp.zeros((tqp - tq, x.shape[1]), x.dtype)], axis=0)

    iq_hi, iq_lo = _split_bf16(pad_rows(iq_ref[...]))
    iq_cat = []
    for h in range(IDX_HEADS):
        s = slice(h * IDX_DIM, (h + 1) * IDX_DIM)
        iq_cat.append(jnp.concatenate([iq_hi[:, s], iq_lo[:, s], iq_hi[:, s]], axis=1))
    wts = pad_rows(misc_ref[...]).T[MISC_IW:MISC_IW + IDX_HEADS, :] * IDX_SCALE
    q_chunk = (first_q + lax.broadcasted_iota(I32, (1, tqp), 1)) >> 6

    def admissible(r0):
        k_pos = r0 + lax.broadcasted_iota(I32, (kc, tqp), 0)
        return ((k_pos >> 6) <= q_chunk) & (k_pos < n_keys)

    def score_body(r0, masked, carry):
        rows = pl.ds(r0, kc)
        ik = ikcat_ref[rows, :]
        acc = jnp.zeros((kc, tqp), F32)
        for h in range(IDX_HEADS):
            acc = acc + wts[h:h + 1, :] * jnp.maximum(_dot_nt(ik, iq_cat[h]), 0.0)
        if masked:
            acc = jnp.where(admissible(r0), acc, -jnp.inf)
        bits = lax.bitcast_convert_type(acc, I32)
        key = jnp.where(bits < 0, bits ^ jnp.int32(0x7FFFFFFF), bits)
        keys_ref[rows, :] = key
        hi_ref[rows, :] = (key >> 16).astype(I16)
        lo_ref[rows, :] = ((key & 0xFFFF) + INT16_MIN).astype(I16)
        return carry

    for_chunks(score_body, 0)

    grp = 32
    one, zero = jnp.asarray(1, BF16), jnp.asarray(0, BF16)

    def count16(ref, pred):
        def body(r0, acc):
            m = jnp.where(pred(ref[pl.ds(r0, kc), :]), one, zero)
            for i in range(kc // grp):
                acc = acc + m[i * grp:(i + 1) * grp, :]
            return acc
        acc = for_all_chunks(body, jnp.zeros((grp, tqp), BF16))
        return jnp.sum(acc.astype(F32), axis=0, keepdims=True)

    def search16(ref, kth):
        def bit_body(i, prefix):
            cand = prefix | (jnp.int32(1) << (15 - i))
            cand16 = (cand + INT16_MIN).astype(I16)
            tot = count16(ref, lambda ch: ch >= cand16)
            return jnp.where(tot >= kth, cand, prefix)
        return lax.fori_loop(0, 16, bit_body, jnp.zeros((1, tqp), I32))

    kth = jnp.full((1, tqp), topk, F32)
    a_u = search16(hi_ref, kth)
    a16 = (a_u + INT16_MIN).astype(I16)
    kth_lo = kth - count16(hi_ref, lambda ch: ch > a16)

    def bucket_body(r0, carry):
        rows = pl.ds(r0, kc)
        lo_ref[rows, :] = jnp.where(hi_ref[rows, :] == a16, lo_ref[rows, :], jnp.asarray(INT16_MIN, I16))
        return carry

    for_all_chunks(bucket_body, 0)
    b_u = search16(lo_ref, kth_lo)
    b16 = (b_u + INT16_MIN).astype(I16)
    thr = ((a_u + INT16_MIN) << 16) | b_u
    need = kth_lo - count16(lo_ref, lambda ch: ch > b16)

    tri = (lax.broadcasted_iota(I32, (kc, kc), 0) >= lax.broadcasted_iota(I32, (kc, kc), 1)).astype(BF16)

    def sel_bias(r0, masked, running):
        ch = keys_ref[pl.ds(r0, kc), :]
        eq = ch == thr
        rank = _dot(tri, jnp.where(eq, 1.0, 0.0).astype(BF16)) + running
        sel = (ch > thr) | (eq & (rank <= need))
        if masked:
            sel = sel & admissible(r0)
        return jnp.where(sel, 0.0, MASKED).T[:tq, :], rank[kc - 1:kc, :]

    m_ref[...] = jnp.full(m_ref.shape, MASKED, F32)
    acc_ref[...] = jnp.zeros(acc_ref.shape, F32)
    q_groups = []
    for g in range(DSA_KV_HEADS):
        qg = jnp.concatenate([aq_ref[:, h * hd:(h + 1) * hd] for h in range(g * hpg, (g + 1) * hpg)], axis=0)
        q_groups.append(qg * jnp.asarray(hd ** -0.5, BF16))

    def attend(r0, masked, running):
        cols = pl.ds(r0, kc)
        bias, running = sel_bias(r0, masked, running)
        for g in range(DSA_KV_HEADS):
            s = _dot_nt(q_groups[g], kb_ref[cols, g * hd:(g + 1) * hd])
            s = jnp.concatenate([s[r * tq:(r + 1) * tq, :] + bias for r in range(hpg)], axis=0)
            tiles = [s[:, i * LANES:(i + 1) * LANES] for i in range(kc // LANES)]
            m_old = m_ref[g]
            m_new = jnp.maximum(m_old, jnp.max(functools.reduce(jnp.maximum, tiles), axis=1, keepdims=True))
            p = jnp.concatenate([jnp.exp(t - m_new) for t in tiles], axis=1)
            pv = _dot(p.astype(BF16), vaug_ref[cols, 2 * g * hd:2 * (g + 1) * hd])
            acc_ref[g] = acc_ref[g] * jnp.exp(m_old - m_new) + pv
            m_ref[g] = m_new
        return running

    for_chunks(attend, jnp.zeros((1, tqp), F32))

    for g in range(DSA_KV_HEADS):
        acc = acc_ref[g]
        out = acc[:, :hd] / acc[:, hd:hd + 1]
        for r in range(hpg):
            h = g * hpg + r
            o_ref[:, h * hd:(h + 1) * hd] = out[r * tq:(r + 1) * tq, :].astype(o_ref.dtype)


def _dsa_call(iq, misc, aq, ikcat, kb, vaug, *, q_start, n_keys, tq):
    bsz, lq, _ = iq.shape
    lk = ikcat.shape[1]
    assert lk % (KEY_UNROLL * KEY_CHUNK) == 0 and lq % tq == 0 and lk // 32 <= 256
    tqp = max(tq, LANES)
    topk = min(TOPK_MAX, n_keys // 4)
    rows_g = (DSA_HEADS // DSA_KV_HEADS) * tq
    qblk = lambda n: pl.BlockSpec((None, tq, n), lambda b, q: (b, q, 0))
    kblk = lambda n: pl.BlockSpec((None, lk, n), lambda b, q: (b, 0, 0))
    kern = functools.partial(_dsa_kernel, tq=tq, tqp=tqp, q_start=q_start, n_keys=n_keys, topk=topk,
                             single_block=(lq == tq))
    return pl.pallas_call(
        kern,
        grid=(bsz, lq // tq),
        in_specs=[qblk(iq.shape[2]), qblk(misc.shape[2]), qblk(aq.shape[2]),
                  kblk(ikcat.shape[2]), kblk(kb.shape[2]), kblk(vaug.shape[2])],
        out_specs=qblk(aq.shape[2]),
        out_shape=jax.ShapeDtypeStruct(aq.shape, BF16),
        scratch_shapes=[pltpu.VMEM((lk, tqp), I32),
                        pltpu.VMEM((lk, tqp), I16),
                        pltpu.VMEM((lk, tqp), I16),
                        pltpu.VMEM((DSA_KV_HEADS, rows_g, LANES), F32),
                        pltpu.VMEM((DSA_KV_HEADS, rows_g, 2 * DSA_HEAD_DIM), F32)],
        compiler_params=_params("parallel", "arbitrary"),
        name="dsa",
    )(iq, misc, aq, ikcat, kb, vaug)


def _merge_kernel(x_ref, go_ref, do_ref, gate_ref, wa_ref, wb_ref, wo_ref, g_ref, y_ref):
    d = x_ref.shape[-1]
    a = _dot(go_ref[...], wa_ref[...])
    b = _dot(do_ref[...], wb_ref[...])
    gate = gate_ref[...].astype(F32)
    mixed = jax.nn.sigmoid(gate[:, :d]) * a + jax.nn.sigmoid(gate[:, d:]) * b
    y_ref[...] = x_ref[...] + _rmsnorm(_dot(mixed.astype(BF16), wo_ref[...]), g_ref[...])


def _merge_call(x, go, do, gate, wa, wb, wo, g, tm):
    t, d = x.shape
    row = lambda n: pl.BlockSpec((tm, n), lambda i: (i, 0))
    full = lambda a: pl.BlockSpec(a.shape, lambda i: (0, 0))
    return pl.pallas_call(
        _merge_kernel,
        grid=(t // tm,),
        in_specs=[row(d), row(go.shape[1]), row(do.shape[1]), row(gate.shape[1]),
                  full(wa), full(wb), full(wo), full(g)],
        out_specs=row(d),
        out_shape=jax.ShapeDtypeStruct((t, d), F32),
        compiler_params=_params("parallel"),
        name="merge",
    )(x, go, do, gate, wa, wb, wo, g)


FFN_ROW_CHUNK = 256
FFN_SLAB_ROWS = 64
GELU_C0 = float(np.sqrt(2.0 / np.pi))
GELU_C1 = 0.044715 * GELU_C0


def _ffn_kernel(x_ref, gpre_ref, wu_ref, cw_ref, cb_ref, pv_ref, wdh_ref, gpost_ref,
                y_ref, cn_ref, xn_ref, acc_ref, car_ref, act_ref, u_ref, *, bps, tn):
    i = pl.program_id(0)
    tm = x_ref.shape[0]
    f = wdh_ref.shape[0]
    xn_ref[...] = _rmsnorm(x_ref[...], gpre_ref[...]).astype(BF16)
    acc_ref[...] = jnp.zeros_like(acc_ref)
    first = (i % bps) == 0
    top = lax.broadcasted_iota(I32, (SUBLANES, tn), 0)
    rc = min(tm, FFN_ROW_CHUNK)

    def up(cols, r):
        return _dot(xn_ref[r * rc:(r + 1) * rc, :], wu_ref[:, cols])

    def conv_slab(cols, u, halo):
        r1, r2 = pltpu.roll(u, 1, 0), pltpu.roll(u, 2, 0)
        top1 = jnp.where(top == 0, halo[1:2, :], r1[:SUBLANES, :])
        top2 = jnp.where(top == 0, halo[0:1, :], jnp.where(top == 1, halo[1:2, :], r2[:SUBLANES, :]))
        u1 = jnp.concatenate([top1, r1[SUBLANES:, :]], axis=0)
        u2 = jnp.concatenate([top2, r2[SUBLANES:, :]], axis=0)
        cw = cw_ref[:, cols]
        return cb_ref[:, cols] + (cw[0:1, :] * u2 + cw[1:2, :] * u1 + cw[2:3, :] * u), u[u.shape[0] - 2:, :]

    def keep_tail(cols, slot, half):
        car_ref[:, cols] = u_ref[slot, half, rc - SUBLANES:rc, :]
        cn_ref[:, cols] = u_ref[slot, half, rc - (CONV_W - 1):rc, :]

    def block_halo(cols):
        return jnp.where(first, pv_ref[:, cols], car_ref[SUBLANES - 2:SUBLANES, cols])

    nft = f // tn
    nrc = tm // rc

    def tile(j, aslot, with_down):
        cols_a = pl.ds(pl.multiple_of(j * tn, tn), tn)
        cols_b = pl.ds(pl.multiple_of(f + j * tn, tn), tn)
        halo_a, halo_b = block_halo(cols_a), block_halo(cols_b)

        def up_to(slot, r):
            u_ref[slot, 0] = up(cols_a, r)
            u_ref[slot, 1] = up(cols_b, r)

        up_to(0, 0)
        for r in range(nrc):
            rows = slice(r * rc, (r + 1) * rc)
            slot = r % 2
            if r + 1 < nrc:
                up_to(1 - slot, r + 1)
            if with_down:
                project_down(j - 1, 1 - aslot, rows)
            if r + 1 == nrc:
                keep_tail(cols_a, slot, 0)
                keep_tail(cols_b, slot, 1)
            sl = min(rc, FFN_SLAB_ROWS)
            for s in range(0, rc, sl):
                a, halo_a = conv_slab(cols_a, u_ref[slot, 0, s:s + sl, :], halo_a)
                b, halo_b = conv_slab(cols_b, u_ref[slot, 1, s:s + sl, :], halo_b)
                act = (a * (1.0 + jnp.tanh(a * (GELU_C0 + GELU_C1 * (a * a))))) * b
                act_ref[aslot, r * rc + s:r * rc + s + sl, :] = act.astype(BF16)

    def project_down(j, aslot, rows):
        acc_ref[rows, :] += _dot(act_ref[aslot, rows, :], wdh_ref[pl.ds(pl.multiple_of(j * tn, tn), tn), :])

    assert nft % 2 == 1
    tile(0, 0, False)

    def trip(p, carry):
        tile(2 * p + 1, 1, True)
        tile(2 * p + 2, 0, True)
        return carry

    lax.fori_loop(0, nft // 2, trip, 0)
    for r in range(nrc):
        project_down(nft - 1, 0, slice(r * rc, (r + 1) * rc))
    y_ref[...] = x_ref[...] + _rmsnorm(acc_ref[...], gpost_ref[...])


def _ffn_call(x, gpre, w_up, conv_w, conv_b, conv_prev, w_down_half, gpost, *, tm, seq, tn):
    t, d = x.shape
    f = w_down_half.shape[0]
    nseq = t // seq
    bps = seq // tm
    assert f % tn == 0 and seq % tm == 0
    xrow = pl.BlockSpec((tm, d), lambda i: (i, 0))
    const = lambda a: pl.BlockSpec(a.shape, lambda i: (0, 0), pipeline_mode=pl.Buffered(1))
    y, cn = pl.pallas_call(
        functools.partial(_ffn_kernel, bps=bps, tn=tn),
        grid=(t // tm,),
        in_specs=[xrow, const(gpre), const(w_up), const(conv_w), const(conv_b),
                  pl.BlockSpec((None, CONV_W - 1, 2 * f), lambda i: (i // bps, 0, 0)),
                  const(w_down_half), const(gpost)],
        out_specs=[xrow, pl.BlockSpec((None, CONV_W - 1, 2 * f), lambda i: (i, 0, 0))],
        out_shape=[jax.ShapeDtypeStruct((t, d), F32),
                   jax.ShapeDtypeStruct((t // tm, CONV_W - 1, 2 * f), F32)],
        scratch_shapes=[pltpu.VMEM((tm, d), BF16), pltpu.VMEM((tm, d), F32),
                        pltpu.VMEM((SUBLANES, 2 * f), F32), pltpu.VMEM((2, tm, tn), BF16),
                        pltpu.VMEM((2, 2, min(tm, FFN_ROW_CHUNK), tn), F32)],
        compiler_params=_params("arbitrary"),
        name="ffn",
    )(x, gpre, w_up, conv_w, conv_b, conv_prev, w_down_half, gpost)
    return y, cn.reshape(nseq, bps, CONV_W - 1, 2 * f)[:, bps - 1]


def _prep_weights(w_in, w_gate_up, d_model):
    sizes = (GLA_HEADS * GLA_DK, GLA_HEADS * GLA_DK, GLA_HEADS * GLA_DV, GLA_RANK, GLA_HEADS * GLA_DV,
             DSA_HEADS * DSA_HEAD_DIM, DSA_KV_HEADS * DSA_HEAD_DIM, DSA_KV_HEADS * DSA_HEAD_DIM,
             IDX_HEADS * IDX_DIM, IDX_DIM, IDX_HEADS, 2 * d_model)
    offs = np.concatenate([[0], np.cumsum(sizes)])
    g_q, g_k, g_v, g_low, g_r, a_q, a_k, a_v, i_q, i_k, i_w, gate = (
        w_in[:, int(offs[n]):int(offs[n + 1])] for n in range(len(sizes)))
    w_main = jnp.concatenate([g_q, g_k, g_v, g_r, a_q, a_k, a_v, gate], axis=1).astype(BF16)
    zeros = lambda n: jnp.zeros((w_in.shape[0], n), w_in.dtype)
    misc = jnp.concatenate([g_low, i_w, zeros(LANES - GLA_RANK - IDX_HEADS)], axis=1)
    assert MISC_GLOW == 0 and MISC_IW == GLA_RANK
    w_hp = jnp.concatenate([i_q, i_k, zeros(LANES - IDX_DIM), misc], axis=1)
    assert w_hp.shape[1] == HP_WIDTH
    w_hp_hi, w_hp_lo = _split_bf16(w_hp)
    wup = jnp.concatenate([w_gate_up, jnp.zeros((LANES - GLA_RANK, w_gate_up.shape[1]), w_gate_up.dtype)], axis=0)
    wup_hi, wup_lo = _split_bf16(wup)
    return w_main, w_hp_hi, w_hp_lo, jnp.concatenate([wup_hi, wup_hi, wup_lo], axis=0)


def _run_layer(x, k_past, v_past, kidx_past, gla_s0, conv_prev, w, *, tm, gla_tb, dsa_tq, ffn_tm):
    bsz, seq, d = x.shape
    past = k_past.shape[1]
    t = bsz * seq
    x2 = x.reshape(t, d)
    gq, gk, gv, gr, aq, ak, av, gate, iq, ik, misc, ikcat, kb, vaug = _proj_call(
        x2, w["attn_pre_norm"], w["w_main"], w["w_hp_hi"], w["w_hp_lo"], tm)
    per_seq = lambda a: a.reshape(bsz, seq, a.shape[-1])

    gla_o, gla_s = _gla_call(per_seq(gq), per_seq(gk), per_seq(gv), per_seq(gr), per_seq(misc),
                             w["wup"], w["b_gla_gate"], w["gla_head_norm"], gla_s0, gla_tb)

    n_keys = past + seq
    key_span = KEY_UNROLL * KEY_CHUNK
    lk = -(-n_keys // key_span) * key_span
    packed = [per_seq(a) for a in (ikcat, kb, vaug)]
    if past:
        flat = lambda a: a.reshape(bsz * past, -1).astype(F32)
        packed_past = _pack_call(flat(kidx_past), flat(k_past), flat(v_past), tm)
        packed = [jnp.concatenate([p.reshape(bsz, past, -1), n], axis=1) for p, n in zip(packed_past, packed)]
    if lk > n_keys:
        packed = [jnp.pad(a, ((0, 0), (0, lk - n_keys), (0, 0))) for a in packed]

    dsa_o = _dsa_call(per_seq(iq), per_seq(misc), per_seq(aq), *packed,
                      q_start=past, n_keys=n_keys, tq=dsa_tq)

    x1 = _merge_call(x2, gla_o.reshape(t, -1), dsa_o.reshape(t, -1), gate,
                     w["w_branch_gla"], w["w_branch_dsa"], w["w_out"], w["attn_post_norm"], tm)
    y, conv_new = _ffn_call(x1, w["ffn_pre_norm"], w["w_up"], w["conv_w"], w["conv_b"], conv_prev,
                            w["w_down_half"], w["ffn_post_norm"], tm=ffn_tm, seq=seq, tn=256)
    k_new = ak.reshape(bsz, seq, DSA_KV_HEADS, DSA_HEAD_DIM)
    v_new = av.reshape(bsz, seq, DSA_KV_HEADS, DSA_HEAD_DIM)
    return y.reshape(bsz, seq, d), k_new, v_new, per_seq(ik), gla_s, conv_new


def kernel(x_prompt, x_sample, cache_k, cache_v, cache_k_idx, state_gla, state_ffn_conv, attn_pre_norm, w_in, w_gla_gate_up, b_gla_gate, gla_head_norm, w_branch_gla, w_branch_dsa, w_out, attn_post_norm, ffn_pre_norm, w_up, conv_w, conv_b, w_down, ffn_post_norm):
    depth = w_in.shape[0]
    bsz, _, d = x_prompt.shape
    dt = x_prompt.dtype
    y_prompt, y_sample = x_prompt, x_sample
    prompt_new, sample_new = [], []
    row = lambda a: a.reshape(1, -1)
    for l in range(depth):
        w_main, w_hp_hi, w_hp_lo, wup = _prep_weights(w_in[l], w_gla_gate_up[l], d)
        w = dict(attn_pre_norm=row(attn_pre_norm[l]), w_main=w_main, w_hp_hi=w_hp_hi, w_hp_lo=w_hp_lo, wup=wup,
                 b_gla_gate=row(b_gla_gate[l]), gla_head_norm=row(gla_head_norm[l]),
                 w_branch_gla=w_branch_gla[l].astype(BF16), w_branch_dsa=w_branch_dsa[l].astype(BF16),
                 w_out=w_out[l].astype(BF16), attn_post_norm=row(attn_post_norm[l]),
                 ffn_pre_norm=row(ffn_pre_norm[l]), w_up=w_up[l].astype(BF16), conv_w=conv_w[l],
                 conv_b=row(conv_b[l]), w_down_half=(0.5 * w_down[l]).astype(BF16),
                 ffn_post_norm=row(ffn_post_norm[l]))
        ffn2 = conv_w.shape[-1]
        y_prompt, *new = _run_layer(
            y_prompt,
            jnp.zeros((bsz, 0, DSA_KV_HEADS * DSA_HEAD_DIM), dt),
            jnp.zeros((bsz, 0, DSA_KV_HEADS * DSA_HEAD_DIM), dt),
            jnp.zeros((bsz, 0, IDX_DIM), dt),
            jnp.zeros((bsz, GLA_HEADS, GLA_DK, GLA_DV), F32),
            jnp.zeros((bsz, CONV_W - 1, ffn2), dt),
            w, tm=256, gla_tb=512, dsa_tq=256, ffn_tm=1024)
        prompt_new.append(new)
        y_sample, *new = _run_layer(
            y_sample, cache_k[l], cache_v[l], cache_k_idx[l], state_gla[l], state_ffn_conv[l],
            w, tm=256, gla_tb=x_sample.shape[1], dsa_tq=x_sample.shape[1], ffn_tm=x_sample.shape[1])
        sample_new.append(new)
    k_p, v_p, kidx_p, gla_p, conv_p = (jnp.stack(t, axis=0) for t in zip(*prompt_new))
    k_s, v_s, kidx_s, gla_s, conv_s = (jnp.stack(t, axis=0) for t in zip(*sample_new))
    return (y_prompt, y_sample, k_p, v_p, kidx_p, gla_p, conv_p, k_s, v_s, kidx_s, gla_s, conv_s)
```

```python
import functools

import numpy as np
import jax
import jax.numpy as jnp
from jax import lax
from jax.experimental import pallas as pl
from jax.experimental.pallas import tpu as pltpu

F32 = jnp.float32
BF16 = jnp.bfloat16
I32 = jnp.int32
I16 = jnp.int16
HIGHEST = lax.Precision.HIGHEST

CHUNK = 64
GLA_HEADS = 4
GLA_DK = 128
GLA_DV = 256
GLA_RANK = 16
GLA_GATE_TAU = 16.0
DSA_HEADS = 8
DSA_KV_HEADS = 2
DSA_HEAD_DIM = 64
IDX_HEADS = 4
IDX_DIM = 64
IDX_SCALE = (IDX_HEADS * IDX_DIM) ** -0.5
TOPK_MAX = 256
CONV_W = 3
EPS = 1e-6

LANES = 128
SUBLANES = 8
VMEM_LIMIT = 56 * 1024 * 1024

MISC_GLOW = 0
MISC_IW = 16

KEY_CHUNK = 256
KEY_UNROLL = 2
GLA_UNROLL = 8
INT16_MIN = -(2 ** 15)
MASKED = -1e30
NT_DIMS = (((1,), (1,)), ((), ()))
TN_DIMS = (((0,), (0,)), ((), ()))


def _dot(a, b, **kw):
    return jnp.dot(a, b, preferred_element_type=F32, **kw)


def _dot_nt(a, b):
    return lax.dot_general(a, b, NT_DIMS, preferred_element_type=F32)


def _split_bf16(x):
    hi = x.astype(BF16)
    lo = (x - hi.astype(F32)).astype(BF16)
    return hi, lo


def _rmsnorm(x, g):
    return x * lax.rsqrt(jnp.mean(x * x, axis=-1, keepdims=True) + EPS) * g


def _params(*sem):
    return pltpu.CompilerParams(dimension_semantics=sem, vmem_limit_bytes=VMEM_LIMIT)


MAIN_SIZES = (GLA_HEADS * GLA_DK, GLA_HEADS * GLA_DK, GLA_HEADS * GLA_DV, GLA_HEADS * GLA_DV,
              DSA_HEADS * DSA_HEAD_DIM, DSA_KV_HEADS * DSA_HEAD_DIM, DSA_KV_HEADS * DSA_HEAD_DIM)
HP_WIDTH = 512
PACK_WIDTHS = (3 * IDX_DIM, DSA_KV_HEADS * DSA_HEAD_DIM, 2 * DSA_KV_HEADS * DSA_HEAD_DIM)


def _pack_keys(ik, k, v):
    hi, lo = _split_bf16(ik)
    ikcat = jnp.concatenate([hi, hi, lo], axis=1)
    vb = v.astype(BF16)
    ones = jnp.ones((v.shape[0], DSA_HEAD_DIM), BF16)
    parts = []
    for g in range(DSA_KV_HEADS):
        parts += [vb[:, g * DSA_HEAD_DIM:(g + 1) * DSA_HEAD_DIM], ones]
    return ikcat, k.astype(BF16), jnp.concatenate(parts, axis=1)


def _proj_kernel(x_ref, g_ref, wm_ref, whi_ref, wlo_ref,
                 gq_ref, gk_ref, gv_ref, gr_ref, aq_ref, ak_ref, av_ref, gate_ref, iq_ref, ik_ref, misc_ref,
                 ikcat_ref, kb_ref, vaug_ref):
    xn = _rmsnorm(x_ref[...], g_ref[...])
    xh, xl = _split_bf16(xn)
    outs = (gq_ref, gk_ref, gv_ref, gr_ref, aq_ref, ak_ref, av_ref, gate_ref)
    off = 0
    vals = []
    for o_ref in outs:
        n = o_ref.shape[-1]
        vals.append(_dot(xh, wm_ref[:, off:off + n]))
        o_ref[...] = vals[-1].astype(o_ref.dtype)
        off += n
    ak, av = vals[5], vals[6]
    whi = whi_ref[...]
    hp = _dot(xh, whi) + _dot(xl, whi) + _dot(xh, wlo_ref[...])
    iq_ref[...] = hp[:, 0:256]
    ik_ref[...] = hp[:, 256:320]
    misc_ref[...] = hp[:, 384:512]
    ikcat_ref[...], kb_ref[...], vaug_ref[...] = _pack_keys(hp[:, 256:320], ak, av)


def _proj_call(x, g, wm, whi, wlo, tm):
    t, d = x.shape
    gate_w = wm.shape[1] - sum(MAIN_SIZES)
    widths = MAIN_SIZES + (gate_w,)
    dtypes = (BF16, BF16, BF16, BF16, BF16, F32, F32, BF16)
    out_shape = [jax.ShapeDtypeStruct((t, n), dt) for n, dt in zip(widths, dtypes)]
    out_shape += [jax.ShapeDtypeStruct((t, n), F32) for n in (256, 64, 128)]
    out_shape += [jax.ShapeDtypeStruct((t, n), BF16) for n in PACK_WIDTHS]
    row = lambda n: pl.BlockSpec((tm, n), lambda i: (i, 0))
    full = lambda a: pl.BlockSpec(a.shape, lambda i: (0, 0))
    return pl.pallas_call(
        _proj_kernel,
        grid=(t // tm,),
        in_specs=[row(d), full(g), full(wm), full(whi), full(wlo)],
        out_specs=[row(s.shape[1]) for s in out_shape],
        out_shape=out_shape,
        compiler_params=_params("parallel"),
        name="proj",
    )(x, g, wm, whi, wlo)


def _pack_kernel(ik_ref, k_ref, v_ref, ikcat_ref, kb_ref, vaug_ref):
    ikcat_ref[...], kb_ref[...], vaug_ref[...] = _pack_keys(ik_ref[...], k_ref[...], v_ref[...])


def _pack_call(ik, k, v, tm):
    t = ik.shape[0]
    row = lambda n: pl.BlockSpec((tm, n), lambda i: (i, 0))
    return pl.pallas_call(
        _pack_kernel,
        grid=(t // tm,),
        in_specs=[row(ik.shape[1]), row(k.shape[1]), row(v.shape[1])],
        out_specs=[row(n) for n in PACK_WIDTHS],
        out_shape=[jax.ShapeDtypeStruct((t, n), BF16) for n in PACK_WIDTHS],
        compiler_params=_params("parallel"),
        name="pack",
    )(ik, k, v)


def _gla_kernel(gq_ref, gk_ref, gv_ref, gr_ref, misc_ref, wup_ref, bup_ref, hn_ref, s0_ref,
                o_ref, sfin_ref, st_ref, qd_ref, ki_ref, kr_ref, dl_ref, *, nchunks):
    j = pl.program_id(1)

    @pl.when(j == 0)
    def _():
        for h in range(GLA_HEADS):
            st_ref[h] = s0_ref[h].T

    row = lax.broadcasted_iota(I32, (CHUNK, CHUNK), 0)
    col = lax.broadcasted_iota(I32, (CHUNK, CHUNK), 1)
    causal = row >= col
    tri = causal.astype(BF16)

    m_hi, m_lo = _split_bf16(misc_ref[...])
    z = _dot(jnp.concatenate([m_hi, m_lo, m_hi], axis=1), wup_ref[...]) + bup_ref[...]
    log_a = (jnp.minimum(z, 0.0) - jnp.log(1.0 + jnp.exp(-jnp.abs(z)))) * (1.0 / GLA_GATE_TAU)
    la_hi, la_lo = _split_bf16(log_a)
    tri2 = jnp.concatenate([tri, tri], axis=1)
    for c in range(nchunks):
        rows = slice(c * CHUNK, (c + 1) * CHUNK)
        b = _dot(tri2, jnp.concatenate([la_hi[rows, :], la_lo[rows, :]], axis=0))
        b_last = b[CHUNK - 1:CHUNK, :]
        q = gq_ref[rows, :].astype(F32)
        k = gk_ref[rows, :].astype(F32)
        qd_ref[rows, :] = (q * (GLA_DK ** -0.5) * jnp.exp(b)).astype(BF16)
        ki_ref[rows, :] = (k * jnp.exp(-b)).astype(BF16)
        kr_ref[rows, :] = (k * jnp.exp(b_last - b)).astype(BF16)
        dl_ref[c] = jnp.broadcast_to(jnp.exp(b_last), dl_ref.shape[1:])

    def chunk(c, carry):
        rows = pl.ds(pl.multiple_of(c * CHUNK, CHUNK), CHUNK)
        q_dec, k_inv, k_rem = qd_ref[rows, :], ki_ref[rows, :], kr_ref[rows, :]
        d_last = dl_ref[c][0:1, :]
        v = gv_ref[rows, :]
        gr = gr_ref[rows, :].astype(F32)
        hn = hn_ref[...]
        for h in range(GLA_HEADS):
            ks = slice(h * GLA_DK, (h + 1) * GLA_DK)
            vs = slice(h * GLA_DV, (h + 1) * GLA_DV)
            qh, vh = q_dec[:, ks], v[:, vs]
            scores = jnp.where(causal, _dot_nt(qh, k_inv[:, ks]), 0.0)
            st = st_ref[h]
            o = _dot(scores.astype(BF16), vh) + _dot_nt(qh, st.astype(BF16))
            ut = lax.dot_general(vh, k_rem[:, ks], TN_DIMS, preferred_element_type=F32)
            st_ref[h] = st * d_last[:, ks] + ut
            y = _rmsnorm(o, hn)
            g = gr[:, vs]
            o_ref[rows, vs] = (y * (g * jax.nn.sigmoid(g))).astype(o_ref.dtype)
        return carry

    lax.fori_loop(0, nchunks, chunk, 0, unroll=min(nchunks, GLA_UNROLL))

    @pl.when(j == pl.num_programs(1) - 1)
    def _():
        for h in range(GLA_HEADS):
            sfin_ref[h] = st_ref[h].T


def _gla_call(gq, gk, gv, gr, misc, wup, bup, hn, s0, tb):
    bsz, seq, _ = gq.shape
    blk = lambda n: pl.BlockSpec((None, tb, n), lambda b, j: (b, j, 0))
    full2 = lambda a: pl.BlockSpec(a.shape, lambda b, j: (0, 0))
    st_spec = pl.BlockSpec((None, GLA_HEADS, GLA_DK, GLA_DV), lambda b, j: (b, 0, 0, 0))
    return pl.pallas_call(
        functools.partial(_gla_kernel, nchunks=tb // CHUNK),
        grid=(bsz, seq // tb),
        in_specs=[blk(gq.shape[2]), blk(gk.shape[2]), blk(gv.shape[2]), blk(gr.shape[2]), blk(misc.shape[2]),
                  full2(wup), full2(bup), full2(hn), st_spec],
        out_specs=[blk(gv.shape[2]), st_spec],
        out_shape=[jax.ShapeDtypeStruct((bsz, seq, gv.shape[2]), BF16),
                   jax.ShapeDtypeStruct(s0.shape, F32)],
        scratch_shapes=[pltpu.VMEM((GLA_HEADS, GLA_DV, GLA_DK), F32),
                        pltpu.VMEM((tb, gq.shape[2]), BF16), pltpu.VMEM((tb, gq.shape[2]), BF16),
                        pltpu.VMEM((tb, gq.shape[2]), BF16),
                        pltpu.VMEM((tb // CHUNK, SUBLANES, gq.shape[2]), F32)],
        compiler_params=_params("parallel", "arbitrary"),
        name="gla",
    )(gq, gk, gv, gr, misc, wup, bup, hn, s0)


def _dsa_kernel(iq_ref, misc_ref, aq_ref, ikcat_ref, kb_ref, vaug_ref, o_ref,
                keys_ref, hi_ref, lo_ref, m_ref, acc_ref,
                *, tq, tqp, q_start, n_keys, topk, single_block):
    qi = 0 if single_block else pl.program_id(1)
    kc = KEY_CHUNK
    hd = DSA_HEAD_DIM
    hpg = DSA_HEADS // DSA_KV_HEADS

    first_q = q_start + qi * tq
    last_q = first_q + tq - 1
    span = KEY_UNROLL * kc
    lowest = min if single_block else jnp.minimum
    nt = (lowest(n_keys, ((last_q >> 6) + 1) * CHUNK) + span - 1) // span
    nfull = lowest(n_keys, ((first_q >> 6) + 1) * CHUNK) // span

    def loop(lo, hi, trip, carry):
        if not single_block:
            return lax.fori_loop(lo, hi, trip, carry)
        for t in range(lo, hi):
            carry = trip(t, carry)
        return carry

    def trips(body, masked):
        def trip(t, carry):
            for u in range(KEY_UNROLL):
                carry = body(pl.multiple_of((t * KEY_UNROLL + u) * kc, kc), masked, carry)
            return carry
        return trip

    def for_chunks(body, init):
        return loop(nfull, nt, trips(body, True), loop(0, nfull, trips(body, False), init))

    def for_all_chunks(body, init):
        return loop(0, nt, trips(lambda r0, _, carry: body(r0, carry), None), init)

    def pad_rows(x):
        if tqp == tq:
            return x
        return jnp.concatenate([x, jnp.zeros((tqp - tq, x.shape[1]), x.dtype)], axis=0)

    iq_hi, iq_lo = _split_bf16(pad_rows(iq_ref[...]))
    iq_cat = []
    for h in range(IDX_HEADS):
        s = slice(h * IDX_DIM, (h + 1) * IDX_DIM)
        iq_cat.append(jnp.concatenate([iq_hi[:, s], iq_lo[:, s], iq_hi[:, s]], axis=1))
    wts = pad_rows(misc_ref[...]).T[MISC_IW:MISC_IW + IDX_HEADS, :] * IDX_SCALE
    q_chunk = (first_q + lax.broadcasted_iota(I32, (1, tqp), 1)) >> 6

    def admissible(r0):
        k_pos = r0 + lax.broadcasted_iota(I32, (kc, tqp), 0)
        return ((k_pos >> 6) <= q_chunk) & (k_pos < n_keys)

    def score_body(r0, masked, carry):
        rows = pl.ds(r0, kc)
        ik = ikcat_ref[rows, :]
        acc = jnp.zeros((kc, tqp), F32)
        for h in range(IDX_HEADS):
            acc = acc + wts[h:h + 1, :] * jnp.maximum(_dot_nt(ik, iq_cat[h]), 0.0)
        if masked:
            acc = jnp.where(admissible(r0), acc, -jnp.inf)
        bits = lax.bitcast_convert_type(acc, I32)
        key = jnp.where(bits < 0, bits ^ jnp.int32(0x7FFFFFFF), bits)
        keys_ref[rows, :] = key
        hi_ref[rows, :] = (key >> 16).astype(I16)
        lo_ref[rows, :] = ((key & 0xFFFF) + INT16_MIN).astype(I16)
        return carry

    for_chunks(score_body, 0)

    grp = 32
    one, zero = jnp.asarray(1, BF16), jnp.asarray(0, BF16)

    def count16(ref, pred):
        def body(r0, acc):
            m = jnp.where(pred(ref[pl.ds(r0, kc), :]), one, zero)
            for i in range(kc // grp):
                acc = acc + m[i * grp:(i + 1) * grp, :]
            return acc
        acc = for_all_chunks(body, jnp.zeros((grp, tqp), BF16))
        return jnp.sum(acc.astype(F32), axis=0, keepdims=True)

    def search16(ref, kth):
        def bit_body(i, prefix):
            cand = prefix | (jnp.int32(1) << (15 - i))
            cand16 = (cand + INT16_MIN).astype(I16)
            tot = count16(ref, lambda ch: ch >= cand16)
            return jnp.where(tot >= kth, cand, prefix)
        return lax.fori_loop(0, 16, bit_body, jnp.zeros((1, tqp), I32))

    kth = jnp.full((1, tqp), topk, F32)
    a_u = search16(hi_ref, kth)
    a16 = (a_u + INT16_MIN).astype(I16)
    kth_lo = kth - count16(hi_ref, lambda ch: ch > a16)

    def bucket_body(r0, carry):
        rows = pl.ds(r0, kc)
        lo_ref[rows, :] = jnp.where(hi_ref[rows, :] == a16, lo_ref[rows, :], jnp.asarray(INT16_MIN, I16))
        return carry

    for_all_chunks(bucket_body, 0)
    b_u = search16(lo_ref, kth_lo)
    b16 = (b_u + INT16_MIN).astype(I16)
    thr = ((a_u + INT16_MIN) << 16) | b_u
    need = kth_lo - count16(lo_ref, lambda ch: ch > b16)

    tri = (lax.broadcasted_iota(I32, (kc, kc), 0) >= lax.broadcasted_iota(I32, (kc, kc), 1)).astype(BF16)

    def sel_bias(r0, masked, running):
        ch = keys_ref[pl.ds(r0, kc), :]
        eq = ch == thr
        rank = _dot(tri, jnp.where(eq, 1.0, 0.0).astype(BF16)) + running
        sel = (ch > thr) | (eq & (rank <= need))
        if masked:
            sel = sel & admissible(r0)
        return jnp.where(sel, 0.0, MASKED).T[:tq, :], rank[kc - 1:kc, :]

    m_ref[...] = jnp.full(m_ref.shape, MASKED, F32)
    acc_ref[...] = jnp.zeros(acc_ref.shape, F32)
    q_groups = []
    for g in range(DSA_KV_HEADS):
        qg = jnp.concatenate([aq_ref[:, h * hd:(h + 1) * hd] for h in range(g * hpg, (g + 1) * hpg)], axis=0)
        q_groups.append(qg * jnp.asarray(hd ** -0.5, BF16))

    def attend(t, masked, running):
        r0 = pl.multiple_of(t * span, span)
        cols = pl.ds(r0, span)
        biases = []
        for u in range(KEY_UNROLL):
            bias, running = sel_bias(pl.multiple_of(r0 + u * kc, kc), masked, running)
            biases.append(bias)
        bias = jnp.concatenate(biases, axis=1)
        for g in range(DSA_KV_HEADS):
            s = _dot_nt(q_groups[g], kb_ref[cols, g * hd:(g + 1) * hd])
            s = jnp.concatenate([s[r * tq:(r + 1) * tq, :] + bias for r in range(hpg)], axis=0)
            tiles = [s[:, i * LANES:(i + 1) * LANES] for i in range(span // LANES)]
            m_old = m_ref[g]
            m_new = jnp.maximum(m_old, jnp.max(functools.reduce(jnp.maximum, tiles), axis=1, keepdims=True))
            p = jnp.concatenate([jnp.exp(t - m_new) for t in tiles], axis=1)
            pv = _dot(p.astype(BF16), vaug_ref[cols, 2 * g * hd:2 * (g + 1) * hd])
            acc_ref[g] = acc_ref[g] * jnp.exp(m_old - m_new) + pv
            m_ref[g] = m_new
        return running

    no_ties = jnp.zeros((1, tqp), F32)
    loop(nfull, nt, lambda t, c: attend(t, True, c), loop(0, nfull, lambda t, c: attend(t, False, c), no_ties))

    for g in range(DSA_KV_HEADS):
        acc = acc_ref[g]
        out = acc[:, :hd] / acc[:, hd:hd + 1]
        for r in range(hpg):
            h = g * hpg + r
            o_ref[:, h * hd:(h + 1) * hd] = out[r * tq:(r + 1) * tq, :].astype(o_ref.dtype)


def _dsa_call(iq, misc, aq, ikcat, kb, vaug, *, q_start, n_keys, tq):
    bsz, lq, _ = iq.shape
    lk = ikcat.shape[1]
    assert lk % (KEY_UNROLL * KEY_CHUNK) == 0 and lq % tq == 0 and lk // 32 <= 256
    tqp = max(tq, LANES)
    topk = min(TOPK_MAX, n_keys // 4)
    rows_g = (DSA_HEADS // DSA_KV_HEADS) * tq
    qblk = lambda n: pl.BlockSpec((None, tq, n), lambda b, q: (b, q, 0))
    kblk = lambda n: pl.BlockSpec((None, lk, n), lambda b, q: (b, 0, 0))
    kern = functools.partial(_dsa_kernel, tq=tq, tqp=tqp, q_start=q_start, n_keys=n_keys, topk=topk,
                             single_block=(lq == tq))
    return pl.pallas_call(
        kern,
        grid=(bsz, lq // tq),
        in_specs=[qblk(iq.shape[2]), qblk(misc.shape[2]), qblk(aq.shape[2]),
                  kblk(ikcat.shape[2]), kblk(kb.shape[2]), kblk(vaug.shape[2])],
        out_specs=qblk(aq.shape[2]),
        out_shape=jax.ShapeDtypeStruct(aq.shape, BF16),
        scratch_shapes=[pltpu.VMEM((lk, tqp), I32),
                        pltpu.VMEM((lk, tqp), I16),
                        pltpu.VMEM((lk, tqp), I16),
                        pltpu.VMEM((DSA_KV_HEADS, rows_g, LANES), F32),
                        pltpu.VMEM((DSA_KV_HEADS, rows_g, 2 * DSA_HEAD_DIM), F32)],
        compiler_params=_params("parallel", "arbitrary"),
        name="dsa",
    )(iq, misc, aq, ikcat, kb, vaug)


def _merge_kernel(x_ref, go_ref, do_ref, gate_ref, wa_ref, wb_ref, wo_ref, g_ref, y_ref):
    d = x_ref.shape[-1]
    a = _dot(go_ref[...], wa_ref[...])
    b = _dot(do_ref[...], wb_ref[...])
    gate = gate_ref[...].astype(F32)
    mixed = jax.nn.sigmoid(gate[:, :d]) * a + jax.nn.sigmoid(gate[:, d:]) * b
    y_ref[...] = x_ref[...] + _rmsnorm(_dot(mixed.astype(BF16), wo_ref[...]), g_ref[...])


def _merge_call(x, go, do, gate, wa, wb, wo, g, tm):
    t, d = x.shape
    row = lambda n: pl.BlockSpec((tm, n), lambda i: (i, 0))
    full = lambda a: pl.BlockSpec(a.shape, lambda i: (0, 0))
    return pl.pallas_call(
        _merge_kernel,
        grid=(t // tm,),
        in_specs=[row(d), row(go.shape[1]), row(do.shape[1]), row(gate.shape[1]),
                  full(wa), full(wb), full(wo), full(g)],
        out_specs=row(d),
        out_shape=jax.ShapeDtypeStruct((t, d), F32),
        compiler_params=_params("parallel"),
        name="merge",
    )(x, go, do, gate, wa, wb, wo, g)


FFN_ROW_CHUNK = 256
FFN_SLAB_ROWS = 64
GELU_C0 = float(np.sqrt(2.0 / np.pi))
GELU_C1 = 0.044715 * GELU_C0


def _ffn_kernel(x_ref, gpre_ref, wu_ref, cw_ref, cb_ref, pv_ref, wdh_ref, gpost_ref,
                y_ref, cn_ref, xn_ref, acc_ref, car_ref, act_ref, u_ref, *, bps, tn):
    i = pl.program_id(0)
    tm = x_ref.shape[0]
    f = wdh_ref.shape[0]
    xn_ref[...] = _rmsnorm(x_ref[...], gpre_ref[...]).astype(BF16)
    acc_ref[...] = jnp.zeros_like(acc_ref)
    first = (i % bps) == 0
    top = lax.broadcasted_iota(I32, (SUBLANES, tn), 0)
    rc = min(tm, FFN_ROW_CHUNK)

    def up(cols, r):
        return _dot(xn_ref[r * rc:(r + 1) * rc, :], wu_ref[:, cols])

    def conv_slab(cols, u, halo):
        r1, r2 = pltpu.roll(u, 1, 0), pltpu.roll(u, 2, 0)
        top1 = jnp.where(top == 0, halo[1:2, :], r1[:SUBLANES, :])
        top2 = jnp.where(top == 0, halo[0:1, :], jnp.where(top == 1, halo[1:2, :], r2[:SUBLANES, :]))
        u1 = jnp.concatenate([top1, r1[SUBLANES:, :]], axis=0)
        u2 = jnp.concatenate([top2, r2[SUBLANES:, :]], axis=0)
        cw = cw_ref[:, cols]
        return cb_ref[:, cols] + (cw[0:1, :] * u2 + cw[1:2, :] * u1 + cw[2:3, :] * u), u[u.shape[0] - 2:, :]

    def keep_tail(cols, slot, half):
        car_ref[:, cols] = u_ref[slot, half, rc - SUBLANES:rc, :]
        cn_ref[:, cols] = u_ref[slot, half, rc - (CONV_W - 1):rc, :]

    def block_halo(cols):
        return jnp.where(first, pv_ref[:, cols], car_ref[SUBLANES - 2:SUBLANES, cols])

    nft = f // tn
    nrc = tm // rc

    def tile(j, aslot, with_down):
        cols_a = pl.ds(pl.multiple_of(j * tn, tn), tn)
        cols_b = pl.ds(pl.multiple_of(f + j * tn, tn), tn)
        halo_a, halo_b = block_halo(cols_a), block_halo(cols_b)

        def up_to(slot, r):
            u_ref[slot, 0] = up(cols_a, r)
            u_ref[slot, 1] = up(cols_b, r)

        up_to(0, 0)
        for r in range(nrc):
            rows = slice(r * rc, (r + 1) * rc)
            slot = r % 2
            if r + 1 < nrc:
                up_to(1 - slot, r + 1)
            if with_down:
                project_down(j - 1, 1 - aslot, rows)
            if r + 1 == nrc:
                keep_tail(cols_a, slot, 0)
                keep_tail(cols_b, slot, 1)
            sl = min(rc, FFN_SLAB_ROWS)
            for s in range(0, rc, sl):
                a, halo_a = conv_slab(cols_a, u_ref[slot, 0, s:s + sl, :], halo_a)
                b, halo_b = conv_slab(cols_b, u_ref[slot, 1, s:s + sl, :], halo_b)
                act = (a * (1.0 + jnp.tanh(a * (GELU_C0 + GELU_C1 * (a * a))))) * b
                act_ref[aslot, r * rc + s:r * rc + s + sl, :] = act.astype(BF16)

    def project_down(j, aslot, rows):
        acc_ref[rows, :] += _dot(act_ref[aslot, rows, :], wdh_ref[pl.ds(pl.multiple_of(j * tn, tn), tn), :])

    assert nft % 2 == 1
    tile(0, 0, False)

    def trip(p, carry):
        tile(2 * p + 1, 1, True)
        tile(2 * p + 2, 0, True)
        return carry

    lax.fori_loop(0, nft // 2, trip, 0)
    for r in range(nrc):
        project_down(nft - 1, 0, slice(r * rc, (r + 1) * rc))
    y_ref[...] = x_ref[...] + _rmsnorm(acc_ref[...], gpost_ref[...])


def _ffn_call(x, gpre, w_up, conv_w, conv_b, conv_prev, w_down_half, gpost, *, tm, seq, tn):
    t, d = x.shape
    f = w_down_half.shape[0]
    nseq = t // seq
    bps = seq // tm
    assert f % tn == 0 and seq % tm == 0
    xrow = pl.BlockSpec((tm, d), lambda i: (i, 0))
    const = lambda a: pl.BlockSpec(a.shape, lambda i: (0, 0), pipeline_mode=pl.Buffered(1))
    y, cn = pl.pallas_call(
        functools.partial(_ffn_kernel, bps=bps, tn=tn),
        grid=(t // tm,),
        in_specs=[xrow, const(gpre), const(w_up), const(conv_w), const(conv_b),
                  pl.BlockSpec((None, CONV_W - 1, 2 * f), lambda i: (i // bps, 0, 0)),
                  const(w_down_half), const(gpost)],
        out_specs=[xrow, pl.BlockSpec((None, CONV_W - 1, 2 * f), lambda i: (i, 0, 0))],
        out_shape=[jax.ShapeDtypeStruct((t, d), F32),
                   jax.ShapeDtypeStruct((t // tm, CONV_W - 1, 2 * f), F32)],
        scratch_shapes=[pltpu.VMEM((tm, d), BF16), pltpu.VMEM((tm, d), F32),
                        pltpu.VMEM((SUBLANES, 2 * f), F32), pltpu.VMEM((2, tm, tn), BF16),
                        pltpu.VMEM((2, 2, min(tm, FFN_ROW_CHUNK), tn), F32)],
        compiler_params=_params("arbitrary"),
        name="ffn",
    )(x, gpre, w_up, conv_w, conv_b, conv_prev, w_down_half, gpost)
    return y, cn.reshape(nseq, bps, CONV_W - 1, 2 * f)[:, bps - 1]


def _prep_weights(w_in, w_gate_up, d_model):
    sizes = (GLA_HEADS * GLA_DK, GLA_HEADS * GLA_DK, GLA_HEADS * GLA_DV, GLA_RANK, GLA_HEADS * GLA_DV,
             DSA_HEADS * DSA_HEAD_DIM, DSA_KV_HEADS * DSA_HEAD_DIM, DSA_KV_HEADS * DSA_HEAD_DIM,
             IDX_HEADS * IDX_DIM, IDX_DIM, IDX_HEADS, 2 * d_model)
    offs = np.concatenate([[0], np.cumsum(sizes)])
    g_q, g_k, g_v, g_low, g_r, a_q, a_k, a_v, i_q, i_k, i_w, gate = (
        w_in[:, int(offs[n]):int(offs[n + 1])] for n in range(len(sizes)))
    w_main = jnp.concatenate([g_q, g_k, g_v, g_r, a_q, a_k, a_v, gate], axis=1).astype(BF16)
    zeros = lambda n: jnp.zeros((w_in.shape[0], n), w_in.dtype)
    misc = jnp.concatenate([g_low, i_w, zeros(LANES - GLA_RANK - IDX_HEADS)], axis=1)
    assert MISC_GLOW == 0 and MISC_IW == GLA_RANK
    w_hp = jnp.concatenate([i_q, i_k, zeros(LANES - IDX_DIM), misc], axis=1)
    assert w_hp.shape[1] == HP_WIDTH
    w_hp_hi, w_hp_lo = _split_bf16(w_hp)
    wup = jnp.concatenate([w_gate_up, jnp.zeros((LANES - GLA_RANK, w_gate_up.shape[1]), w_gate_up.dtype)], axis=0)
    wup_hi, wup_lo = _split_bf16(wup)
    return w_main, w_hp_hi, w_hp_lo, jnp.concatenate([wup_hi, wup_hi, wup_lo], axis=0)


def _run_layer(x, k_past, v_past, kidx_past, gla_s0, conv_prev, w, *, tm, gla_tb, dsa_tq, ffn_tm):
    bsz, seq, d = x.shape
    past = k_past.shape[1]
    t = bsz * seq
    x2 = x.reshape(t, d)
    gq, gk, gv, gr, aq, ak, av, gate, iq, ik, misc, ikcat, kb, vaug = _proj_call(
        x2, w["attn_pre_norm"], w["w_main"], w["w_hp_hi"], w["w_hp_lo"], tm)
    per_seq = lambda a: a.reshape(bsz, seq, a.shape[-1])

    gla_o, gla_s = _gla_call(per_seq(gq), per_seq(gk), per_seq(gv), per_seq(gr), per_seq(misc),
                             w["wup"], w["b_gla_gate"], w["gla_head_norm"], gla_s0, gla_tb)

    n_keys = past + seq
    key_span = KEY_UNROLL * KEY_CHUNK
    lk = -(-n_keys // key_span) * key_span
    packed = [per_seq(a) for a in (ikcat, kb, vaug)]
    if past:
        flat = lambda a: a.reshape(bsz * past, -1).astype(F32)
        packed_past = _pack_call(flat(kidx_past), flat(k_past), flat(v_past), tm)
        packed = [jnp.concatenate([p.reshape(bsz, past, -1), n], axis=1) for p, n in zip(packed_past, packed)]
    if lk > n_keys:
        packed = [jnp.pad(a, ((0, 0), (0, lk - n_keys), (0, 0))) for a in packed]

    dsa_o = _dsa_call(per_seq(iq), per_seq(misc), per_seq(aq), *packed,
                      q_start=past, n_keys=n_keys, tq=dsa_tq)

    x1 = _merge_call(x2, gla_o.reshape(t, -1), dsa_o.reshape(t, -1), gate,
                     w["w_branch_gla"], w["w_branch_dsa"], w["w_out"], w["attn_post_norm"], tm)
    y, conv_new = _ffn_call(x1, w["ffn_pre_norm"], w["w_up"], w["conv_w"], w["conv_b"], conv_prev,
                            w["w_down_half"], w["ffn_post_norm"], tm=ffn_tm, seq=seq, tn=256)
    k_new = ak.reshape(bsz, seq, DSA_KV_HEADS, DSA_HEAD_DIM)
    v_new = av.reshape(bsz, seq, DSA_KV_HEADS, DSA_HEAD_DIM)
    return y.reshape(bsz, seq, d), k_new, v_new, per_seq(ik), gla_s, conv_new


def kernel(x_prompt, x_sample, cache_k, cache_v, cache_k_idx, state_gla, state_ffn_conv, attn_pre_norm, w_in, w_gla_gate_up, b_gla_gate, gla_head_norm, w_branch_gla, w_branch_dsa, w_out, attn_post_norm, ffn_pre_norm, w_up, conv_w, conv_b, w_down, ffn_post_norm):
    depth = w_in.shape[0]
    bsz, _, d = x_prompt.shape
    dt = x_prompt.dtype
    y_prompt, y_sample = x_prompt, x_sample
    prompt_new, sample_new = [], []
    row = lambda a: a.reshape(1, -1)
    for l in range(depth):
        w_main, w_hp_hi, w_hp_lo, wup = _prep_weights(w_in[l], w_gla_gate_up[l], d)
        w = dict(attn_pre_norm=row(attn_pre_norm[l]), w_main=w_main, w_hp_hi=w_hp_hi, w_hp_lo=w_hp_lo, wup=wup,
                 b_gla_gate=row(b_gla_gate[l]), gla_head_norm=row(gla_head_norm[l]),
                 w_branch_gla=w_branch_gla[l].astype(BF16), w_branch_dsa=w_branch_dsa[l].astype(BF16),
                 w_out=w_out[l].astype(BF16), attn_post_norm=row(attn_post_norm[l]),
                 ffn_pre_norm=row(ffn_pre_norm[l]), w_up=w_up[l].astype(BF16), conv_w=conv_w[l],
                 conv_b=row(conv_b[l]), w_down_half=(0.5 * w_down[l]).astype(BF16),
                 ffn_post_norm=row(ffn_post_norm[l]))
        ffn2 = conv_w.shape[-1]
        y_prompt, *new = _run_layer(
            y_prompt,
            jnp.zeros((bsz, 0, DSA_KV_HEADS * DSA_HEAD_DIM), dt),
            jnp.zeros((bsz, 0, DSA_KV_HEADS * DSA_HEAD_DIM), dt),
            jnp.zeros((bsz, 0, IDX_DIM), dt),
            jnp.zeros((bsz, GLA_HEADS, GLA_DK, GLA_DV), F32),
            jnp.zeros((bsz, CONV_W - 1, ffn2), dt),
            w, tm=256, gla_tb=512, dsa_tq=256, ffn_tm=1024)
        prompt_new.append(new)
        y_sample, *new = _run_layer(
            y_sample, cache_k[l], cache_v[l], cache_k_idx[l], state_gla[l], state_ffn_conv[l],
            w, tm=256, gla_tb=x_sample.shape[1], dsa_tq=x_sample.shape[1], ffn_tm=x_sample.shape[1])
        sample_new.append(new)
    k_p, v_p, kidx_p, gla_p, conv_p = (jnp.stack(t, axis=0) for t in zip(*prompt_new))
    k_s, v_s, kidx_s, gla_s, conv_s = (jnp.stack(t, axis=0) for t in zip(*sample_new))
    return (y_prompt, y_sample, k_p, v_p, kidx_p, gla_p, conv_p, k_s, v_s, kidx_s, gla_s, conv_s)
```

```python
import functools

import numpy as np
import jax
import jax.numpy as jnp
from jax import lax
from jax.experimental import pallas as pl
from jax.experimental.pallas import tpu as pltpu

F32 = jnp.float32
BF16 = jnp.bfloat16
I32 = jnp.int32
I16 = jnp.int16
HIGHEST = lax.Precision.HIGHEST

CHUNK = 64
GLA_HEADS = 4
GLA_DK = 128
GLA_DV = 256
GLA_RANK = 16
GLA_GATE_TAU = 16.0
DSA_HEADS = 8
DSA_KV_HEADS = 2
DSA_HEAD_DIM = 64
IDX_HEADS = 4
IDX_DIM = 64
IDX_SCALE = (IDX_HEADS * IDX_DIM) ** -0.5
TOPK_MAX = 256
CONV_W = 3
EPS = 1e-6

LANES = 128
SUBLANES = 8
VMEM_LIMIT = 56 * 1024 * 1024

MISC_GLOW = 0
MISC_IW = 16

KEY_CHUNK = 256
KEY_UNROLL = 2
GLA_UNROLL = 8
INT16_MIN = -(2 ** 15)
MASKED = -1e30
NT_DIMS = (((1,), (1,)), ((), ()))
TN_DIMS = (((0,), (0,)), ((), ()))


def _dot(a, b, **kw):
    return jnp.dot(a, b, preferred_element_type=F32, **kw)


def _dot_nt(a, b):
    return lax.dot_general(a, b, NT_DIMS, preferred_element_type=F32)


def _split_bf16(x):
    hi = x.astype(BF16)
    lo = (x - hi.astype(F32)).astype(BF16)
    return hi, lo


def _rmsnorm(x, g):
    return x * lax.rsqrt(jnp.mean(x * x, axis=-1, keepdims=True) + EPS) * g


def _params(*sem):
    return pltpu.CompilerParams(dimension_semantics=sem, vmem_limit_bytes=VMEM_LIMIT)


MAIN_SIZES = (GLA_HEADS * GLA_DK, GLA_HEADS * GLA_DK, GLA_HEADS * GLA_DV, GLA_HEADS * GLA_DV,
              DSA_HEADS * DSA_HEAD_DIM, DSA_KV_HEADS * DSA_HEAD_DIM, DSA_KV_HEADS * DSA_HEAD_DIM)
HP_WIDTH = 512
PACK_ROWS = 1024
PACK_WIDTHS = (3 * IDX_DIM, DSA_KV_HEADS * DSA_HEAD_DIM, 2 * DSA_KV_HEADS * DSA_HEAD_DIM)


def _pack_keys(ik, k, v):
    hi, lo = _split_bf16(ik)
    ikcat = jnp.concatenate([hi, hi, lo], axis=1)
    vb = v.astype(BF16)
    ones = jnp.ones((v.shape[0], DSA_HEAD_DIM), BF16)
    parts = []
    for g in range(DSA_KV_HEADS):
        parts += [vb[:, g * DSA_HEAD_DIM:(g + 1) * DSA_HEAD_DIM], ones]
    return ikcat, k.astype(BF16), jnp.concatenate(parts, axis=1)


def _store_heads(o_ref, x):
    dim = o_ref.shape[-1]
    for g in range(o_ref.shape[1]):
        o_ref[:, g, :] = x[:, g * dim:(g + 1) * dim].astype(o_ref.dtype)


def _load_heads(x_ref):
    return jnp.concatenate([x_ref[:, g, :] for g in range(x_ref.shape[1])], axis=1)


def _proj_kernel(x_ref, g_ref, wm_ref, whi_ref, wlo_ref,
                 gq_ref, gk_ref, gv_ref, gr_ref, aq_ref, ak_ref, av_ref, gate_ref, iq_ref, ik_ref, misc_ref,
                 ikcat_ref, kb_ref, vaug_ref):
    xn = _rmsnorm(x_ref[...], g_ref[...])
    xh, xl = _split_bf16(xn)
    outs = (gq_ref, gk_ref, gv_ref, gr_ref, aq_ref, ak_ref, av_ref, gate_ref)
    off = 0
    vals = []
    for o_ref, n in zip(outs, MAIN_SIZES + (gate_ref.shape[-1],)):
        vals.append(_dot(xh, wm_ref[:, off:off + n]))
        if o_ref.ndim == 2:
            o_ref[...] = vals[-1].astype(o_ref.dtype)
        else:
            _store_heads(o_ref, vals[-1])
        off += n
    ak, av = vals[5], vals[6]
    whi = whi_ref[...]
    hp = _dot(xh, whi) + _dot(xl, whi) + _dot(xh, wlo_ref[...])
    iq_ref[...] = hp[:, 0:256]
    ik_ref[...] = hp[:, 256:320]
    misc_ref[...] = hp[:, 384:512]
    ikcat_ref[...], kb_ref[...], vaug_ref[...] = _pack_keys(hp[:, 256:320], ak, av)


def _proj_call(x, g, wm, whi, wlo, tm):
    t, d = x.shape
    gate_w = wm.shape[1] - sum(MAIN_SIZES)
    widths = MAIN_SIZES + (gate_w,)
    dtypes = (BF16, BF16, BF16, BF16, BF16, F32, F32, BF16)
    out_shape = [jax.ShapeDtypeStruct((t, n), dt) for n, dt in zip(widths, dtypes)]
    for n in (5, 6):
        out_shape[n] = jax.ShapeDtypeStruct((t, DSA_KV_HEADS, DSA_HEAD_DIM), F32)
    out_shape += [jax.ShapeDtypeStruct((t, n), F32) for n in (256, 64, 128)]
    out_shape += [jax.ShapeDtypeStruct((t, n), BF16) for n in PACK_WIDTHS]
    row = lambda n: pl.BlockSpec((tm, n), lambda i: (i, 0))
    rows = lambda s: pl.BlockSpec((tm,) + s.shape[1:], lambda i: (i,) + (0,) * (len(s.shape) - 1))
    full = lambda a: pl.BlockSpec(a.shape, lambda i: (0, 0))
    return pl.pallas_call(
        _proj_kernel,
        grid=(t // tm,),
        in_specs=[row(d), full(g), full(wm), full(whi), full(wlo)],
        out_specs=[rows(s) for s in out_shape],
        out_shape=out_shape,
        compiler_params=_params("parallel"),
        name="proj",
    )(x, g, wm, whi, wlo)


def _pack_kernel(ik_ref, k_ref, v_ref, ikcat_ref, kb_ref, vaug_ref):
    ikcat_ref[...], kb_ref[...], vaug_ref[...] = _pack_keys(ik_ref[...], _load_heads(k_ref), _load_heads(v_ref))


def _pack_call(ik, k, v, tm):
    t = ik.shape[0]
    row = lambda n: pl.BlockSpec((tm, n), lambda i: (i, 0))
    heads = pl.BlockSpec((tm,) + k.shape[1:], lambda i: (i, 0, 0))
    return pl.pallas_call(
        _pack_kernel,
        grid=(t // tm,),
        in_specs=[row(ik.shape[1]), heads, heads],
        out_specs=[row(n) for n in PACK_WIDTHS],
        out_shape=[jax.ShapeDtypeStruct((t, n), BF16) for n in PACK_WIDTHS],
        compiler_params=_params("parallel"),
        name="pack",
    )(ik, k, v)


def _gla_kernel(gq_ref, gk_ref, gv_ref, gr_ref, misc_ref, wup_ref, bup_ref, hn_ref, s0_ref,
                o_ref, sfin_ref, st_ref, qd_ref, ki_ref, kr_ref, dl_ref, *, nchunks):
    j = pl.program_id(1)

    @pl.when(j == 0)
    def _():
        for h in range(GLA_HEADS):
            st_ref[h] = s0_ref[h].T

    row = lax.broadcasted_iota(I32, (CHUNK, CHUNK), 0)
    col = lax.broadcasted_iota(I32, (CHUNK, CHUNK), 1)
    causal = row >= col
    tri = causal.astype(BF16)

    m_hi, m_lo = _split_bf16(misc_ref[...])
    z = _dot(jnp.concatenate([m_hi, m_lo, m_hi], axis=1), wup_ref[...]) + bup_ref[...]
    log_a = (jnp.minimum(z, 0.0) - jnp.log(1.0 + jnp.exp(-jnp.abs(z)))) * (1.0 / GLA_GATE_TAU)
    la_hi, la_lo = _split_bf16(log_a)
    tri2 = jnp.concatenate([tri, tri], axis=1)
    for c in range(nchunks):
        rows = slice(c * CHUNK, (c + 1) * CHUNK)
        b = _dot(tri2, jnp.concatenate([la_hi[rows, :], la_lo[rows, :]], axis=0))
        b_last = b[CHUNK - 1:CHUNK, :]
        q = gq_ref[rows, :].astype(F32)
        k = gk_ref[rows, :].astype(F32)
        qd_ref[rows, :] = (q * (GLA_DK ** -0.5) * jnp.exp(b)).astype(BF16)
        ki_ref[rows, :] = (k * jnp.exp(-b)).astype(BF16)
        kr_ref[rows, :] = (k * jnp.exp(b_last - b)).astype(BF16)
        dl_ref[c] = jnp.broadcast_to(jnp.exp(b_last), dl_ref.shape[1:])

    def chunk(c, carry):
        rows = pl.ds(pl.multiple_of(c * CHUNK, CHUNK), CHUNK)
        q_dec, k_inv, k_rem = qd_ref[rows, :], ki_ref[rows, :], kr_ref[rows, :]
        d_last = dl_ref[c][0:1, :]
        v = gv_ref[rows, :]
        gr = gr_ref[rows, :].astype(F32)
        hn = hn_ref[...]
        for h in range(GLA_HEADS):
            ks = slice(h * GLA_DK, (h + 1) * GLA_DK)
            vs = slice(h * GLA_DV, (h + 1) * GLA_DV)
            qh, vh = q_dec[:, ks], v[:, vs]
            scores = jnp.where(causal, _dot_nt(qh, k_inv[:, ks]), 0.0)
            st = st_ref[h]
            o = _dot(scores.astype(BF16), vh) + _dot_nt(qh, st.astype(BF16))
            ut = lax.dot_general(vh, k_rem[:, ks], TN_DIMS, preferred_element_type=F32)
            st_ref[h] = st * d_last[:, ks] + ut
            y = _rmsnorm(o, hn)
            g = gr[:, vs]
            o_ref[rows, vs] = (y * (g * jax.nn.sigmoid(g))).astype(o_ref.dtype)
        return carry

    lax.fori_loop(0, nchunks, chunk, 0, unroll=min(nchunks, GLA_UNROLL))

    @pl.when(j == pl.num_programs(1) - 1)
    def _():
        for h in range(GLA_HEADS):
            sfin_ref[h] = st_ref[h].T


def _gla_call(gq, gk, gv, gr, misc, wup, bup, hn, s0, tb):
    bsz, seq, _ = gq.shape
    blk = lambda n: pl.BlockSpec((None, tb, n), lambda b, j: (b, j, 0))
    full2 = lambda a: pl.BlockSpec(a.shape, lambda b, j: (0, 0))
    st_spec = pl.BlockSpec((None, GLA_HEADS, GLA_DK, GLA_DV), lambda b, j: (b, 0, 0, 0))
    return pl.pallas_call(
        functools.partial(_gla_kernel, nchunks=tb // CHUNK),
        grid=(bsz, seq // tb),
        in_specs=[blk(gq.shape[2]), blk(gk.shape[2]), blk(gv.shape[2]), blk(gr.shape[2]), blk(misc.shape[2]),
                  full2(wup), full2(bup), full2(hn), st_spec],
        out_specs=[blk(gv.shape[2]), st_spec],
        out_shape=[jax.ShapeDtypeStruct((bsz, seq, gv.shape[2]), BF16),
                   jax.ShapeDtypeStruct(s0.shape, F32)],
        scratch_shapes=[pltpu.VMEM((GLA_HEADS, GLA_DV, GLA_DK), F32),
                        pltpu.VMEM((tb, gq.shape[2]), BF16), pltpu.VMEM((tb, gq.shape[2]), BF16),
                        pltpu.VMEM((tb, gq.shape[2]), BF16),
                        pltpu.VMEM((tb // CHUNK, SUBLANES, gq.shape[2]), F32)],
        compiler_params=_params("parallel", "arbitrary"),
        name="gla",
    )(gq, gk, gv, gr, misc, wup, bup, hn, s0)


def _dsa_kernel(iq_ref, misc_ref, aq_ref, ikcat_ref, kb_ref, vaug_ref, o_ref,
                keys_ref, hi_ref, lo_ref, m_ref, acc_ref,
                *, tq, tqp, q_start, n_keys, topk, single_block):
    qi = 0 if single_block else pl.program_id(1)
    kc = KEY_CHUNK
    hd = DSA_HEAD_DIM
    hpg = DSA_HEADS // DSA_KV_HEADS

    first_q = q_start + qi * tq
    last_q = first_q + tq - 1
    span = KEY_UNROLL * kc
    lowest = min if single_block else jnp.minimum
    nt = (lowest(n_keys, ((last_q >> 6) + 1) * CHUNK) + span - 1) // span
    nfull = lowest(n_keys, ((first_q >> 6) + 1) * CHUNK) // span

    def loop(lo, hi, trip, carry):
        if not single_block:
            return lax.fori_loop(lo, hi, trip, carry)
        for t in range(lo, hi):
            carry = trip(t, carry)
        return carry

    def trips(body, masked):
        def trip(t, carry):
            for u in range(KEY_UNROLL):
                carry = body(pl.multiple_of((t * KEY_UNROLL + u) * kc, kc), masked, carry)
            return carry
        return trip

    def for_chunks(body, init):
        return loop(nfull, nt, trips(body, True), loop(0, nfull, trips(body, False), init))

    def for_all_chunks(body, init):
        return loop(0, nt, trips(lambda r0, _, carry: body(r0, carry), None), init)

    def pad_rows(x):
        if tqp == tq:
            return x
        return jnp.concatenate([x, jnp.zeros((tqp - tq, x.shape[1]), x.dtype)], axis=0)

    iq_hi, iq_lo = _split_bf16(pad_rows(iq_ref[...]))
    iq_cat = []
    for h in range(IDX_HEADS):
        s = slice(h * IDX_DIM, (h + 1) * IDX_DIM)
        iq_cat.append(jnp.concatenate([iq_hi[:, s], iq_lo[:, s], iq_hi[:, s]], axis=1))
    wts = pad_rows(misc_ref[...]).T[MISC_IW:MISC_IW + IDX_HEADS, :] * IDX_SCALE
    q_chunk = (first_q + lax.broadcasted_iota(I32, (1, tqp), 1)) >> 6

    def admissible(r0):
        k_pos = r0 + lax.broadcasted_iota(I32, (kc, tqp), 0)
        return ((k_pos >> 6) <= q_chunk) & (k_pos < n_keys)

    def score_body(r0, masked, carry):
        rows = pl.ds(r0, kc)
        ik = ikcat_ref[rows, :]
        acc = jnp.zeros((kc, tqp), F32)
        for h in range(IDX_HEADS):
            acc = acc + wts[h:h + 1, :] * jnp.maximum(_dot_nt(ik, iq_cat[h]), 0.0)
        if masked:
            acc = jnp.where(admissible(r0), acc, -jnp.inf)
        bits = lax.bitcast_convert_type(acc, I32)
        key = jnp.where(bits < 0, bits ^ jnp.int32(0x7FFFFFFF), bits)
        keys_ref[rows, :] = key
        hi_ref[rows, :] = (key >> 16).astype(I16)
        lo_ref[rows, :] = ((key & 0xFFFF) + INT16_MIN).astype(I16)
        return carry

    for_chunks(score_body, 0)

    grp = 32
    one, zero = jnp.asarray(1, BF16), jnp.asarray(0, BF16)

    def count16(ref, pred):
        def body(r0, acc):
            m = jnp.where(pred(ref[pl.ds(r0, kc), :]), one, zero)
            for i in range(kc // grp):
                acc = acc + m[i * grp:(i + 1) * grp, :]
            return acc
        acc = for_all_chunks(body, jnp.zeros((grp, tqp), BF16))
        return jnp.sum(acc.astype(F32), axis=0, keepdims=True)

    def search16(ref, kth):
        def bit_body(i, carry):
            prefix, above = carry
            cand = prefix | (jnp.int32(1) << (15 - i))
            cand16 = (cand + INT16_MIN).astype(I16)
            tot = count16(ref, lambda ch: ch >= cand16)
            keep = tot >= kth
            return jnp.where(keep, cand, prefix), jnp.where(keep, above, tot)
        return lax.fori_loop(0, 16, bit_body, (jnp.zeros((1, tqp), I32), jnp.zeros((1, tqp), F32)))

    kth = jnp.full((1, tqp), topk, F32)
    a_u, above_a = search16(hi_ref, kth)
    a16 = (a_u + INT16_MIN).astype(I16)
    kth_lo = kth - above_a

    def bucket_body(r0, carry):
        rows = pl.ds(r0, kc)
        lo_ref[rows, :] = jnp.where(hi_ref[rows, :] == a16, lo_ref[rows, :], jnp.asarray(INT16_MIN, I16))
        return carry

    for_all_chunks(bucket_body, 0)
    b_u, above_b = search16(lo_ref, kth_lo)
    thr = ((a_u + INT16_MIN) << 16) | b_u
    need = kth_lo - above_b

    tri = (lax.broadcasted_iota(I32, (kc, kc), 0) >= lax.broadcasted_iota(I32, (kc, kc), 1)).astype(BF16)

    def sel_bias(r0, masked, running):
        ch = keys_ref[pl.ds(r0, kc), :]
        eq = ch == thr
        rank = _dot(tri, jnp.where(eq, 1.0, 0.0).astype(BF16)) + running
        sel = (ch > thr) | (eq & (rank <= need))
        if masked:
            sel = sel & admissible(r0)
        return jnp.where(sel, 0.0, MASKED).T[:tq, :], rank[kc - 1:kc, :]

    m_ref[...] = jnp.full(m_ref.shape, MASKED, F32)
    acc_ref[...] = jnp.zeros(acc_ref.shape, F32)
    q_groups = []
    for g in range(DSA_KV_HEADS):
        qg = jnp.concatenate([aq_ref[:, h * hd:(h + 1) * hd] for h in range(g * hpg, (g + 1) * hpg)], axis=0)
        q_groups.append(qg * jnp.asarray(hd ** -0.5, BF16))

    def attend(t, masked, running):
        r0 = pl.multiple_of(t * span, span)
        cols = pl.ds(r0, span)
        biases = []
        for u in range(KEY_UNROLL):
            bias, running = sel_bias(pl.multiple_of(r0 + u * kc, kc), masked, running)
            biases.append(bias)
        bias = jnp.concatenate(biases, axis=1)
        for g in range(DSA_KV_HEADS):
            s = _dot_nt(q_groups[g], kb_ref[cols, g * hd:(g + 1) * hd])
            s = jnp.concatenate([s[r * tq:(r + 1) * tq, :] + bias for r in range(hpg)], axis=0)
            tiles = [s[:, i * LANES:(i + 1) * LANES] for i in range(span // LANES)]
            m_old = m_ref[g]
            m_new = jnp.maximum(m_old, jnp.max(functools.reduce(jnp.maximum, tiles), axis=1, keepdims=True))
            p = jnp.concatenate([jnp.exp(t - m_new) for t in tiles], axis=1)
            pv = _dot(p.astype(BF16), vaug_ref[cols, 2 * g * hd:2 * (g + 1) * hd])
            acc_ref[g] = acc_ref[g] * jnp.exp(m_old - m_new) + pv
            m_ref[g] = m_new
        return running

    no_ties = jnp.zeros((1, tqp), F32)
    loop(nfull, nt, lambda t, c: attend(t, True, c), loop(0, nfull, lambda t, c: attend(t, False, c), no_ties))

    for g in range(DSA_KV_HEADS):
        acc = acc_ref[g]
        out = acc[:, :hd] / acc[:, hd:hd + 1]
        for r in range(hpg):
            h = g * hpg + r
            o_ref[:, h * hd:(h + 1) * hd] = out[r * tq:(r + 1) * tq, :].astype(o_ref.dtype)


def _dsa_call(iq, misc, aq, ikcat, kb, vaug, *, q_start, n_keys, tq):
    bsz, lq, _ = iq.shape
    lk = ikcat.shape[1]
    assert lk % (KEY_UNROLL * KEY_CHUNK) == 0 and lq % tq == 0 and lk // 32 <= 256
    tqp = max(tq, LANES)
    topk = min(TOPK_MAX, n_keys // 4)
    rows_g = (DSA_HEADS // DSA_KV_HEADS) * tq
    qblk = lambda n: pl.BlockSpec((None, tq, n), lambda b, q: (b, q, 0))
    kblk = lambda n: pl.BlockSpec((None, lk, n), lambda b, q: (b, 0, 0))
    kern = functools.partial(_dsa_kernel, tq=tq, tqp=tqp, q_start=q_start, n_keys=n_keys, topk=topk,
                             single_block=(lq == tq))
    return pl.pallas_call(
        kern,
        grid=(bsz, lq // tq),
        in_specs=[qblk(iq.shape[2]), qblk(misc.shape[2]), qblk(aq.shape[2]),
                  kblk(ikcat.shape[2]), kblk(kb.shape[2]), kblk(vaug.shape[2])],
        out_specs=qblk(aq.shape[2]),
        out_shape=jax.ShapeDtypeStruct(aq.shape, BF16),
        scratch_shapes=[pltpu.VMEM((lk, tqp), I32),
                        pltpu.VMEM((lk, tqp), I16),
                        pltpu.VMEM((lk, tqp), I16),
                        pltpu.VMEM((DSA_KV_HEADS, rows_g, LANES), F32),
                        pltpu.VMEM((DSA_KV_HEADS, rows_g, 2 * DSA_HEAD_DIM), F32)],
        compiler_params=_params("parallel", "arbitrary"),
        name="dsa",
    )(iq, misc, aq, ikcat, kb, vaug)


def _merge_kernel(x_ref, go_ref, do_ref, gate_ref, wa_ref, wb_ref, wo_ref, g_ref, y_ref):
    d = x_ref.shape[-1]
    a = _dot(go_ref[...], wa_ref[...])
    b = _dot(do_ref[...], wb_ref[...])
    gate = gate_ref[...].astype(F32)
    mixed = jax.nn.sigmoid(gate[:, :d]) * a + jax.nn.sigmoid(gate[:, d:]) * b
    y_ref[...] = x_ref[...] + _rmsnorm(_dot(mixed.astype(BF16), wo_ref[...]), g_ref[...])


def _merge_call(x, go, do, gate, wa, wb, wo, g, tm):
    t, d = x.shape
    row = lambda n: pl.BlockSpec((tm, n), lambda i: (i, 0))
    full = lambda a: pl.BlockSpec(a.shape, lambda i: (0, 0))
    return pl.pallas_call(
        _merge_kernel,
        grid=(t // tm,),
        in_specs=[row(d), row(go.shape[1]), row(do.shape[1]), row(gate.shape[1]),
                  full(wa), full(wb), full(wo), full(g)],
        out_specs=row(d),
        out_shape=jax.ShapeDtypeStruct((t, d), F32),
        compiler_params=_params("parallel"),
        name="merge",
    )(x, go, do, gate, wa, wb, wo, g)


FFN_ROW_CHUNK = 256
FFN_SLAB_ROWS = 64
GELU_C0 = float(np.sqrt(2.0 / np.pi))
GELU_C1 = 0.044715 * GELU_C0


def _ffn_kernel(x_ref, gpre_ref, wu_ref, cw_ref, cb_ref, pv_ref, wdh_ref, gpost_ref,
                y_ref, cn_ref, xn_ref, acc_ref, car_ref, act_ref, u_ref, *, bps, tn):
    i = pl.program_id(0)
    tm = x_ref.shape[0]
    f = wdh_ref.shape[0]
    xn_ref[...] = _rmsnorm(x_ref[...], gpre_ref[...]).astype(BF16)
    acc_ref[...] = jnp.zeros_like(acc_ref)
    first = (i % bps) == 0
    top = lax.broadcasted_iota(I32, (SUBLANES, tn), 0)
    rc = min(tm, FFN_ROW_CHUNK)

    def up(cols, r):
        return _dot(xn_ref[r * rc:(r + 1) * rc, :], wu_ref[:, cols])

    def conv_slab(cols, u, halo):
        r1, r2 = pltpu.roll(u, 1, 0), pltpu.roll(u, 2, 0)
        top1 = jnp.where(top == 0, halo[1:2, :], r1[:SUBLANES, :])
        top2 = jnp.where(top == 0, halo[0:1, :], jnp.where(top == 1, halo[1:2, :], r2[:SUBLANES, :]))
        u1 = jnp.concatenate([top1, r1[SUBLANES:, :]], axis=0)
        u2 = jnp.concatenate([top2, r2[SUBLANES:, :]], axis=0)
        cw = cw_ref[:, cols]
        return cb_ref[:, cols] + (cw[0:1, :] * u2 + cw[1:2, :] * u1 + cw[2:3, :] * u), u[u.shape[0] - 2:, :]

    def keep_tail(cols, slot, half):
        car_ref[:, cols] = u_ref[slot, half, rc - SUBLANES:rc, :]
        cn_ref[:, cols] = u_ref[slot, half, rc - (CONV_W - 1):rc, :]

    def block_halo(cols):
        return jnp.where(first, pv_ref[:, cols], car_ref[SUBLANES - 2:SUBLANES, cols])

    nft = f // tn
    nrc = tm // rc

    def tile(j, aslot, with_down):
        cols_a = pl.ds(pl.multiple_of(j * tn, tn), tn)
        cols_b = pl.ds(pl.multiple_of(f + j * tn, tn), tn)
        halo_a, halo_b = block_halo(cols_a), block_halo(cols_b)

        def up_to(slot, r):
            u_ref[slot, 0] = up(cols_a, r)
            u_ref[slot, 1] = up(cols_b, r)

        up_to(0, 0)
        for r in range(nrc):
            rows = slice(r * rc, (r + 1) * rc)
            slot = r % 2
            if r + 1 < nrc:
                up_to(1 - slot, r + 1)
            if with_down:
                project_down(j - 1, 1 - aslot, rows)
            if r + 1 == nrc:
                keep_tail(cols_a, slot, 0)
                keep_tail(cols_b, slot, 1)
            sl = min(rc, FFN_SLAB_ROWS)
            for s in range(0, rc, sl):
                a, halo_a = conv_slab(cols_a, u_ref[slot, 0, s:s + sl, :], halo_a)
                b, halo_b = conv_slab(cols_b, u_ref[slot, 1, s:s + sl, :], halo_b)
                act = (a * (1.0 + jnp.tanh(a * (GELU_C0 + GELU_C1 * (a * a))))) * b
                act_ref[aslot, r * rc + s:r * rc + s + sl, :] = act.astype(BF16)

    def project_down(j, aslot, rows):
        acc_ref[rows, :] += _dot(act_ref[aslot, rows, :], wdh_ref[pl.ds(pl.multiple_of(j * tn, tn), tn), :])

    assert nft % 2 == 1
    tile(0, 0, False)

    def trip(p, carry):
        tile(2 * p + 1, 1, True)
        tile(2 * p + 2, 0, True)
        return carry

    lax.fori_loop(0, nft // 2, trip, 0)
    for r in range(nrc):
        project_down(nft - 1, 0, slice(r * rc, (r + 1) * rc))
    y_ref[...] = x_ref[...] + _rmsnorm(acc_ref[...], gpost_ref[...])


def _ffn_call(x, gpre, w_up, conv_w, conv_b, conv_prev, w_down_half, gpost, *, tm, seq, tn):
    t, d = x.shape
    f = w_down_half.shape[0]
    nseq = t // seq
    bps = seq // tm
    assert f % tn == 0 and seq % tm == 0
    xrow = pl.BlockSpec((tm, d), lambda i: (i, 0))
    const = lambda a: pl.BlockSpec(a.shape, lambda i: (0, 0), pipeline_mode=pl.Buffered(1))
    y, cn = pl.pallas_call(
        functools.partial(_ffn_kernel, bps=bps, tn=tn),
        grid=(t // tm,),
        in_specs=[xrow, const(gpre), const(w_up), const(conv_w), const(conv_b),
                  pl.BlockSpec((None, CONV_W - 1, 2 * f), lambda i: (i // bps, 0, 0)),
                  const(w_down_half), const(gpost)],
        out_specs=[xrow, pl.BlockSpec((None, CONV_W - 1, 2 * f), lambda i: (i, 0, 0))],
        out_shape=[jax.ShapeDtypeStruct((t, d), F32),
                   jax.ShapeDtypeStruct((t // tm, CONV_W - 1, 2 * f), F32)],
        scratch_shapes=[pltpu.VMEM((tm, d), BF16), pltpu.VMEM((tm, d), F32),
                        pltpu.VMEM((SUBLANES, 2 * f), F32), pltpu.VMEM((2, tm, tn), BF16),
                        pltpu.VMEM((2, 2, min(tm, FFN_ROW_CHUNK), tn), F32)],
        compiler_params=_params("arbitrary"),
        name="ffn",
    )(x, gpre, w_up, conv_w, conv_b, conv_prev, w_down_half, gpost)
    return y, cn.reshape(nseq, bps, CONV_W - 1, 2 * f)[:, bps - 1]


def _prep_weights(w_in, w_gate_up, d_model):
    sizes = (GLA_HEADS * GLA_DK, GLA_HEADS * GLA_DK, GLA_HEADS * GLA_DV, GLA_RANK, GLA_HEADS * GLA_DV,
             DSA_HEADS * DSA_HEAD_DIM, DSA_KV_HEADS * DSA_HEAD_DIM, DSA_KV_HEADS * DSA_HEAD_DIM,
             IDX_HEADS * IDX_DIM, IDX_DIM, IDX_HEADS, 2 * d_model)
    offs = np.concatenate([[0], np.cumsum(sizes)])
    g_q, g_k, g_v, g_low, g_r, a_q, a_k, a_v, i_q, i_k, i_w, gate = (
        w_in[:, int(offs[n]):int(offs[n + 1])] for n in range(len(sizes)))
    w_main = jnp.concatenate([g_q, g_k, g_v, g_r, a_q, a_k, a_v, gate], axis=1).astype(BF16)
    zeros = lambda n: jnp.zeros((w_in.shape[0], n), w_in.dtype)
    misc = jnp.concatenate([g_low, i_w, zeros(LANES - GLA_RANK - IDX_HEADS)], axis=1)
    assert MISC_GLOW == 0 and MISC_IW == GLA_RANK
    w_hp = jnp.concatenate([i_q, i_k, zeros(LANES - IDX_DIM), misc], axis=1)
    assert w_hp.shape[1] == HP_WIDTH
    w_hp_hi, w_hp_lo = _split_bf16(w_hp)
    wup = jnp.concatenate([w_gate_up, jnp.zeros((LANES - GLA_RANK, w_gate_up.shape[1]), w_gate_up.dtype)], axis=0)
    wup_hi, wup_lo = _split_bf16(wup)
    return w_main, w_hp_hi, w_hp_lo, jnp.concatenate([wup_hi, wup_hi, wup_lo], axis=0)


def _run_layer(x, k_past, v_past, kidx_past, gla_s0, conv_prev, w, *, tm, gla_tb, dsa_tq, ffn_tm):
    bsz, seq, d = x.shape
    past = k_past.shape[1]
    t = bsz * seq
    x2 = x.reshape(t, d)
    gq, gk, gv, gr, aq, ak, av, gate, iq, ik, misc, ikcat, kb, vaug = _proj_call(
        x2, w["attn_pre_norm"], w["w_main"], w["w_hp_hi"], w["w_hp_lo"], tm)
    per_seq = lambda a: a.reshape(bsz, seq, a.shape[-1])

    gla_o, gla_s = _gla_call(per_seq(gq), per_seq(gk), per_seq(gv), per_seq(gr), per_seq(misc),
                             w["wup"], w["b_gla_gate"], w["gla_head_norm"], gla_s0, gla_tb)

    n_keys = past + seq
    key_span = KEY_UNROLL * KEY_CHUNK
    lk = -(-n_keys // key_span) * key_span
    packed = [per_seq(a) for a in (ikcat, kb, vaug)]
    if past:
        flat = lambda a: a.reshape((bsz * past,) + a.shape[2:]).astype(F32)
        packed_past = _pack_call(flat(kidx_past), flat(k_past), flat(v_past), min(bsz * past, PACK_ROWS))
        packed = [jnp.concatenate([p.reshape(bsz, past, -1), n], axis=1) for p, n in zip(packed_past, packed)]
    if lk > n_keys:
        packed = [jnp.pad(a, ((0, 0), (0, lk - n_keys), (0, 0))) for a in packed]

    dsa_o = _dsa_call(per_seq(iq), per_seq(misc), per_seq(aq), *packed,
                      q_start=past, n_keys=n_keys, tq=dsa_tq)

    x1 = _merge_call(x2, gla_o.reshape(t, -1), dsa_o.reshape(t, -1), gate,
                     w["w_branch_gla"], w["w_branch_dsa"], w["w_out"], w["attn_post_norm"], tm)
    y, conv_new = _ffn_call(x1, w["ffn_pre_norm"], w["w_up"], w["conv_w"], w["conv_b"], conv_prev,
                            w["w_down_half"], w["ffn_post_norm"], tm=ffn_tm, seq=seq, tn=256)
    k_new = ak.reshape(bsz, seq, DSA_KV_HEADS, DSA_HEAD_DIM)
    v_new = av.reshape(bsz, seq, DSA_KV_HEADS, DSA_HEAD_DIM)
    return y.reshape(bsz, seq, d), k_new, v_new, per_seq(ik), gla_s, conv_new


def kernel(x_prompt, x_sample, cache_k, cache_v, cache_k_idx, state_gla, state_ffn_conv, attn_pre_norm, w_in, w_gla_gate_up, b_gla_gate, gla_head_norm, w_branch_gla, w_branch_dsa, w_out, attn_post_norm, ffn_pre_norm, w_up, conv_w, conv_b, w_down, ffn_post_norm):
    depth = w_in.shape[0]
    bsz, _, d = x_prompt.shape
    dt = x_prompt.dtype
    y_prompt, y_sample = x_prompt, x_sample
    prompt_new, sample_new = [], []
    row = lambda a: a.reshape(1, -1)
    for l in range(depth):
        w_main, w_hp_hi, w_hp_lo, wup = _prep_weights(w_in[l], w_gla_gate_up[l], d)
        w = dict(attn_pre_norm=row(attn_pre_norm[l]), w_main=w_main, w_hp_hi=w_hp_hi, w_hp_lo=w_hp_lo, wup=wup,
                 b_gla_gate=row(b_gla_gate[l]), gla_head_norm=row(gla_head_norm[l]),
                 w_branch_gla=w_branch_gla[l].astype(BF16), w_branch_dsa=w_branch_dsa[l].astype(BF16),
                 w_out=w_out[l].astype(BF16), attn_post_norm=row(attn_post_norm[l]),
                 ffn_pre_norm=row(ffn_pre_norm[l]), w_up=w_up[l].astype(BF16), conv_w=conv_w[l],
                 conv_b=row(conv_b[l]), w_down_half=(0.5 * w_down[l]).astype(BF16),
                 ffn_post_norm=row(ffn_post_norm[l]))
        ffn2 = conv_w.shape[-1]
        y_prompt, *new = _run_layer(
            y_prompt,
            jnp.zeros((bsz, 0, DSA_KV_HEADS * DSA_HEAD_DIM), dt),
            jnp.zeros((bsz, 0, DSA_KV_HEADS * DSA_HEAD_DIM), dt),
            jnp.zeros((bsz, 0, IDX_DIM), dt),
            jnp.zeros((bsz, GLA_HEADS, GLA_DK, GLA_DV), F32),
            jnp.zeros((bsz, CONV_W - 1, ffn2), dt),
            w, tm=256, gla_tb=512, dsa_tq=256, ffn_tm=1024)
        prompt_new.append(new)
        y_sample, *new = _run_layer(
            y_sample, cache_k[l], cache_v[l], cache_k_idx[l], state_gla[l], state_ffn_conv[l],
            w, tm=256, gla_tb=x_sample.shape[1], dsa_tq=x_sample.shape[1], ffn_tm=x_sample.shape[1])
        sample_new.append(new)
    k_p, v_p, kidx_p, gla_p, conv_p = (jnp.stack(t, axis=0) for t in zip(*prompt_new))
    k_s, v_s, kidx_s, gla_s, conv_s = (jnp.stack(t, axis=0) for t in zip(*sample_new))
    return (y_prompt, y_sample, k_p, v_p, kidx_p, gla_p, conv_p, k_s, v_s, kidx_s, gla_s, conv_s)
```

```python
import functools

import numpy as np
import jax
import jax.numpy as jnp
from jax import lax
from jax.experimental import pallas as pl
from jax.experimental.pallas import tpu as pltpu

F32 = jnp.float32
BF16 = jnp.bfloat16
I32 = jnp.int32
I16 = jnp.int16
HIGHEST = lax.Precision.HIGHEST

CHUNK = 64
GLA_HEADS = 4
GLA_DK = 128
GLA_DV = 256
GLA_RANK = 16
GLA_GATE_TAU = 16.0
DSA_HEADS = 8
DSA_KV_HEADS = 2
DSA_HEAD_DIM = 64
IDX_HEADS = 4
IDX_DIM = 64
IDX_SCALE = (IDX_HEADS * IDX_DIM) ** -0.5
TOPK_MAX = 256
CONV_W = 3
EPS = 1e-6

LANES = 128
SUBLANES = 8
VMEM_LIMIT = 56 * 1024 * 1024

MISC_GLOW = 0
MISC_IW = 16

KEY_CHUNK = 256
KEY_UNROLL = 2
GLA_UNROLL = 8
INT16_MIN = -(2 ** 15)
MASKED = -1e30
NT_DIMS = (((1,), (1,)), ((), ()))
TN_DIMS = (((0,), (0,)), ((), ()))


def _dot(a, b, **kw):
    return jnp.dot(a, b, preferred_element_type=F32, **kw)


def _dot_nt(a, b):
    return lax.dot_general(a, b, NT_DIMS, preferred_element_type=F32)


def _split_bf16(x):
    hi = x.astype(BF16)
    lo = (x - hi.astype(F32)).astype(BF16)
    return hi, lo


def _rmsnorm(x, g):
    return x * lax.rsqrt(jnp.mean(x * x, axis=-1, keepdims=True) + EPS) * g


def _params(*sem):
    return pltpu.CompilerParams(dimension_semantics=sem, vmem_limit_bytes=VMEM_LIMIT)


MAIN_SIZES = (GLA_HEADS * GLA_DK, GLA_HEADS * GLA_DK, GLA_HEADS * GLA_DV, GLA_HEADS * GLA_DV,
              DSA_HEADS * DSA_HEAD_DIM, DSA_KV_HEADS * DSA_HEAD_DIM, DSA_KV_HEADS * DSA_HEAD_DIM)
HP_WIDTH = 512
PACK_ROWS = 1024
MERGE_ROWS = 512
PACK_WIDTHS = (3 * IDX_DIM, DSA_KV_HEADS * DSA_HEAD_DIM, 2 * DSA_KV_HEADS * DSA_HEAD_DIM)


def _pack_keys(ik, k, v):
    hi, lo = _split_bf16(ik)
    ikcat = jnp.concatenate([hi, hi, lo], axis=1)
    vb = v.astype(BF16)
    ones = jnp.ones((v.shape[0], DSA_HEAD_DIM), BF16)
    parts = []
    for g in range(DSA_KV_HEADS):
        parts += [vb[:, g * DSA_HEAD_DIM:(g + 1) * DSA_HEAD_DIM], ones]
    return ikcat, k.astype(BF16), jnp.concatenate(parts, axis=1)


def _store_heads(o_ref, x):
    dim = o_ref.shape[-1]
    for g in range(o_ref.shape[1]):
        o_ref[:, g, :] = x[:, g * dim:(g + 1) * dim].astype(o_ref.dtype)


def _load_heads(x_ref):
    return jnp.concatenate([x_ref[:, g, :] for g in range(x_ref.shape[1])], axis=1)


def _proj_kernel(x_ref, g_ref, wm_ref, whi_ref, wlo_ref,
                 gq_ref, gk_ref, gv_ref, gr_ref, aq_ref, ak_ref, av_ref, gate_ref, iq_ref, ik_ref, misc_ref,
                 ikcat_ref, kb_ref, vaug_ref):
    xn = _rmsnorm(x_ref[...], g_ref[...])
    xh, xl = _split_bf16(xn)
    outs = (gq_ref, gk_ref, gv_ref, gr_ref, aq_ref, ak_ref, av_ref, gate_ref)
    off = 0
    vals = []
    for o_ref, n in zip(outs, MAIN_SIZES + (gate_ref.shape[-1],)):
        vals.append(_dot(xh, wm_ref[:, off:off + n]))
        if o_ref.ndim == 2:
            o_ref[...] = vals[-1].astype(o_ref.dtype)
        else:
            _store_heads(o_ref, vals[-1])
        off += n
    ak, av = vals[5], vals[6]
    whi = whi_ref[...]
    hp = _dot(xh, whi) + _dot(xl, whi) + _dot(xh, wlo_ref[...])
    iq_ref[...] = hp[:, 0:256]
    ik_ref[...] = hp[:, 256:320]
    misc_ref[...] = hp[:, 384:512]
    ikcat_ref[...], kb_ref[...], vaug_ref[...] = _pack_keys(hp[:, 256:320], ak, av)


def _proj_call(x, g, wm, whi, wlo, tm):
    t, d = x.shape
    gate_w = wm.shape[1] - sum(MAIN_SIZES)
    widths = MAIN_SIZES + (gate_w,)
    dtypes = (BF16, BF16, BF16, BF16, BF16, F32, F32, BF16)
    out_shape = [jax.ShapeDtypeStruct((t, n), dt) for n, dt in zip(widths, dtypes)]
    for n in (5, 6):
        out_shape[n] = jax.ShapeDtypeStruct((t, DSA_KV_HEADS, DSA_HEAD_DIM), F32)
    out_shape += [jax.ShapeDtypeStruct((t, n), F32) for n in (256, 64, 128)]
    out_shape += [jax.ShapeDtypeStruct((t, n), BF16) for n in PACK_WIDTHS]
    row = lambda n: pl.BlockSpec((tm, n), lambda i: (i, 0))
    rows = lambda s: pl.BlockSpec((tm,) + s.shape[1:], lambda i: (i,) + (0,) * (len(s.shape) - 1))
    full = lambda a: pl.BlockSpec(a.shape, lambda i: (0, 0))
    return pl.pallas_call(
        _proj_kernel,
        grid=(t // tm,),
        in_specs=[row(d), full(g), full(wm), full(whi), full(wlo)],
        out_specs=[rows(s) for s in out_shape],
        out_shape=out_shape,
        compiler_params=_params("parallel"),
        name="proj",
    )(x, g, wm, whi, wlo)


def _pack_kernel(ik_ref, k_ref, v_ref, ikcat_ref, kb_ref, vaug_ref):
    ikcat_ref[...], kb_ref[...], vaug_ref[...] = _pack_keys(ik_ref[...], _load_heads(k_ref), _load_heads(v_ref))


def _pack_call(ik, k, v, tm):
    t = ik.shape[0]
    row = lambda n: pl.BlockSpec((tm, n), lambda i: (i, 0))
    heads = pl.BlockSpec((tm,) + k.shape[1:], lambda i: (i, 0, 0))
    return pl.pallas_call(
        _pack_kernel,
        grid=(t // tm,),
        in_specs=[row(ik.shape[1]), heads, heads],
        out_specs=[row(n) for n in PACK_WIDTHS],
        out_shape=[jax.ShapeDtypeStruct((t, n), BF16) for n in PACK_WIDTHS],
        compiler_params=_params("parallel"),
        name="pack",
    )(ik, k, v)


def _gla_kernel(gq_ref, gk_ref, gv_ref, gr_ref, misc_ref, wup_ref, bup_ref, hn_ref, s0_ref,
                o_ref, sfin_ref, st_ref, qd_ref, ki_ref, kr_ref, dl_ref, *, nchunks):
    j = pl.program_id(1)

    @pl.when(j == 0)
    def _():
        for h in range(GLA_HEADS):
            st_ref[h] = s0_ref[h].T

    row = lax.broadcasted_iota(I32, (CHUNK, CHUNK), 0)
    col = lax.broadcasted_iota(I32, (CHUNK, CHUNK), 1)
    causal = row >= col
    tri = causal.astype(BF16)

    m_hi, m_lo = _split_bf16(misc_ref[...])
    z = _dot(jnp.concatenate([m_hi, m_lo, m_hi], axis=1), wup_ref[...]) + bup_ref[...]
    log_a = (jnp.minimum(z, 0.0) - jnp.log(1.0 + jnp.exp(-jnp.abs(z)))) * (1.0 / GLA_GATE_TAU)
    la_hi, la_lo = _split_bf16(log_a)
    tri2 = jnp.concatenate([tri, tri], axis=1)
    for c in range(nchunks):
        rows = slice(c * CHUNK, (c + 1) * CHUNK)
        b = _dot(tri2, jnp.concatenate([la_hi[rows, :], la_lo[rows, :]], axis=0))
        b_last = b[CHUNK - 1:CHUNK, :]
        q = gq_ref[rows, :].astype(F32)
        k = gk_ref[rows, :].astype(F32)
        qd_ref[rows, :] = (q * (GLA_DK ** -0.5) * jnp.exp(b)).astype(BF16)
        ki_ref[rows, :] = (k * jnp.exp(-b)).astype(BF16)
        kr_ref[rows, :] = (k * jnp.exp(b_last - b)).astype(BF16)
        dl_ref[c] = jnp.broadcast_to(jnp.exp(b_last), dl_ref.shape[1:])

    def chunk(c, carry):
        rows = pl.ds(pl.multiple_of(c * CHUNK, CHUNK), CHUNK)
        q_dec, k_inv, k_rem = qd_ref[rows, :], ki_ref[rows, :], kr_ref[rows, :]
        d_last = dl_ref[c][0:1, :]
        v = gv_ref[rows, :]
        gr = gr_ref[rows, :].astype(F32)
        hn = hn_ref[...]
        for h in range(GLA_HEADS):
            ks = slice(h * GLA_DK, (h + 1) * GLA_DK)
            vs = slice(h * GLA_DV, (h + 1) * GLA_DV)
            qh, vh = q_dec[:, ks], v[:, vs]
            scores = jnp.where(causal, _dot_nt(qh, k_inv[:, ks]), 0.0)
            st = st_ref[h]
            o = _dot(scores.astype(BF16), vh) + _dot_nt(qh, st.astype(BF16))
            ut = lax.dot_general(vh, k_rem[:, ks], TN_DIMS, preferred_element_type=F32)
            st_ref[h] = st * d_last[:, ks] + ut
            y = _rmsnorm(o, hn)
            g = gr[:, vs]
            o_ref[rows, vs] = (y * (g * jax.nn.sigmoid(g))).astype(o_ref.dtype)
        return carry

    lax.fori_loop(0, nchunks, chunk, 0, unroll=min(nchunks, GLA_UNROLL))

    @pl.when(j == pl.num_programs(1) - 1)
    def _():
        for h in range(GLA_HEADS):
            sfin_ref[h] = st_ref[h].T


def _gla_call(gq, gk, gv, gr, misc, wup, bup, hn, s0, tb):
    bsz, seq, _ = gq.shape
    blk = lambda n: pl.BlockSpec((None, tb, n), lambda b, j: (b, j, 0))
    full2 = lambda a: pl.BlockSpec(a.shape, lambda b, j: (0, 0))
    st_spec = pl.BlockSpec((None, GLA_HEADS, GLA_DK, GLA_DV), lambda b, j: (b, 0, 0, 0))
    return pl.pallas_call(
        functools.partial(_gla_kernel, nchunks=tb // CHUNK),
        grid=(bsz, seq // tb),
        in_specs=[blk(gq.shape[2]), blk(gk.shape[2]), blk(gv.shape[2]), blk(gr.shape[2]), blk(misc.shape[2]),
                  full2(wup), full2(bup), full2(hn), st_spec],
        out_specs=[blk(gv.shape[2]), st_spec],
        out_shape=[jax.ShapeDtypeStruct((bsz, seq, gv.shape[2]), BF16),
                   jax.ShapeDtypeStruct(s0.shape, F32)],
        scratch_shapes=[pltpu.VMEM((GLA_HEADS, GLA_DV, GLA_DK), F32),
                        pltpu.VMEM((tb, gq.shape[2]), BF16), pltpu.VMEM((tb, gq.shape[2]), BF16),
                        pltpu.VMEM((tb, gq.shape[2]), BF16),
                        pltpu.VMEM((tb // CHUNK, SUBLANES, gq.shape[2]), F32)],
        compiler_params=_params("parallel", "arbitrary"),
        name="gla",
    )(gq, gk, gv, gr, misc, wup, bup, hn, s0)


def _dsa_kernel(iq_ref, misc_ref, aq_ref, ikcat_ref, kb_ref, vaug_ref, o_ref,
                keys_ref, hi_ref, lo_ref, m_ref, acc_ref,
                *, tq, tqp, q_start, n_keys, topk, single_block):
    qi = 0 if single_block else pl.program_id(1)
    kc = KEY_CHUNK
    hd = DSA_HEAD_DIM
    hpg = DSA_HEADS // DSA_KV_HEADS

    first_q = q_start + qi * tq
    last_q = first_q + tq - 1
    span = KEY_UNROLL * kc
    lowest = min if single_block else jnp.minimum
    nt = (lowest(n_keys, ((last_q >> 6) + 1) * CHUNK) + span - 1) // span
    nfull = lowest(n_keys, ((first_q >> 6) + 1) * CHUNK) // span

    def loop(lo, hi, trip, carry):
        if not single_block:
            return lax.fori_loop(lo, hi, trip, carry)
        for t in range(lo, hi):
            carry = trip(t, carry)
        return carry

    def trips(body, masked):
        def trip(t, carry):
            for u in range(KEY_UNROLL):
                carry = body(pl.multiple_of((t * KEY_UNROLL + u) * kc, kc), masked, carry)
            return carry
        return trip

    def for_chunks(body, init):
        return loop(nfull, nt, trips(body, True), loop(0, nfull, trips(body, False), init))

    def for_all_chunks(body, init):
        return loop(0, nt, trips(lambda r0, _, carry: body(r0, carry), None), init)

    def pad_rows(x):
        if tqp == tq:
            return x
        return jnp.concatenate([x, jnp.zeros((tqp - tq, x.shape[1]), x.dtype)], axis=0)

    iq_hi, iq_lo = _split_bf16(pad_rows(iq_ref[...]))
    iq_cat = []
    for h in range(IDX_HEADS):
        s = slice(h * IDX_DIM, (h + 1) * IDX_DIM)
        iq_cat.append(jnp.concatenate([iq_hi[:, s], iq_lo[:, s], iq_hi[:, s]], axis=1))
    wts = pad_rows(misc_ref[...]).T[MISC_IW:MISC_IW + IDX_HEADS, :] * IDX_SCALE
    q_chunk = (first_q + lax.broadcasted_iota(I32, (1, tqp), 1)) >> 6

    def admissible(r0):
        k_pos = r0 + lax.broadcasted_iota(I32, (kc, tqp), 0)
        return ((k_pos >> 6) <= q_chunk) & (k_pos < n_keys)

    def score_body(r0, masked, carry):
        rows = pl.ds(r0, kc)
        ik = ikcat_ref[rows, :]
        acc = jnp.zeros((kc, tqp), F32)
        for h in range(IDX_HEADS):
            acc = acc + wts[h:h + 1, :] * jnp.maximum(_dot_nt(ik, iq_cat[h]), 0.0)
        if masked:
            acc = jnp.where(admissible(r0), acc, -jnp.inf)
        bits = lax.bitcast_convert_type(acc, I32)
        key = jnp.where(bits < 0, bits ^ jnp.int32(0x7FFFFFFF), bits)
        keys_ref[rows, :] = key
        hi_ref[rows, :] = (key >> 16).astype(I16)
        lo_ref[rows, :] = ((key & 0xFFFF) + INT16_MIN).astype(I16)
        return carry

    for_chunks(score_body, 0)

    grp = 32
    one, zero = jnp.asarray(1, BF16), jnp.asarray(0, BF16)

    def count16(ref, pred):
        def body(r0, acc):
            m = jnp.where(pred(ref[pl.ds(r0, kc), :]), one, zero)
            for i in range(kc // grp):
                acc = acc + m[i * grp:(i + 1) * grp, :]
            return acc
        acc = for_all_chunks(body, jnp.zeros((grp, tqp), BF16))
        return jnp.sum(acc.astype(F32), axis=0, keepdims=True)

    def search16(ref, kth):
        def bit_body(i, carry):
            prefix, above = carry
            cand = prefix | (jnp.int32(1) << (15 - i))
            cand16 = (cand + INT16_MIN).astype(I16)
            tot = count16(ref, lambda ch: ch >= cand16)
            keep = tot >= kth
            return jnp.where(keep, cand, prefix), jnp.where(keep, above, tot)
        return lax.fori_loop(0, 16, bit_body, (jnp.zeros((1, tqp), I32), jnp.zeros((1, tqp), F32)))

    kth = jnp.full((1, tqp), topk, F32)
    a_u, above_a = search16(hi_ref, kth)
    a16 = (a_u + INT16_MIN).astype(I16)
    kth_lo = kth - above_a

    def bucket_body(r0, carry):
        rows = pl.ds(r0, kc)
        lo_ref[rows, :] = jnp.where(hi_ref[rows, :] == a16, lo_ref[rows, :], jnp.asarray(INT16_MIN, I16))
        return carry

    for_all_chunks(bucket_body, 0)
    b_u, above_b = search16(lo_ref, kth_lo)
    thr = ((a_u + INT16_MIN) << 16) | b_u
    need = kth_lo - above_b

    tri = (lax.broadcasted_iota(I32, (kc, kc), 0) >= lax.broadcasted_iota(I32, (kc, kc), 1)).astype(BF16)

    def sel_bias(r0, masked, running):
        ch = keys_ref[pl.ds(r0, kc), :]
        eq = ch == thr
        rank = _dot(tri, jnp.where(eq, 1.0, 0.0).astype(BF16)) + running
        sel = (ch > thr) | (eq & (rank <= need))
        if masked:
            sel = sel & admissible(r0)
        return jnp.where(sel, 0.0, MASKED).T[:tq, :], rank[kc - 1:kc, :]

    m_ref[...] = jnp.full(m_ref.shape, MASKED, F32)
    acc_ref[...] = jnp.zeros(acc_ref.shape, F32)
    q_groups = []
    for g in range(DSA_KV_HEADS):
        qg = jnp.concatenate([aq_ref[:, h * hd:(h + 1) * hd] for h in range(g * hpg, (g + 1) * hpg)], axis=0)
        q_groups.append(qg * jnp.asarray(hd ** -0.5, BF16))

    def attend(t, masked, running):
        r0 = pl.multiple_of(t * span, span)
        cols = pl.ds(r0, span)
        biases = []
        for u in range(KEY_UNROLL):
            bias, running = sel_bias(pl.multiple_of(r0 + u * kc, kc), masked, running)
            biases.append(bias)
        bias = jnp.concatenate(biases, axis=1)
        for g in range(DSA_KV_HEADS):
            s = _dot_nt(q_groups[g], kb_ref[cols, g * hd:(g + 1) * hd])
            s = jnp.concatenate([s[r * tq:(r + 1) * tq, :] + bias for r in range(hpg)], axis=0)
            tiles = [s[:, i * LANES:(i + 1) * LANES] for i in range(span // LANES)]
            m_old = m_ref[g]
            m_new = jnp.maximum(m_old, jnp.max(functools.reduce(jnp.maximum, tiles), axis=1, keepdims=True))
            p = jnp.concatenate([jnp.exp(t - m_new) for t in tiles], axis=1)
            pv = _dot(p.astype(BF16), vaug_ref[cols, 2 * g * hd:2 * (g + 1) * hd])
            acc_ref[g] = acc_ref[g] * jnp.exp(m_old - m_new) + pv
            m_ref[g] = m_new
        return running

    no_ties = jnp.zeros((1, tqp), F32)
    loop(nfull, nt, lambda t, c: attend(t, True, c), loop(0, nfull, lambda t, c: attend(t, False, c), no_ties))

    for g in range(DSA_KV_HEADS):
        acc = acc_ref[g]
        out = acc[:, :hd] / acc[:, hd:hd + 1]
        for r in range(hpg):
            h = g * hpg + r
            o_ref[:, h * hd:(h + 1) * hd] = out[r * tq:(r + 1) * tq, :].astype(o_ref.dtype)


def _dsa_call(iq, misc, aq, ikcat, kb, vaug, *, q_start, n_keys, tq):
    bsz, lq, _ = iq.shape
    lk = ikcat.shape[1]
    assert lk % (KEY_UNROLL * KEY_CHUNK) == 0 and lq % tq == 0 and lk // 32 <= 256
    tqp = max(tq, LANES)
    topk = min(TOPK_MAX, n_keys // 4)
    rows_g = (DSA_HEADS // DSA_KV_HEADS) * tq
    qblk = lambda n: pl.BlockSpec((None, tq, n), lambda b, q: (b, q, 0))
    kblk = lambda n: pl.BlockSpec((None, lk, n), lambda b, q: (b, 0, 0))
    kern = functools.partial(_dsa_kernel, tq=tq, tqp=tqp, q_start=q_start, n_keys=n_keys, topk=topk,
                             single_block=(lq == tq))
    return pl.pallas_call(
        kern,
        grid=(bsz, lq // tq),
        in_specs=[qblk(iq.shape[2]), qblk(misc.shape[2]), qblk(aq.shape[2]),
                  kblk(ikcat.shape[2]), kblk(kb.shape[2]), kblk(vaug.shape[2])],
        out_specs=qblk(aq.shape[2]),
        out_shape=jax.ShapeDtypeStruct(aq.shape, BF16),
        scratch_shapes=[pltpu.VMEM((lk, tqp), I32),
                        pltpu.VMEM((lk, tqp), I16),
                        pltpu.VMEM((lk, tqp), I16),
                        pltpu.VMEM((DSA_KV_HEADS, rows_g, LANES), F32),
                        pltpu.VMEM((DSA_KV_HEADS, rows_g, 2 * DSA_HEAD_DIM), F32)],
        compiler_params=_params("parallel", "arbitrary"),
        name="dsa",
    )(iq, misc, aq, ikcat, kb, vaug)


def _merge_kernel(x_ref, go_ref, do_ref, gate_ref, wa_ref, wb_ref, wo_ref, g_ref, y_ref):
    d = x_ref.shape[-1]
    a = _dot(go_ref[...], wa_ref[...])
    b = _dot(do_ref[...], wb_ref[...])
    gate = gate_ref[...].astype(F32)
    mixed = jax.nn.sigmoid(gate[:, :d]) * a + jax.nn.sigmoid(gate[:, d:]) * b
    y_ref[...] = x_ref[...] + _rmsnorm(_dot(mixed.astype(BF16), wo_ref[...]), g_ref[...])


def _merge_call(x, go, do, gate, wa, wb, wo, g, tm):
    t, d = x.shape
    row = lambda n: pl.BlockSpec((tm, n), lambda i: (i, 0))
    full = lambda a: pl.BlockSpec(a.shape, lambda i: (0, 0))
    return pl.pallas_call(
        _merge_kernel,
        grid=(t // tm,),
        in_specs=[row(d), row(go.shape[1]), row(do.shape[1]), row(gate.shape[1]),
                  full(wa), full(wb), full(wo), full(g)],
        out_specs=row(d),
        out_shape=jax.ShapeDtypeStruct((t, d), F32),
        compiler_params=_params("parallel"),
        name="merge",
    )(x, go, do, gate, wa, wb, wo, g)


FFN_ROW_CHUNK = 256
FFN_SLAB_ROWS = 64
GELU_C0 = float(np.sqrt(2.0 / np.pi))
GELU_C1 = 0.044715 * GELU_C0


def _ffn_kernel(x_ref, gpre_ref, wu_ref, cw_ref, cb_ref, pv_ref, wdh_ref, gpost_ref,
                y_ref, cn_ref, xn_ref, acc_ref, car_ref, act_ref, u_ref, *, seq, tn):
    i = pl.program_id(0)
    tm = x_ref.shape[0]
    f = wdh_ref.shape[0]
    xn_ref[...] = _rmsnorm(x_ref[...], gpre_ref[...]).astype(BF16)
    acc_ref[...] = jnp.zeros_like(acc_ref)
    first = (i % max(seq // tm, 1)) == 0
    top = lax.broadcasted_iota(I32, (SUBLANES, tn), 0)
    rc = min(tm, FFN_ROW_CHUNK)

    def up(cols, r):
        return _dot(xn_ref[r * rc:(r + 1) * rc, :], wu_ref[:, cols])

    def conv_slab(cols, u, halo):
        r1, r2 = pltpu.roll(u, 1, 0), pltpu.roll(u, 2, 0)
        top1 = jnp.where(top == 0, halo[1:2, :], r1[:SUBLANES, :])
        top2 = jnp.where(top == 0, halo[0:1, :], jnp.where(top == 1, halo[1:2, :], r2[:SUBLANES, :]))
        u1 = jnp.concatenate([top1, r1[SUBLANES:, :]], axis=0)
        u2 = jnp.concatenate([top2, r2[SUBLANES:, :]], axis=0)
        cw = cw_ref[:, cols]
        return cb_ref[:, cols] + (cw[0:1, :] * u2 + cw[1:2, :] * u1 + cw[2:3, :] * u), u[u.shape[0] - 2:, :]

    period = min(seq, tm)
    sl = min(rc, FFN_SLAB_ROWS, period)

    def keep_tail(cols, slot, half, end, k):
        cn_ref[k, :, cols] = u_ref[slot, half, end - (CONV_W - 1):end, :]
        if period == tm:
            car_ref[:, cols] = u_ref[slot, half, end - SUBLANES:end, :]

    def start_halo(cols, k):
        if period == tm:
            return jnp.where(first, pv_ref[0, :, cols], car_ref[SUBLANES - 2:SUBLANES, cols])
        return pv_ref[k, :, cols]

    nft = f // tn
    nrc = tm // rc

    def tile(j, aslot, with_down):
        cols_a = pl.ds(pl.multiple_of(j * tn, tn), tn)
        cols_b = pl.ds(pl.multiple_of(f + j * tn, tn), tn)
        halo_a = halo_b = None

        def up_to(slot, r):
            u_ref[slot, 0] = up(cols_a, r)
            u_ref[slot, 1] = up(cols_b, r)

        up_to(0, 0)
        for r in range(nrc):
            rows = slice(r * rc, (r + 1) * rc)
            slot = r % 2
            if r + 1 < nrc:
                up_to(1 - slot, r + 1)
            if with_down:
                project_down(j - 1, 1 - aslot, rows)
            for s in range(0, rc, sl):
                row0 = r * rc + s
                if row0 % period == 0:
                    halo_a, halo_b = start_halo(cols_a, row0 // period), start_halo(cols_b, row0 // period)
                a, halo_a = conv_slab(cols_a, u_ref[slot, 0, s:s + sl, :], halo_a)
                b, halo_b = conv_slab(cols_b, u_ref[slot, 1, s:s + sl, :], halo_b)
                act = (a * (1.0 + jnp.tanh(a * (GELU_C0 + GELU_C1 * (a * a))))) * b
                act_ref[aslot, row0:row0 + sl, :] = act.astype(BF16)
                if (row0 + sl) % period == 0:
                    keep_tail(cols_a, slot, 0, s + sl, row0 // period)
                    keep_tail(cols_b, slot, 1, s + sl, row0 // period)

    def project_down(j, aslot, rows):
        acc_ref[rows, :] += _dot(act_ref[aslot, rows, :], wdh_ref[pl.ds(pl.multiple_of(j * tn, tn), tn), :])

    assert nft % 2 == 1
    tile(0, 0, False)

    def trip(p, carry):
        tile(2 * p + 1, 1, True)
        tile(2 * p + 2, 0, True)
        return carry

    lax.fori_loop(0, nft // 2, trip, 0)
    for r in range(nrc):
        project_down(nft - 1, 0, slice(r * rc, (r + 1) * rc))
    y_ref[...] = x_ref[...] + _rmsnorm(acc_ref[...], gpost_ref[...])


def _ffn_call(x, gpre, w_up, conv_w, conv_b, conv_prev, w_down_half, gpost, *, tm, seq, tn):
    t, d = x.shape
    f = w_down_half.shape[0]
    nseq = t // seq
    assert f % tn == 0 and (seq % tm == 0 or tm % seq == 0) and t % tm == 0
    bps = max(seq // tm, 1)
    spb = max(tm // seq, 1)
    xrow = pl.BlockSpec((tm, d), lambda i: (i, 0))
    const = lambda a: pl.BlockSpec(a.shape, lambda i: (0, 0), pipeline_mode=pl.Buffered(1))
    state = lambda step: pl.BlockSpec((spb, CONV_W - 1, 2 * f), lambda i: (step(i), 0, 0))
    y, cn = pl.pallas_call(
        functools.partial(_ffn_kernel, seq=seq, tn=tn),
        grid=(t // tm,),
        in_specs=[xrow, const(gpre), const(w_up), const(conv_w), const(conv_b),
                  state(lambda i: i // bps), const(w_down_half), const(gpost)],
        out_specs=[xrow, state(lambda i: i)],
        out_shape=[jax.ShapeDtypeStruct((t, d), F32),
                   jax.ShapeDtypeStruct((t // tm * spb, CONV_W - 1, 2 * f), F32)],
        scratch_shapes=[pltpu.VMEM((tm, d), BF16), pltpu.VMEM((tm, d), F32),
                        pltpu.VMEM((SUBLANES, 2 * f), F32), pltpu.VMEM((2, tm, tn), BF16),
                        pltpu.VMEM((2, 2, min(tm, FFN_ROW_CHUNK), tn), F32)],
        compiler_params=_params("arbitrary"),
        name="ffn",
    )(x, gpre, w_up, conv_w, conv_b, conv_prev, w_down_half, gpost)
    return y, cn.reshape(nseq, bps, CONV_W - 1, 2 * f)[:, bps - 1]


def _prep_weights(w_in, w_gate_up, d_model):
    sizes = (GLA_HEADS * GLA_DK, GLA_HEADS * GLA_DK, GLA_HEADS * GLA_DV, GLA_RANK, GLA_HEADS * GLA_DV,
             DSA_HEADS * DSA_HEAD_DIM, DSA_KV_HEADS * DSA_HEAD_DIM, DSA_KV_HEADS * DSA_HEAD_DIM,
             IDX_HEADS * IDX_DIM, IDX_DIM, IDX_HEADS, 2 * d_model)
    offs = np.concatenate([[0], np.cumsum(sizes)])
    g_q, g_k, g_v, g_low, g_r, a_q, a_k, a_v, i_q, i_k, i_w, gate = (
        w_in[:, int(offs[n]):int(offs[n + 1])] for n in range(len(sizes)))
    w_main = jnp.concatenate([g_q, g_k, g_v, g_r, a_q, a_k, a_v, gate], axis=1).astype(BF16)
    zeros = lambda n: jnp.zeros((w_in.shape[0], n), w_in.dtype)
    misc = jnp.concatenate([g_low, i_w, zeros(LANES - GLA_RANK - IDX_HEADS)], axis=1)
    assert MISC_GLOW == 0 and MISC_IW == GLA_RANK
    w_hp = jnp.concatenate([i_q, i_k, zeros(LANES - IDX_DIM), misc], axis=1)
    assert w_hp.shape[1] == HP_WIDTH
    w_hp_hi, w_hp_lo = _split_bf16(w_hp)
    wup = jnp.concatenate([w_gate_up, jnp.zeros((LANES - GLA_RANK, w_gate_up.shape[1]), w_gate_up.dtype)], axis=0)
    wup_hi, wup_lo = _split_bf16(wup)
    return w_main, w_hp_hi, w_hp_lo, jnp.concatenate([wup_hi, wup_hi, wup_lo], axis=0)


def _run_layer(x, k_past, v_past, kidx_past, gla_s0, conv_prev, w, *, tm, gla_tb, dsa_tq, ffn_tm):
    bsz, seq, d = x.shape
    past = k_past.shape[1]
    t = bsz * seq
    x2 = x.reshape(t, d)
    gq, gk, gv, gr, aq, ak, av, gate, iq, ik, misc, ikcat, kb, vaug = _proj_call(
        x2, w["attn_pre_norm"], w["w_main"], w["w_hp_hi"], w["w_hp_lo"], tm)
    per_seq = lambda a: a.reshape(bsz, seq, a.shape[-1])

    gla_o, gla_s = _gla_call(per_seq(gq), per_seq(gk), per_seq(gv), per_seq(gr), per_seq(misc),
                             w["wup"], w["b_gla_gate"], w["gla_head_norm"], gla_s0, gla_tb)

    n_keys = past + seq
    key_span = KEY_UNROLL * KEY_CHUNK
    lk = -(-n_keys // key_span) * key_span
    packed = [per_seq(a) for a in (ikcat, kb, vaug)]
    if past:
        flat = lambda a: a.reshape((bsz * past,) + a.shape[2:]).astype(F32)
        packed_past = _pack_call(flat(kidx_past), flat(k_past), flat(v_past), min(bsz * past, PACK_ROWS))
        packed = [jnp.concatenate([p.reshape(bsz, past, -1), n], axis=1) for p, n in zip(packed_past, packed)]
    if lk > n_keys:
        packed = [jnp.pad(a, ((0, 0), (0, lk - n_keys), (0, 0))) for a in packed]

    dsa_o = _dsa_call(per_seq(iq), per_seq(misc), per_seq(aq), *packed,
                      q_start=past, n_keys=n_keys, tq=dsa_tq)

    x1 = _merge_call(x2, gla_o.reshape(t, -1), dsa_o.reshape(t, -1), gate,
                     w["w_branch_gla"], w["w_branch_dsa"], w["w_out"], w["attn_post_norm"], min(t, MERGE_ROWS))
    y, conv_new = _ffn_call(x1, w["ffn_pre_norm"], w["w_up"], w["conv_w"], w["conv_b"], conv_prev,
                            w["w_down_half"], w["ffn_post_norm"], tm=ffn_tm, seq=seq, tn=256)
    k_new = ak.reshape(bsz, seq, DSA_KV_HEADS, DSA_HEAD_DIM)
    v_new = av.reshape(bsz, seq, DSA_KV_HEADS, DSA_HEAD_DIM)
    return y.reshape(bsz, seq, d), k_new, v_new, per_seq(ik), gla_s, conv_new


def kernel(x_prompt, x_sample, cache_k, cache_v, cache_k_idx, state_gla, state_ffn_conv, attn_pre_norm, w_in, w_gla_gate_up, b_gla_gate, gla_head_norm, w_branch_gla, w_branch_dsa, w_out, attn_post_norm, ffn_pre_norm, w_up, conv_w, conv_b, w_down, ffn_post_norm):
    depth = w_in.shape[0]
    bsz, _, d = x_prompt.shape
    dt = x_prompt.dtype
    y_prompt, y_sample = x_prompt, x_sample
    prompt_new, sample_new = [], []
    row = lambda a: a.reshape(1, -1)
    for l in range(depth):
        w_main, w_hp_hi, w_hp_lo, wup = _prep_weights(w_in[l], w_gla_gate_up[l], d)
        w = dict(attn_pre_norm=row(attn_pre_norm[l]), w_main=w_main, w_hp_hi=w_hp_hi, w_hp_lo=w_hp_lo, wup=wup,
                 b_gla_gate=row(b_gla_gate[l]), gla_head_norm=row(gla_head_norm[l]),
                 w_branch_gla=w_branch_gla[l].astype(BF16), w_branch_dsa=w_branch_dsa[l].astype(BF16),
                 w_out=w_out[l].astype(BF16), attn_post_norm=row(attn_post_norm[l]),
                 ffn_pre_norm=row(ffn_pre_norm[l]), w_up=w_up[l].astype(BF16), conv_w=conv_w[l],
                 conv_b=row(conv_b[l]), w_down_half=(0.5 * w_down[l]).astype(BF16),
                 ffn_post_norm=row(ffn_post_norm[l]))
        ffn2 = conv_w.shape[-1]
        y_prompt, *new = _run_layer(
            y_prompt,
            jnp.zeros((bsz, 0, DSA_KV_HEADS * DSA_HEAD_DIM), dt),
            jnp.zeros((bsz, 0, DSA_KV_HEADS * DSA_HEAD_DIM), dt),
            jnp.zeros((bsz, 0, IDX_DIM), dt),
            jnp.zeros((bsz, GLA_HEADS, GLA_DK, GLA_DV), F32),
            jnp.zeros((bsz, CONV_W - 1, ffn2), dt),
            w, tm=256, gla_tb=512, dsa_tq=256, ffn_tm=1024)
        prompt_new.append(new)
        y_sample, *new = _run_layer(
            y_sample, cache_k[l], cache_v[l], cache_k_idx[l], state_gla[l], state_ffn_conv[l],
            w, tm=256, gla_tb=x_sample.shape[1], dsa_tq=x_sample.shape[1],
            ffn_tm=min(1024, x_sample.shape[0] * x_sample.shape[1]))
        sample_new.append(new)
    k_p, v_p, kidx_p, gla_p, conv_p = (jnp.stack(t, axis=0) for t in zip(*prompt_new))
    k_s, v_s, kidx_s, gla_s, conv_s = (jnp.stack(t, axis=0) for t in zip(*sample_new))
    return (y_prompt, y_sample, k_p, v_p, kidx_p, gla_p, conv_p, k_s, v_s, kidx_s, gla_s, conv_s)
```

```python
import functools

import numpy as np
import jax
import jax.numpy as jnp
from jax import lax
from jax.experimental import pallas as pl
from jax.experimental.pallas import tpu as pltpu

F32 = jnp.float32
BF16 = jnp.bfloat16
I32 = jnp.int32
I16 = jnp.int16
HIGHEST = lax.Precision.HIGHEST

CHUNK = 64
GLA_HEADS = 4
GLA_DK = 128
GLA_DV = 256
GLA_RANK = 16
GLA_GATE_TAU = 16.0
DSA_HEADS = 8
DSA_KV_HEADS = 2
DSA_HEAD_DIM = 64
IDX_HEADS = 4
IDX_DIM = 64
IDX_SCALE = (IDX_HEADS * IDX_DIM) ** -0.5
TOPK_MAX = 256
CONV_W = 3
EPS = 1e-6

LANES = 128
SUBLANES = 8
VMEM_LIMIT = 56 * 1024 * 1024

MISC_GLOW = 0
MISC_IW = 16

KEY_CHUNK = 256
KEY_UNROLL = 2
GLA_UNROLL = 8
INT16_MIN = -(2 ** 15)
MASKED = -1e30
NT_DIMS = (((1,), (1,)), ((), ()))
TN_DIMS = (((0,), (0,)), ((), ()))


def _dot(a, b, **kw):
    return jnp.dot(a, b, preferred_element_type=F32, **kw)


def _dot_nt(a, b):
    return lax.dot_general(a, b, NT_DIMS, preferred_element_type=F32)


def _split_bf16(x):
    hi = x.astype(BF16)
    lo = (x - hi.astype(F32)).astype(BF16)
    return hi, lo


def _rmsnorm(x, g):
    return x * lax.rsqrt(jnp.mean(x * x, axis=-1, keepdims=True) + EPS) * g


def _params(*sem):
    return pltpu.CompilerParams(dimension_semantics=sem, vmem_limit_bytes=VMEM_LIMIT)


MAIN_SIZES = (GLA_HEADS * GLA_DK, GLA_HEADS * GLA_DK, GLA_HEADS * GLA_DV, GLA_HEADS * GLA_DV,
              DSA_HEADS * DSA_HEAD_DIM, DSA_KV_HEADS * DSA_HEAD_DIM, DSA_KV_HEADS * DSA_HEAD_DIM)
HP_WIDTH = 512
PROJ_ROWS = 256
PACK_ROWS = 1024
GLA_ROWS = 512
DSA_QUERIES = 256
MERGE_ROWS = 512
FFN_ROWS = 1024
FFN_TILE = 256
PACK_WIDTHS = (3 * IDX_DIM, DSA_KV_HEADS * DSA_HEAD_DIM, 2 * DSA_KV_HEADS * DSA_HEAD_DIM)


def _pack_keys(ik, k, v):
    hi, lo = _split_bf16(ik)
    ikcat = jnp.concatenate([hi, hi, lo], axis=1)
    vb = v.astype(BF16)
    ones = jnp.ones((v.shape[0], DSA_HEAD_DIM), BF16)
    parts = []
    for g in range(DSA_KV_HEADS):
        parts += [vb[:, g * DSA_HEAD_DIM:(g + 1) * DSA_HEAD_DIM], ones]
    return ikcat, k.astype(BF16), jnp.concatenate(parts, axis=1)


def _store_heads(o_ref, x):
    dim = o_ref.shape[-1]
    for g in range(o_ref.shape[1]):
        o_ref[:, g, :] = x[:, g * dim:(g + 1) * dim].astype(o_ref.dtype)


def _load_heads(x_ref):
    return jnp.concatenate([x_ref[:, g, :] for g in range(x_ref.shape[1])], axis=1)


def _proj_kernel(x_ref, g_ref, wm_ref, whi_ref, wlo_ref,
                 gq_ref, gk_ref, gv_ref, gr_ref, aq_ref, ak_ref, av_ref, gate_ref, iq_ref, ik_ref, misc_ref,
                 ikcat_ref, kb_ref, vaug_ref):
    xn = _rmsnorm(x_ref[...], g_ref[...])
    xh, xl = _split_bf16(xn)
    outs = (gq_ref, gk_ref, gv_ref, gr_ref, aq_ref, ak_ref, av_ref, gate_ref)
    off = 0
    vals = []
    for o_ref, n in zip(outs, MAIN_SIZES + (gate_ref.shape[-1],)):
        vals.append(_dot(xh, wm_ref[:, off:off + n]))
        if o_ref.ndim == 2:
            o_ref[...] = vals[-1].astype(o_ref.dtype)
        else:
            _store_heads(o_ref, vals[-1])
        off += n
    ak, av = vals[5], vals[6]
    whi = whi_ref[...]
    hp = _dot(xh, whi) + _dot(xl, whi) + _dot(xh, wlo_ref[...])
    iq_ref[...] = hp[:, 0:256]
    ik_ref[...] = hp[:, 256:320]
    misc_ref[...] = hp[:, 384:512]
    ikcat_ref[...], kb_ref[...], vaug_ref[...] = _pack_keys(hp[:, 256:320], ak, av)


def _proj_call(x, g, wm, whi, wlo, tm):
    t, d = x.shape
    gate_w = wm.shape[1] - sum(MAIN_SIZES)
    widths = MAIN_SIZES + (gate_w,)
    dtypes = (BF16, BF16, BF16, BF16, BF16, F32, F32, BF16)
    out_shape = [jax.ShapeDtypeStruct((t, n), dt) for n, dt in zip(widths, dtypes)]
    for n in (5, 6):
        out_shape[n] = jax.ShapeDtypeStruct((t, DSA_KV_HEADS, DSA_HEAD_DIM), F32)
    out_shape += [jax.ShapeDtypeStruct((t, n), F32) for n in (256, 64, 128)]
    out_shape += [jax.ShapeDtypeStruct((t, n), BF16) for n in PACK_WIDTHS]
    row = lambda n: pl.BlockSpec((tm, n), lambda i: (i, 0))
    rows = lambda s: pl.BlockSpec((tm,) + s.shape[1:], lambda i: (i,) + (0,) * (len(s.shape) - 1))
    full = lambda a: pl.BlockSpec(a.shape, lambda i: (0, 0))
    return pl.pallas_call(
        _proj_kernel,
        grid=(t // tm,),
        in_specs=[row(d), full(g), full(wm), full(whi), full(wlo)],
        out_specs=[rows(s) for s in out_shape],
        out_shape=out_shape,
        compiler_params=_params("parallel"),
        name="proj",
    )(x, g, wm, whi, wlo)


def _pack_kernel(ik_ref, k_ref, v_ref, ikcat_ref, kb_ref, vaug_ref):
    ikcat_ref[...], kb_ref[...], vaug_ref[...] = _pack_keys(ik_ref[...], _load_heads(k_ref), _load_heads(v_ref))


def _pack_call(ik, k, v, tm):
    t = ik.shape[0]
    row = lambda n: pl.BlockSpec((tm, n), lambda i: (i, 0))
    heads = pl.BlockSpec((tm,) + k.shape[1:], lambda i: (i, 0, 0))
    return pl.pallas_call(
        _pack_kernel,
        grid=(t // tm,),
        in_specs=[row(ik.shape[1]), heads, heads],
        out_specs=[row(n) for n in PACK_WIDTHS],
        out_shape=[jax.ShapeDtypeStruct((t, n), BF16) for n in PACK_WIDTHS],
        compiler_params=_params("parallel"),
        name="pack",
    )(ik, k, v)


def _gla_kernel(gq_ref, gk_ref, gv_ref, gr_ref, misc_ref, wup_ref, bup_ref, hn_ref, s0_ref,
                o_ref, sfin_ref, st_ref, qd_ref, ki_ref, kr_ref, dl_ref, *, nchunks):
    j = pl.program_id(1)

    @pl.when(j == 0)
    def _():
        for h in range(GLA_HEADS):
            st_ref[h] = s0_ref[h].T

    row = lax.broadcasted_iota(I32, (CHUNK, CHUNK), 0)
    col = lax.broadcasted_iota(I32, (CHUNK, CHUNK), 1)
    causal = row >= col
    tri = causal.astype(BF16)

    m_hi, m_lo = _split_bf16(misc_ref[...])
    z = _dot(jnp.concatenate([m_hi, m_lo, m_hi], axis=1), wup_ref[...]) + bup_ref[...]
    log_a = (jnp.minimum(z, 0.0) - jnp.log(1.0 + jnp.exp(-jnp.abs(z)))) * (1.0 / GLA_GATE_TAU)
    la_hi, la_lo = _split_bf16(log_a)
    tri2 = jnp.concatenate([tri, tri], axis=1)
    for c in range(nchunks):
        rows = slice(c * CHUNK, (c + 1) * CHUNK)
        b = _dot(tri2, jnp.concatenate([la_hi[rows, :], la_lo[rows, :]], axis=0))
        b_last = b[CHUNK - 1:CHUNK, :]
        q = gq_ref[rows, :].astype(F32)
        k = gk_ref[rows, :].astype(F32)
        qd_ref[rows, :] = (q * (GLA_DK ** -0.5) * jnp.exp(b)).astype(BF16)
        ki_ref[rows, :] = (k * jnp.exp(-b)).astype(BF16)
        kr_ref[rows, :] = (k * jnp.exp(b_last - b)).astype(BF16)
        dl_ref[c] = jnp.broadcast_to(jnp.exp(b_last), dl_ref.shape[1:])

    def chunk(c, carry):
        rows = pl.ds(pl.multiple_of(c * CHUNK, CHUNK), CHUNK)
        q_dec, k_inv, k_rem = qd_ref[rows, :], ki_ref[rows, :], kr_ref[rows, :]
        d_last = dl_ref[c][0:1, :]
        v = gv_ref[rows, :]
        gr = gr_ref[rows, :].astype(F32)
        hn = hn_ref[...]
        for h in range(GLA_HEADS):
            ks = slice(h * GLA_DK, (h + 1) * GLA_DK)
            vs = slice(h * GLA_DV, (h + 1) * GLA_DV)
            qh, vh = q_dec[:, ks], v[:, vs]
            scores = jnp.where(causal, _dot_nt(qh, k_inv[:, ks]), 0.0)
            st = st_ref[h]
            o = _dot(scores.astype(BF16), vh) + _dot_nt(qh, st.astype(BF16))
            ut = lax.dot_general(vh, k_rem[:, ks], TN_DIMS, preferred_element_type=F32)
            st_ref[h] = st * d_last[:, ks] + ut
            y = _rmsnorm(o, hn)
            g = gr[:, vs]
            o_ref[rows, vs] = (y * (g * jax.nn.sigmoid(g))).astype(o_ref.dtype)
        return carry

    lax.fori_loop(0, nchunks, chunk, 0, unroll=min(nchunks, GLA_UNROLL))

    @pl.when(j == pl.num_programs(1) - 1)
    def _():
        for h in range(GLA_HEADS):
            sfin_ref[h] = st_ref[h].T


def _gla_call(gq, gk, gv, gr, misc, wup, bup, hn, s0, tb):
    bsz, seq, _ = gq.shape
    blk = lambda n: pl.BlockSpec((None, tb, n), lambda b, j: (b, j, 0))
    full2 = lambda a: pl.BlockSpec(a.shape, lambda b, j: (0, 0))
    st_spec = pl.BlockSpec((None, GLA_HEADS, GLA_DK, GLA_DV), lambda b, j: (b, 0, 0, 0))
    return pl.pallas_call(
        functools.partial(_gla_kernel, nchunks=tb // CHUNK),
        grid=(bsz, seq // tb),
        in_specs=[blk(gq.shape[2]), blk(gk.shape[2]), blk(gv.shape[2]), blk(gr.shape[2]), blk(misc.shape[2]),
                  full2(wup), full2(bup), full2(hn), st_spec],
        out_specs=[blk(gv.shape[2]), st_spec],
        out_shape=[jax.ShapeDtypeStruct((bsz, seq, gv.shape[2]), BF16),
                   jax.ShapeDtypeStruct(s0.shape, F32)],
        scratch_shapes=[pltpu.VMEM((GLA_HEADS, GLA_DV, GLA_DK), F32),
                        pltpu.VMEM((tb, gq.shape[2]), BF16), pltpu.VMEM((tb, gq.shape[2]), BF16),
                        pltpu.VMEM((tb, gq.shape[2]), BF16),
                        pltpu.VMEM((tb // CHUNK, SUBLANES, gq.shape[2]), F32)],
        compiler_params=_params("parallel", "arbitrary"),
        name="gla",
    )(gq, gk, gv, gr, misc, wup, bup, hn, s0)


def _dsa_kernel(iq_ref, misc_ref, aq_ref, ikcat_ref, kb_ref, vaug_ref, o_ref,
                keys_ref, hi_ref, lo_ref, m_ref, acc_ref,
                *, tq, tqp, q_start, n_keys, topk, single_block):
    qi = 0 if single_block else pl.program_id(1)
    kc = KEY_CHUNK
    hd = DSA_HEAD_DIM
    hpg = DSA_HEADS // DSA_KV_HEADS

    first_q = q_start + qi * tq
    last_q = first_q + tq - 1
    span = KEY_UNROLL * kc
    lowest = min if single_block else jnp.minimum
    nt = (lowest(n_keys, ((last_q >> 6) + 1) * CHUNK) + span - 1) // span
    nfull = lowest(n_keys, ((first_q >> 6) + 1) * CHUNK) // span

    def loop(lo, hi, trip, carry):
        if not single_block:
            return lax.fori_loop(lo, hi, trip, carry)
        for t in range(lo, hi):
            carry = trip(t, carry)
        return carry

    def trips(body, masked):
        def trip(t, carry):
            for u in range(KEY_UNROLL):
                carry = body(pl.multiple_of((t * KEY_UNROLL + u) * kc, kc), masked, carry)
            return carry
        return trip

    def for_chunks(body, init):
        return loop(nfull, nt, trips(body, True), loop(0, nfull, trips(body, False), init))

    def for_all_chunks(body, init):
        return loop(0, nt, trips(lambda r0, _, carry: body(r0, carry), None), init)

    def pad_rows(x):
        if tqp == tq:
            return x
        return jnp.concatenate([x, jnp.zeros((tqp - tq, x.shape[1]), x.dtype)], axis=0)

    iq_hi, iq_lo = _split_bf16(pad_rows(iq_ref[...]))
    iq_cat = []
    for h in range(IDX_HEADS):
        s = slice(h * IDX_DIM, (h + 1) * IDX_DIM)
        iq_cat.append(jnp.concatenate([iq_hi[:, s], iq_lo[:, s], iq_hi[:, s]], axis=1))
    wts = pad_rows(misc_ref[...]).T[MISC_IW:MISC_IW + IDX_HEADS, :] * IDX_SCALE
    q_chunk = (first_q + lax.broadcasted_iota(I32, (1, tqp), 1)) >> 6

    def admissible(r0):
        k_pos = r0 + lax.broadcasted_iota(I32, (kc, tqp), 0)
        return ((k_pos >> 6) <= q_chunk) & (k_pos < n_keys)

    def score_body(r0, masked, carry):
        rows = pl.ds(r0, kc)
        ik = ikcat_ref[rows, :]
        acc = jnp.zeros((kc, tqp), F32)
        for h in range(IDX_HEADS):
            acc = acc + wts[h:h + 1, :] * jnp.maximum(_dot_nt(ik, iq_cat[h]), 0.0)
        if masked:
            acc = jnp.where(admissible(r0), acc, -jnp.inf)
        bits = lax.bitcast_convert_type(acc, I32)
        key = jnp.where(bits < 0, bits ^ jnp.int32(0x7FFFFFFF), bits)
        keys_ref[rows, :] = key
        hi_ref[rows, :] = (key >> 16).astype(I16)
        lo_ref[rows, :] = ((key & 0xFFFF) + INT16_MIN).astype(I16)
        return carry

    for_chunks(score_body, 0)

    grp = 32
    one, zero = jnp.asarray(1, BF16), jnp.asarray(0, BF16)

    def count16(ref, pred):
        def body(r0, acc):
            m = jnp.where(pred(ref[pl.ds(r0, kc), :]), one, zero)
            for i in range(kc // grp):
                acc = acc + m[i * grp:(i + 1) * grp, :]
            return acc
        acc = for_all_chunks(body, jnp.zeros((grp, tqp), BF16))
        return jnp.sum(acc.astype(F32), axis=0, keepdims=True)

    def search16(ref, kth):
        def bit_body(i, carry):
            prefix, above = carry
            cand = prefix | (jnp.int32(1) << (15 - i))
            cand16 = (cand + INT16_MIN).astype(I16)
            tot = count16(ref, lambda ch: ch >= cand16)
            keep = tot >= kth
            return jnp.where(keep, cand, prefix), jnp.where(keep, above, tot)
        return lax.fori_loop(0, 16, bit_body, (jnp.zeros((1, tqp), I32), jnp.zeros((1, tqp), F32)))

    kth = jnp.full((1, tqp), topk, F32)
    a_u, above_a = search16(hi_ref, kth)
    a16 = (a_u + INT16_MIN).astype(I16)
    kth_lo = kth - above_a

    def bucket_body(r0, carry):
        rows = pl.ds(r0, kc)
        lo_ref[rows, :] = jnp.where(hi_ref[rows, :] == a16, lo_ref[rows, :], jnp.asarray(INT16_MIN, I16))
        return carry

    for_all_chunks(bucket_body, 0)
    b_u, above_b = search16(lo_ref, kth_lo)
    thr = ((a_u + INT16_MIN) << 16) | b_u
    need = kth_lo - above_b

    tri = (lax.broadcasted_iota(I32, (kc, kc), 0) >= lax.broadcasted_iota(I32, (kc, kc), 1)).astype(BF16)

    def sel_bias(r0, masked, running):
        ch = keys_ref[pl.ds(r0, kc), :]
        eq = ch == thr
        rank = _dot(tri, jnp.where(eq, 1.0, 0.0).astype(BF16)) + running
        sel = (ch > thr) | (eq & (rank <= need))
        if masked:
            sel = sel & admissible(r0)
        return jnp.where(sel, 0.0, MASKED).T[:tq, :], rank[kc - 1:kc, :]

    m_ref[...] = jnp.full(m_ref.shape, MASKED, F32)
    acc_ref[...] = jnp.zeros(acc_ref.shape, F32)
    q_groups = []
    for g in range(DSA_KV_HEADS):
        qg = jnp.concatenate([aq_ref[:, h * hd:(h + 1) * hd] for h in range(g * hpg, (g + 1) * hpg)], axis=0)
        q_groups.append(qg * jnp.asarray(hd ** -0.5, BF16))

    def attend(t, masked, running):
        r0 = pl.multiple_of(t * span, span)
        cols = pl.ds(r0, span)
        biases = []
        for u in range(KEY_UNROLL):
            bias, running = sel_bias(pl.multiple_of(r0 + u * kc, kc), masked, running)
            biases.append(bias)
        bias = jnp.concatenate(biases, axis=1)
        for g in range(DSA_KV_HEADS):
            s = _dot_nt(q_groups[g], kb_ref[cols, g * hd:(g + 1) * hd])
            s = jnp.concatenate([s[r * tq:(r + 1) * tq, :] + bias for r in range(hpg)], axis=0)
            tiles = [s[:, i * LANES:(i + 1) * LANES] for i in range(span // LANES)]
            m_old = m_ref[g]
            m_new = jnp.maximum(m_old, jnp.max(functools.reduce(jnp.maximum, tiles), axis=1, keepdims=True))
            p = jnp.concatenate([jnp.exp(t - m_new) for t in tiles], axis=1)
            pv = _dot(p.astype(BF16), vaug_ref[cols, 2 * g * hd:2 * (g + 1) * hd])
            acc_ref[g] = acc_ref[g] * jnp.exp(m_old - m_new) + pv
            m_ref[g] = m_new
        return running

    no_ties = jnp.zeros((1, tqp), F32)
    loop(nfull, nt, lambda t, c: attend(t, True, c), loop(0, nfull, lambda t, c: attend(t, False, c), no_ties))

    for g in range(DSA_KV_HEADS):
        acc = acc_ref[g]
        out = acc[:, :hd] / acc[:, hd:hd + 1]
        for r in range(hpg):
            h = g * hpg + r
            o_ref[:, h * hd:(h + 1) * hd] = out[r * tq:(r + 1) * tq, :].astype(o_ref.dtype)


def _dsa_call(iq, misc, aq, ikcat, kb, vaug, *, q_start, n_keys, tq):
    bsz, lq, _ = iq.shape
    lk = ikcat.shape[1]
    assert lk % (KEY_UNROLL * KEY_CHUNK) == 0 and lq % tq == 0 and lk // 32 <= 256
    tqp = max(tq, LANES)
    topk = min(TOPK_MAX, n_keys // 4)
    rows_g = (DSA_HEADS // DSA_KV_HEADS) * tq
    qblk = lambda n: pl.BlockSpec((None, tq, n), lambda b, q: (b, q, 0))
    kblk = lambda n: pl.BlockSpec((None, lk, n), lambda b, q: (b, 0, 0))
    kern = functools.partial(_dsa_kernel, tq=tq, tqp=tqp, q_start=q_start, n_keys=n_keys, topk=topk,
                             single_block=(lq == tq))
    return pl.pallas_call(
        kern,
        grid=(bsz, lq // tq),
        in_specs=[qblk(iq.shape[2]), qblk(misc.shape[2]), qblk(aq.shape[2]),
                  kblk(ikcat.shape[2]), kblk(kb.shape[2]), kblk(vaug.shape[2])],
        out_specs=qblk(aq.shape[2]),
        out_shape=jax.ShapeDtypeStruct(aq.shape, BF16),
        scratch_shapes=[pltpu.VMEM((lk, tqp), I32),
                        pltpu.VMEM((lk, tqp), I16),
                        pltpu.VMEM((lk, tqp), I16),
                        pltpu.VMEM((DSA_KV_HEADS, rows_g, LANES), F32),
                        pltpu.VMEM((DSA_KV_HEADS, rows_g, 2 * DSA_HEAD_DIM), F32)],
        compiler_params=_params("parallel", "arbitrary"),
        name="dsa",
    )(iq, misc, aq, ikcat, kb, vaug)


def _merge_kernel(x_ref, go_ref, do_ref, gate_ref, wa_ref, wb_ref, wo_ref, g_ref, y_ref):
    d = x_ref.shape[-1]
    a = _dot(go_ref[...], wa_ref[...])
    b = _dot(do_ref[...], wb_ref[...])
    gate = gate_ref[...].astype(F32)
    mixed = jax.nn.sigmoid(gate[:, :d]) * a + jax.nn.sigmoid(gate[:, d:]) * b
    y_ref[...] = x_ref[...] + _rmsnorm(_dot(mixed.astype(BF16), wo_ref[...]), g_ref[...])


def _merge_call(x, go, do, gate, wa, wb, wo, g, tm):
    t, d = x.shape
    row = lambda n: pl.BlockSpec((tm, n), lambda i: (i, 0))
    full = lambda a: pl.BlockSpec(a.shape, lambda i: (0, 0))
    return pl.pallas_call(
        _merge_kernel,
        grid=(t // tm,),
        in_specs=[row(d), row(go.shape[1]), row(do.shape[1]), row(gate.shape[1]),
                  full(wa), full(wb), full(wo), full(g)],
        out_specs=row(d),
        out_shape=jax.ShapeDtypeStruct((t, d), F32),
        compiler_params=_params("parallel"),
        name="merge",
    )(x, go, do, gate, wa, wb, wo, g)


FFN_ROW_CHUNK = 256
FFN_SLAB_ROWS = 64
GELU_C0 = float(np.sqrt(2.0 / np.pi))
GELU_C1 = 0.044715 * GELU_C0


def _ffn_kernel(x_ref, gpre_ref, wu_ref, cw_ref, cb_ref, pv_ref, wdh_ref, gpost_ref,
                y_ref, cn_ref, xn_ref, acc_ref, car_ref, act_ref, u_ref, *, seq, tn):
    i = pl.program_id(0)
    tm = x_ref.shape[0]
    f = wdh_ref.shape[0]
    xn_ref[...] = _rmsnorm(x_ref[...], gpre_ref[...]).astype(BF16)
    acc_ref[...] = jnp.zeros_like(acc_ref)
    first = (i % max(seq // tm, 1)) == 0
    top = lax.broadcasted_iota(I32, (SUBLANES, tn), 0)
    rc = min(tm, FFN_ROW_CHUNK)

    def up(cols, r):
        return _dot(xn_ref[r * rc:(r + 1) * rc, :], wu_ref[:, cols])

    def conv_slab(cols, u, halo):
        r1, r2 = pltpu.roll(u, 1, 0), pltpu.roll(u, 2, 0)
        top1 = jnp.where(top == 0, halo[1:2, :], r1[:SUBLANES, :])
        top2 = jnp.where(top == 0, halo[0:1, :], jnp.where(top == 1, halo[1:2, :], r2[:SUBLANES, :]))
        u1 = jnp.concatenate([top1, r1[SUBLANES:, :]], axis=0)
        u2 = jnp.concatenate([top2, r2[SUBLANES:, :]], axis=0)
        cw = cw_ref[:, cols]
        return cb_ref[:, cols] + (cw[0:1, :] * u2 + cw[1:2, :] * u1 + cw[2:3, :] * u), u[u.shape[0] - 2:, :]

    period = min(seq, tm)
    sl = min(rc, FFN_SLAB_ROWS, period)

    def keep_tail(cols, slot, half, end, k):
        cn_ref[k, :, cols] = u_ref[slot, half, end - (CONV_W - 1):end, :]
        if period == tm:
            car_ref[:, cols] = u_ref[slot, half, end - SUBLANES:end, :]

    def start_halo(cols, k):
        if period == tm:
            return jnp.where(first, pv_ref[0, :, cols], car_ref[SUBLANES - 2:SUBLANES, cols])
        return pv_ref[k, :, cols]

    nft = f // tn
    nrc = tm // rc

    def tile(j, aslot, with_down):
        cols_a = pl.ds(pl.multiple_of(j * tn, tn), tn)
        cols_b = pl.ds(pl.multiple_of(f + j * tn, tn), tn)
        halo_a = halo_b = None

        def up_to(slot, r):
            u_ref[slot, 0] = up(cols_a, r)
            u_ref[slot, 1] = up(cols_b, r)

        up_to(0, 0)
        for r in range(nrc):
            rows = slice(r * rc, (r + 1) * rc)
            slot = r % 2
            if r + 1 < nrc:
                up_to(1 - slot, r + 1)
            if with_down:
                project_down(j - 1, 1 - aslot, rows)
            for s in range(0, rc, sl):
                row0 = r * rc + s
                if row0 % period == 0:
                    halo_a, halo_b = start_halo(cols_a, row0 // period), start_halo(cols_b, row0 // period)
                a, halo_a = conv_slab(cols_a, u_ref[slot, 0, s:s + sl, :], halo_a)
                b, halo_b = conv_slab(cols_b, u_ref[slot, 1, s:s + sl, :], halo_b)
                act = (a * (1.0 + jnp.tanh(a * (GELU_C0 + GELU_C1 * (a * a))))) * b
                act_ref[aslot, row0:row0 + sl, :] = act.astype(BF16)
                if (row0 + sl) % period == 0:
                    keep_tail(cols_a, slot, 0, s + sl, row0 // period)
                    keep_tail(cols_b, slot, 1, s + sl, row0 // period)

    def project_down(j, aslot, rows):
        acc_ref[rows, :] += _dot(act_ref[aslot, rows, :], wdh_ref[pl.ds(pl.multiple_of(j * tn, tn), tn), :])

    assert nft % 2 == 1
    tile(0, 0, False)

    def trip(p, carry):
        tile(2 * p + 1, 1, True)
        tile(2 * p + 2, 0, True)
        return carry

    lax.fori_loop(0, nft // 2, trip, 0)
    for r in range(nrc):
        project_down(nft - 1, 0, slice(r * rc, (r + 1) * rc))
    y_ref[...] = x_ref[...] + _rmsnorm(acc_ref[...], gpost_ref[...])


def _ffn_call(x, gpre, w_up, conv_w, conv_b, conv_prev, w_down_half, gpost, *, tm, seq, tn):
    t, d = x.shape
    f = w_down_half.shape[0]
    nseq = t // seq
    assert f % tn == 0 and (seq % tm == 0 or tm % seq == 0) and t % tm == 0
    bps = max(seq // tm, 1)
    spb = max(tm // seq, 1)
    xrow = pl.BlockSpec((tm, d), lambda i: (i, 0))
    const = lambda a: pl.BlockSpec(a.shape, lambda i: (0, 0), pipeline_mode=pl.Buffered(1))
    state = lambda step: pl.BlockSpec((spb, CONV_W - 1, 2 * f), lambda i: (step(i), 0, 0))
    y, cn = pl.pallas_call(
        functools.partial(_ffn_kernel, seq=seq, tn=tn),
        grid=(t // tm,),
        in_specs=[xrow, const(gpre), const(w_up), const(conv_w), const(conv_b),
                  state(lambda i: i // bps), const(w_down_half), const(gpost)],
        out_specs=[xrow, state(lambda i: i)],
        out_shape=[jax.ShapeDtypeStruct((t, d), F32),
                   jax.ShapeDtypeStruct((t // tm * spb, CONV_W - 1, 2 * f), F32)],
        scratch_shapes=[pltpu.VMEM((tm, d), BF16), pltpu.VMEM((tm, d), F32),
                        pltpu.VMEM((SUBLANES, 2 * f), F32), pltpu.VMEM((2, tm, tn), BF16),
                        pltpu.VMEM((2, 2, min(tm, FFN_ROW_CHUNK), tn), F32)],
        compiler_params=_params("arbitrary"),
        name="ffn",
    )(x, gpre, w_up, conv_w, conv_b, conv_prev, w_down_half, gpost)
    return y, cn.reshape(nseq, bps, CONV_W - 1, 2 * f)[:, bps - 1]


def _prep_weights(w_in, w_gate_up, d_model):
    sizes = (GLA_HEADS * GLA_DK, GLA_HEADS * GLA_DK, GLA_HEADS * GLA_DV, GLA_RANK, GLA_HEADS * GLA_DV,
             DSA_HEADS * DSA_HEAD_DIM, DSA_KV_HEADS * DSA_HEAD_DIM, DSA_KV_HEADS * DSA_HEAD_DIM,
             IDX_HEADS * IDX_DIM, IDX_DIM, IDX_HEADS, 2 * d_model)
    offs = np.concatenate([[0], np.cumsum(sizes)])
    g_q, g_k, g_v, g_low, g_r, a_q, a_k, a_v, i_q, i_k, i_w, gate = (
        w_in[:, int(offs[n]):int(offs[n + 1])] for n in range(len(sizes)))
    w_main = jnp.concatenate([g_q, g_k, g_v, g_r, a_q, a_k, a_v, gate], axis=1).astype(BF16)
    zeros = lambda n: jnp.zeros((w_in.shape[0], n), w_in.dtype)
    misc = jnp.concatenate([g_low, i_w, zeros(LANES - GLA_RANK - IDX_HEADS)], axis=1)
    assert MISC_GLOW == 0 and MISC_IW == GLA_RANK
    w_hp = jnp.concatenate([i_q, i_k, zeros(LANES - IDX_DIM), misc], axis=1)
    assert w_hp.shape[1] == HP_WIDTH
    w_hp_hi, w_hp_lo = _split_bf16(w_hp)
    wup = jnp.concatenate([w_gate_up, jnp.zeros((LANES - GLA_RANK, w_gate_up.shape[1]), w_gate_up.dtype)], axis=0)
    wup_hi, wup_lo = _split_bf16(wup)
    return w_main, w_hp_hi, w_hp_lo, jnp.concatenate([wup_hi, wup_hi, wup_lo], axis=0)


def _run_layer(x, k_past, v_past, kidx_past, gla_s0, conv_prev, w):
    bsz, seq, d = x.shape
    past = k_past.shape[1]
    t = bsz * seq
    tm, gla_tb, dsa_tq, ffn_tm = min(t, PROJ_ROWS), min(seq, GLA_ROWS), min(seq, DSA_QUERIES), min(t, FFN_ROWS)
    x2 = x.reshape(t, d)
    gq, gk, gv, gr, aq, ak, av, gate, iq, ik, misc, ikcat, kb, vaug = _proj_call(
        x2, w["attn_pre_norm"], w["w_main"], w["w_hp_hi"], w["w_hp_lo"], tm)
    per_seq = lambda a: a.reshape(bsz, seq, a.shape[-1])

    gla_o, gla_s = _gla_call(per_seq(gq), per_seq(gk), per_seq(gv), per_seq(gr), per_seq(misc),
                             w["wup"], w["b_gla_gate"], w["gla_head_norm"], gla_s0, gla_tb)

    n_keys = past + seq
    key_span = KEY_UNROLL * KEY_CHUNK
    lk = -(-n_keys // key_span) * key_span
    packed = [per_seq(a) for a in (ikcat, kb, vaug)]
    if past:
        flat = lambda a: a.reshape((bsz * past,) + a.shape[2:]).astype(F32)
        packed_past = _pack_call(flat(kidx_past), flat(k_past), flat(v_past), min(bsz * past, PACK_ROWS))
        packed = [jnp.concatenate([p.reshape(bsz, past, -1), n], axis=1) for p, n in zip(packed_past, packed)]
    if lk > n_keys:
        packed = [jnp.pad(a, ((0, 0), (0, lk - n_keys), (0, 0))) for a in packed]

    dsa_o = _dsa_call(per_seq(iq), per_seq(misc), per_seq(aq), *packed,
                      q_start=past, n_keys=n_keys, tq=dsa_tq)

    x1 = _merge_call(x2, gla_o.reshape(t, -1), dsa_o.reshape(t, -1), gate,
                     w["w_branch_gla"], w["w_branch_dsa"], w["w_out"], w["attn_post_norm"], min(t, MERGE_ROWS))
    y, conv_new = _ffn_call(x1, w["ffn_pre_norm"], w["w_up"], w["conv_w"], w["conv_b"], conv_prev,
                            w["w_down_half"], w["ffn_post_norm"], tm=ffn_tm, seq=seq, tn=FFN_TILE)
    k_new = ak.reshape(bsz, seq, DSA_KV_HEADS, DSA_HEAD_DIM)
    v_new = av.reshape(bsz, seq, DSA_KV_HEADS, DSA_HEAD_DIM)
    return y.reshape(bsz, seq, d), k_new, v_new, per_seq(ik), gla_s, conv_new


def kernel(x_prompt, x_sample, cache_k, cache_v, cache_k_idx, state_gla, state_ffn_conv, attn_pre_norm, w_in, w_gla_gate_up, b_gla_gate, gla_head_norm, w_branch_gla, w_branch_dsa, w_out, attn_post_norm, ffn_pre_norm, w_up, conv_w, conv_b, w_down, ffn_post_norm):
    depth = w_in.shape[0]
    bsz, _, d = x_prompt.shape
    dt = x_prompt.dtype
    y_prompt, y_sample = x_prompt, x_sample
    prompt_new, sample_new = [], []
    row = lambda a: a.reshape(1, -1)
    for l in range(depth):
        w_main, w_hp_hi, w_hp_lo, wup = _prep_weights(w_in[l], w_gla_gate_up[l], d)
        w = dict(attn_pre_norm=row(attn_pre_norm[l]), w_main=w_main, w_hp_hi=w_hp_hi, w_hp_lo=w_hp_lo, wup=wup,
                 b_gla_gate=row(b_gla_gate[l]), gla_head_norm=row(gla_head_norm[l]),
                 w_branch_gla=w_branch_gla[l].astype(BF16), w_branch_dsa=w_branch_dsa[l].astype(BF16),
                 w_out=w_out[l].astype(BF16), attn_post_norm=row(attn_post_norm[l]),
                 ffn_pre_norm=row(ffn_pre_norm[l]), w_up=w_up[l].astype(BF16), conv_w=conv_w[l],
                 conv_b=row(conv_b[l]), w_down_half=(0.5 * w_down[l]).astype(BF16),
                 ffn_post_norm=row(ffn_post_norm[l]))
        ffn2 = conv_w.shape[-1]
        y_prompt, *new = _run_layer(
            y_prompt,
            jnp.zeros((bsz, 0, DSA_KV_HEADS * DSA_HEAD_DIM), dt),
            jnp.zeros((bsz, 0, DSA_KV_HEADS * DSA_HEAD_DIM), dt),
            jnp.zeros((bsz, 0, IDX_DIM), dt),
            jnp.zeros((bsz, GLA_HEADS, GLA_DK, GLA_DV), F32),
            jnp.zeros((bsz, CONV_W - 1, ffn2), dt),
            w)
        prompt_new.append(new)
        y_sample, *new = _run_layer(
            y_sample, cache_k[l], cache_v[l], cache_k_idx[l], state_gla[l], state_ffn_conv[l],
            w)
        sample_new.append(new)
    k_p, v_p, kidx_p, gla_p, conv_p = (jnp.stack(t, axis=0) for t in zip(*prompt_new))
    k_s, v_s, kidx_s, gla_s, conv_s = (jnp.stack(t, axis=0) for t in zip(*sample_new))
    return (y_prompt, y_sample, k_p, v_p, kidx_p, gla_p, conv_p, k_s, v_s, kidx_s, gla_s, conv_s)
```

```python
import functools

import numpy as np
import jax
import jax.numpy as jnp
from jax import lax
from jax.experimental import pallas as pl
from jax.experimental.pallas import tpu as pltpu

F32 = jnp.float32
BF16 = jnp.bfloat16
I32 = jnp.int32
I16 = jnp.int16
HIGHEST = lax.Precision.HIGHEST

CHUNK = 64
GLA_HEADS = 4
GLA_DK = 128
GLA_DV = 256
GLA_RANK = 16
GLA_GATE_TAU = 16.0
DSA_HEADS = 8
DSA_KV_HEADS = 2
DSA_HEAD_DIM = 64
IDX_HEADS = 4
IDX_DIM = 64
IDX_SCALE = (IDX_HEADS * IDX_DIM) ** -0.5
TOPK_MAX = 256
CONV_W = 3
EPS = 1e-6

LANES = 128
SUBLANES = 8
VMEM_LIMIT = 56 * 1024 * 1024

MISC_GLOW = 0
MISC_IW = 16

KEY_CHUNK = 256
KEY_UNROLL = 2
GLA_UNROLL = 8
INT16_MIN = -(2 ** 15)
ATTN_Q_SCALE = float(DSA_HEAD_DIM ** -0.5 * np.log2(np.e))
MASKED = -1e30
NT_DIMS = (((1,), (1,)), ((), ()))
TN_DIMS = (((0,), (0,)), ((), ()))


def _dot(a, b, **kw):
    return jnp.dot(a, b, preferred_element_type=F32, **kw)


def _dot_nt(a, b):
    return lax.dot_general(a, b, NT_DIMS, preferred_element_type=F32)


def _split_bf16(x):
    hi = x.astype(BF16)
    lo = (x - hi.astype(F32)).astype(BF16)
    return hi, lo


def _rmsnorm(x, g):
    return x * lax.rsqrt(jnp.mean(x * x, axis=-1, keepdims=True) + EPS) * g


def _params(*sem):
    return pltpu.CompilerParams(dimension_semantics=sem, vmem_limit_bytes=VMEM_LIMIT)


MAIN_SIZES = (GLA_HEADS * GLA_DK, GLA_HEADS * GLA_DK, GLA_HEADS * GLA_DV, GLA_HEADS * GLA_DV,
              DSA_HEADS * DSA_HEAD_DIM, DSA_KV_HEADS * DSA_HEAD_DIM, DSA_KV_HEADS * DSA_HEAD_DIM)
HP_WIDTH = 512
PROJ_ROWS = 256
PACK_ROWS = 1024
GLA_ROWS = 512
DSA_QUERIES = 256
MERGE_ROWS = 512
FFN_ROWS = 1024
FFN_TILE = 256
PACK_WIDTHS = (3 * IDX_DIM, DSA_KV_HEADS * DSA_HEAD_DIM, 2 * DSA_KV_HEADS * DSA_HEAD_DIM)


def _pack_keys(ik, k, v):
    hi, lo = _split_bf16(ik)
    ikcat = jnp.concatenate([hi, hi, lo], axis=1)
    vb = v.astype(BF16)
    ones = jnp.ones((v.shape[0], DSA_HEAD_DIM), BF16)
    parts = []
    for g in range(DSA_KV_HEADS):
        parts += [vb[:, g * DSA_HEAD_DIM:(g + 1) * DSA_HEAD_DIM], ones]
    return ikcat, k.astype(BF16), jnp.concatenate(parts, axis=1)


def _store_heads(o_ref, x):
    dim = o_ref.shape[-1]
    for g in range(o_ref.shape[1]):
        o_ref[:, g, :] = x[:, g * dim:(g + 1) * dim].astype(o_ref.dtype)


def _load_heads(x_ref):
    return jnp.concatenate([x_ref[:, g, :] for g in range(x_ref.shape[1])], axis=1)


def _proj_kernel(x_ref, g_ref, wm_ref, whi_ref, wlo_ref,
                 gq_ref, gk_ref, gv_ref, gr_ref, aq_ref, ak_ref, av_ref, gate_ref, iq_ref, ik_ref, misc_ref,
                 ikcat_ref, kb_ref, vaug_ref):
    xn = _rmsnorm(x_ref[...], g_ref[...])
    xh, xl = _split_bf16(xn)
    outs = (gq_ref, gk_ref, gv_ref, gr_ref, aq_ref, ak_ref, av_ref, gate_ref)
    off = 0
    vals = []
    for o_ref, n in zip(outs, MAIN_SIZES + (gate_ref.shape[-1],)):
        vals.append(_dot(xh, wm_ref[:, off:off + n]))
        if o_ref is aq_ref:
            vals[-1] = vals[-1] * ATTN_Q_SCALE
        if o_ref.ndim == 2:
            o_ref[...] = vals[-1].astype(o_ref.dtype)
        else:
            _store_heads(o_ref, vals[-1])
        off += n
    ak, av = vals[5], vals[6]
    whi = whi_ref[...]
    hp = _dot(xh, whi) + _dot(xl, whi) + _dot(xh, wlo_ref[...])
    iq_ref[...] = hp[:, 0:256]
    ik_ref[...] = hp[:, 256:320]
    misc_ref[...] = hp[:, 384:512]
    ikcat_ref[...], kb_ref[...], vaug_ref[...] = _pack_keys(hp[:, 256:320], ak, av)


def _proj_call(x, g, wm, whi, wlo, tm):
    t, d = x.shape
    gate_w = wm.shape[1] - sum(MAIN_SIZES)
    widths = MAIN_SIZES + (gate_w,)
    dtypes = (BF16, BF16, BF16, BF16, BF16, F32, F32, BF16)
    out_shape = [jax.ShapeDtypeStruct((t, n), dt) for n, dt in zip(widths, dtypes)]
    for n in (5, 6):
        out_shape[n] = jax.ShapeDtypeStruct((t, DSA_KV_HEADS, DSA_HEAD_DIM), F32)
    out_shape += [jax.ShapeDtypeStruct((t, n), F32) for n in (256, 64, 128)]
    out_shape += [jax.ShapeDtypeStruct((t, n), BF16) for n in PACK_WIDTHS]
    row = lambda n: pl.BlockSpec((tm, n), lambda i: (i, 0))
    rows = lambda s: pl.BlockSpec((tm,) + s.shape[1:], lambda i: (i,) + (0,) * (len(s.shape) - 1))
    full = lambda a: pl.BlockSpec(a.shape, lambda i: (0, 0))
    return pl.pallas_call(
        _proj_kernel,
        grid=(t // tm,),
        in_specs=[row(d), full(g), full(wm), full(whi), full(wlo)],
        out_specs=[rows(s) for s in out_shape],
        out_shape=out_shape,
        compiler_params=_params("parallel"),
        name="proj",
    )(x, g, wm, whi, wlo)


def _pack_kernel(ik_ref, k_ref, v_ref, ikcat_ref, kb_ref, vaug_ref):
    ikcat_ref[...], kb_ref[...], vaug_ref[...] = _pack_keys(ik_ref[...], _load_heads(k_ref), _load_heads(v_ref))


def _pack_call(ik, k, v, tm):
    t = ik.shape[0]
    row = lambda n: pl.BlockSpec((tm, n), lambda i: (i, 0))
    heads = pl.BlockSpec((tm,) + k.shape[1:], lambda i: (i, 0, 0))
    return pl.pallas_call(
        _pack_kernel,
        grid=(t // tm,),
        in_specs=[row(ik.shape[1]), heads, heads],
        out_specs=[row(n) for n in PACK_WIDTHS],
        out_shape=[jax.ShapeDtypeStruct((t, n), BF16) for n in PACK_WIDTHS],
        compiler_params=_params("parallel"),
        name="pack",
    )(ik, k, v)


def _gla_kernel(gq_ref, gk_ref, gv_ref, gr_ref, misc_ref, wup_ref, bup_ref, hn_ref, s0_ref,
                o_ref, sfin_ref, st_ref, qd_ref, ki_ref, kr_ref, dl_ref, *, nchunks):
    j = pl.program_id(1)

    @pl.when(j == 0)
    def _():
        for h in range(GLA_HEADS):
            st_ref[h] = s0_ref[h].T

    row = lax.broadcasted_iota(I32, (CHUNK, CHUNK), 0)
    col = lax.broadcasted_iota(I32, (CHUNK, CHUNK), 1)
    causal = row >= col
    tri = causal.astype(BF16)

    m_hi, m_lo = _split_bf16(misc_ref[...])
    z = _dot(jnp.concatenate([m_hi, m_lo, m_hi], axis=1), wup_ref[...]) + bup_ref[...]
    log_a = (jnp.minimum(z, 0.0) - jnp.log(1.0 + jnp.exp(-jnp.abs(z)))) * (1.0 / GLA_GATE_TAU)
    la_hi, la_lo = _split_bf16(log_a)
    tri2 = jnp.concatenate([tri, tri], axis=1)
    for c in range(nchunks):
        rows = slice(c * CHUNK, (c + 1) * CHUNK)
        b = _dot(tri2, jnp.concatenate([la_hi[rows, :], la_lo[rows, :]], axis=0))
        b_last = b[CHUNK - 1:CHUNK, :]
        q = gq_ref[rows, :].astype(F32)
        k = gk_ref[rows, :].astype(F32)
        qd_ref[rows, :] = (q * (GLA_DK ** -0.5) * jnp.exp(b)).astype(BF16)
        ki_ref[rows, :] = (k * jnp.exp(-b)).astype(BF16)
        kr_ref[rows, :] = (k * jnp.exp(b_last - b)).astype(BF16)
        dl_ref[c] = jnp.broadcast_to(jnp.exp(b_last), dl_ref.shape[1:])

    def chunk(c, carry):
        rows = pl.ds(pl.multiple_of(c * CHUNK, CHUNK), CHUNK)
        q_dec, k_inv, k_rem = qd_ref[rows, :], ki_ref[rows, :], kr_ref[rows, :]
        d_last = dl_ref[c][0:1, :]
        v = gv_ref[rows, :]
        gr = gr_ref[rows, :].astype(F32)
        hn = hn_ref[...]
        for h in range(GLA_HEADS):
            ks = slice(h * GLA_DK, (h + 1) * GLA_DK)
            vs = slice(h * GLA_DV, (h + 1) * GLA_DV)
            qh, vh = q_dec[:, ks], v[:, vs]
            scores = jnp.where(causal, _dot_nt(qh, k_inv[:, ks]), 0.0)
            st = st_ref[h]
            o = _dot(scores.astype(BF16), vh) + _dot_nt(qh, st.astype(BF16))
            ut = lax.dot_general(vh, k_rem[:, ks], TN_DIMS, preferred_element_type=F32)
            st_ref[h] = st * d_last[:, ks] + ut
            y = _rmsnorm(o, hn)
            g = gr[:, vs]
            o_ref[rows, vs] = (y * (g * jax.nn.sigmoid(g))).astype(o_ref.dtype)
        return carry

    lax.fori_loop(0, nchunks, chunk, 0, unroll=min(nchunks, GLA_UNROLL))

    @pl.when(j == pl.num_programs(1) - 1)
    def _():
        for h in range(GLA_HEADS):
            sfin_ref[h] = st_ref[h].T


def _gla_call(gq, gk, gv, gr, misc, wup, bup, hn, s0, tb):
    bsz, seq, _ = gq.shape
    blk = lambda n: pl.BlockSpec((None, tb, n), lambda b, j: (b, j, 0))
    full2 = lambda a: pl.BlockSpec(a.shape, lambda b, j: (0, 0))
    st_spec = pl.BlockSpec((None, GLA_HEADS, GLA_DK, GLA_DV), lambda b, j: (b, 0, 0, 0))
    return pl.pallas_call(
        functools.partial(_gla_kernel, nchunks=tb // CHUNK),
        grid=(bsz, seq // tb),
        in_specs=[blk(gq.shape[2]), blk(gk.shape[2]), blk(gv.shape[2]), blk(gr.shape[2]), blk(misc.shape[2]),
                  full2(wup), full2(bup), full2(hn), st_spec],
        out_specs=[blk(gv.shape[2]), st_spec],
        out_shape=[jax.ShapeDtypeStruct((bsz, seq, gv.shape[2]), BF16),
                   jax.ShapeDtypeStruct(s0.shape, F32)],
        scratch_shapes=[pltpu.VMEM((GLA_HEADS, GLA_DV, GLA_DK), F32),
                        pltpu.VMEM((tb, gq.shape[2]), BF16), pltpu.VMEM((tb, gq.shape[2]), BF16),
                        pltpu.VMEM((tb, gq.shape[2]), BF16),
                        pltpu.VMEM((tb // CHUNK, SUBLANES, gq.shape[2]), F32)],
        compiler_params=_params("parallel", "arbitrary"),
        name="gla",
    )(gq, gk, gv, gr, misc, wup, bup, hn, s0)


def _dsa_kernel(iq_ref, misc_ref, aq_ref, ikcat_ref, kb_ref, vaug_ref, o_ref,
                keys_ref, hi_ref, lo_ref, m_ref, acc_ref,
                *, tq, tqp, q_start, n_keys, topk, single_block):
    qi = 0 if single_block else pl.program_id(1)
    kc = KEY_CHUNK
    hd = DSA_HEAD_DIM
    hpg = DSA_HEADS // DSA_KV_HEADS

    first_q = q_start + qi * tq
    last_q = first_q + tq - 1
    span = KEY_UNROLL * kc
    lowest = min if single_block else jnp.minimum
    nt = (lowest(n_keys, ((last_q >> 6) + 1) * CHUNK) + span - 1) // span
    nfull = lowest(n_keys, ((first_q >> 6) + 1) * CHUNK) // span

    def loop(lo, hi, trip, carry):
        if not single_block:
            return lax.fori_loop(lo, hi, trip, carry)
        for t in range(lo, hi):
            carry = trip(t, carry)
        return carry

    def trips(body, masked):
        def trip(t, carry):
            for u in range(KEY_UNROLL):
                carry = body(pl.multiple_of((t * KEY_UNROLL + u) * kc, kc), masked, carry)
            return carry
        return trip

    def for_chunks(body, init):
        return loop(nfull, nt, trips(body, True), loop(0, nfull, trips(body, False), init))

    def for_all_chunks(body, init):
        return loop(0, nt, trips(lambda r0, _, carry: body(r0, carry), None), init)

    def pad_rows(x):
        if tqp == tq:
            return x
        return jnp.concatenate([x, jnp.zeros((tqp - tq, x.shape[1]), x.dtype)], axis=0)

    iq_hi, iq_lo = _split_bf16(pad_rows(iq_ref[...]))
    iq_cat = []
    for h in range(IDX_HEADS):
        s = slice(h * IDX_DIM, (h + 1) * IDX_DIM)
        iq_cat.append(jnp.concatenate([iq_hi[:, s], iq_lo[:, s], iq_hi[:, s]], axis=1))
    wts = pad_rows(misc_ref[...]).T[MISC_IW:MISC_IW + IDX_HEADS, :] * IDX_SCALE
    q_chunk = (first_q + lax.broadcasted_iota(I32, (1, tqp), 1)) >> 6

    def admissible(r0):
        k_pos = r0 + lax.broadcasted_iota(I32, (kc, tqp), 0)
        return ((k_pos >> 6) <= q_chunk) & (k_pos < n_keys)

    def score_body(r0, masked, carry):
        rows = pl.ds(r0, kc)
        ik = ikcat_ref[rows, :]
        acc = jnp.zeros((kc, tqp), F32)
        for h in range(IDX_HEADS):
            acc = acc + wts[h:h + 1, :] * jnp.maximum(_dot_nt(ik, iq_cat[h]), 0.0)
        if masked:
            acc = jnp.where(admissible(r0), acc, -jnp.inf)
        bits = lax.bitcast_convert_type(acc, I32)
        key = jnp.where(bits < 0, bits ^ jnp.int32(0x7FFFFFFF), bits)
        keys_ref[rows, :] = key
        hi_ref[rows, :] = (key >> 16).astype(I16)
        lo_ref[rows, :] = ((key & 0xFFFF) + INT16_MIN).astype(I16)
        return carry

    for_chunks(score_body, 0)

    grp = 32
    one, zero = jnp.asarray(1, BF16), jnp.asarray(0, BF16)

    def count16(ref, pred):
        def body(r0, acc):
            m = jnp.where(pred(ref[pl.ds(r0, kc), :]), one, zero)
            for i in range(kc // grp):
                acc = acc + m[i * grp:(i + 1) * grp, :]
            return acc
        acc = for_all_chunks(body, jnp.zeros((grp, tqp), BF16))
        return jnp.sum(acc.astype(F32), axis=0, keepdims=True)

    def search16(ref, kth):
        def bit_body(i, carry):
            prefix, above = carry
            cand = prefix | (jnp.int32(1) << (15 - i))
            cand16 = (cand + INT16_MIN).astype(I16)
            tot = count16(ref, lambda ch: ch >= cand16)
            keep = tot >= kth
            return jnp.where(keep, cand, prefix), jnp.where(keep, above, tot)
        return lax.fori_loop(0, 16, bit_body, (jnp.zeros((1, tqp), I32), jnp.zeros((1, tqp), F32)))

    kth = jnp.full((1, tqp), topk, F32)
    a_u, above_a = search16(hi_ref, kth)
    a16 = (a_u + INT16_MIN).astype(I16)
    kth_lo = kth - above_a

    def bucket_body(r0, carry):
        rows = pl.ds(r0, kc)
        lo_ref[rows, :] = jnp.where(hi_ref[rows, :] == a16, lo_ref[rows, :], jnp.asarray(INT16_MIN, I16))
        return carry

    for_all_chunks(bucket_body, 0)
    b_u, above_b = search16(lo_ref, kth_lo)
    thr = ((a_u + INT16_MIN) << 16) | b_u
    need = kth_lo - above_b

    tri = (lax.broadcasted_iota(I32, (kc, kc), 0) >= lax.broadcasted_iota(I32, (kc, kc), 1)).astype(BF16)

    def sel_bias(r0, masked, running):
        ch = keys_ref[pl.ds(r0, kc), :]
        eq = ch == thr
        rank = _dot(tri, jnp.where(eq, 1.0, 0.0).astype(BF16)) + running
        sel = (ch > thr) | (eq & (rank <= need))
        if masked:
            sel = sel & admissible(r0)
        return jnp.where(sel, 0.0, MASKED).T[:tq, :], rank[kc - 1:kc, :]

    m_ref[...] = jnp.full(m_ref.shape, MASKED, F32)
    acc_ref[...] = jnp.zeros(acc_ref.shape, F32)
    q_groups = []
    for g in range(DSA_KV_HEADS):
        qg = jnp.concatenate([aq_ref[:, h * hd:(h + 1) * hd] for h in range(g * hpg, (g + 1) * hpg)], axis=0)
        q_groups.append(qg)

    def attend(t, masked, running):
        r0 = pl.multiple_of(t * span, span)
        cols = pl.ds(r0, span)
        biases = []
        for u in range(KEY_UNROLL):
            bias, running = sel_bias(pl.multiple_of(r0 + u * kc, kc), masked, running)
            biases.append(bias)
        bias = jnp.concatenate(biases, axis=1)
        for g in range(DSA_KV_HEADS):
            s = _dot_nt(q_groups[g], kb_ref[cols, g * hd:(g + 1) * hd])
            s = jnp.concatenate([s[r * tq:(r + 1) * tq, :] + bias for r in range(hpg)], axis=0)
            tiles = [s[:, i * LANES:(i + 1) * LANES] for i in range(span // LANES)]
            m_old = m_ref[g]
            m_new = jnp.maximum(m_old, jnp.max(functools.reduce(jnp.maximum, tiles), axis=1, keepdims=True))
            p = jnp.concatenate([jnp.exp2(t - m_new) for t in tiles], axis=1)
            pv = _dot(p.astype(BF16), vaug_ref[cols, 2 * g * hd:2 * (g + 1) * hd])
            acc_ref[g] = acc_ref[g] * jnp.exp2(m_old - m_new) + pv
            m_ref[g] = m_new
        return running

    no_ties = jnp.zeros((1, tqp), F32)
    loop(nfull, nt, lambda t, c: attend(t, True, c), loop(0, nfull, lambda t, c: attend(t, False, c), no_ties))

    for g in range(DSA_KV_HEADS):
        acc = acc_ref[g]
        out = acc[:, :hd] / acc[:, hd:hd + 1]
        for r in range(hpg):
            h = g * hpg + r
            o_ref[:, h * hd:(h + 1) * hd] = out[r * tq:(r + 1) * tq, :].astype(o_ref.dtype)


def _dsa_call(iq, misc, aq, ikcat, kb, vaug, *, q_start, n_keys, tq):
    bsz, lq, _ = iq.shape
    lk = ikcat.shape[1]
    assert lk % (KEY_UNROLL * KEY_CHUNK) == 0 and lq % tq == 0 and lk // 32 <= 256
    tqp = max(tq, LANES)
    topk = min(TOPK_MAX, n_keys // 4)
    rows_g = (DSA_HEADS // DSA_KV_HEADS) * tq
    qblk = lambda n: pl.BlockSpec((None, tq, n), lambda b, q: (b, q, 0))
    kblk = lambda n: pl.BlockSpec((None, lk, n), lambda b, q: (b, 0, 0))
    kern = functools.partial(_dsa_kernel, tq=tq, tqp=tqp, q_start=q_start, n_keys=n_keys, topk=topk,
                             single_block=(lq == tq))
    return pl.pallas_call(
        kern,
        grid=(bsz, lq // tq),
        in_specs=[qblk(iq.shape[2]), qblk(misc.shape[2]), qblk(aq.shape[2]),
                  kblk(ikcat.shape[2]), kblk(kb.shape[2]), kblk(vaug.shape[2])],
        out_specs=qblk(aq.shape[2]),
        out_shape=jax.ShapeDtypeStruct(aq.shape, BF16),
        scratch_shapes=[pltpu.VMEM((lk, tqp), I32),
                        pltpu.VMEM((lk, tqp), I16),
                        pltpu.VMEM((lk, tqp), I16),
                        pltpu.VMEM((DSA_KV_HEADS, rows_g, LANES), F32),
                        pltpu.VMEM((DSA_KV_HEADS, rows_g, 2 * DSA_HEAD_DIM), F32)],
        compiler_params=_params("parallel", "arbitrary"),
        name="dsa",
    )(iq, misc, aq, ikcat, kb, vaug)


def _merge_kernel(x_ref, go_ref, do_ref, gate_ref, wa_ref, wb_ref, wo_ref, g_ref, y_ref):
    d = x_ref.shape[-1]
    a = _dot(go_ref[...], wa_ref[...])
    b = _dot(do_ref[...], wb_ref[...])
    gate = gate_ref[...].astype(F32)
    mixed = jax.nn.sigmoid(gate[:, :d]) * a + jax.nn.sigmoid(gate[:, d:]) * b
    y_ref[...] = x_ref[...] + _rmsnorm(_dot(mixed.astype(BF16), wo_ref[...]), g_ref[...])


def _merge_call(x, go, do, gate, wa, wb, wo, g, tm):
    t, d = x.shape
    row = lambda n: pl.BlockSpec((tm, n), lambda i: (i, 0))
    full = lambda a: pl.BlockSpec(a.shape, lambda i: (0, 0))
    return pl.pallas_call(
        _merge_kernel,
        grid=(t // tm,),
        in_specs=[row(d), row(go.shape[1]), row(do.shape[1]), row(gate.shape[1]),
                  full(wa), full(wb), full(wo), full(g)],
        out_specs=row(d),
        out_shape=jax.ShapeDtypeStruct((t, d), F32),
        compiler_params=_params("parallel"),
        name="merge",
    )(x, go, do, gate, wa, wb, wo, g)


FFN_ROW_CHUNK = 256
FFN_SLAB_ROWS = 64
GELU_C0 = float(np.sqrt(2.0 / np.pi))
GELU_C1 = 0.044715 * GELU_C0


def _ffn_kernel(x_ref, gpre_ref, wu_ref, cw_ref, cb_ref, pv_ref, wdh_ref, gpost_ref,
                y_ref, cn_ref, xn_ref, acc_ref, car_ref, act_ref, u_ref, *, seq, tn):
    i = pl.program_id(0)
    tm = x_ref.shape[0]
    f = wdh_ref.shape[0]
    xn_ref[...] = _rmsnorm(x_ref[...], gpre_ref[...]).astype(BF16)
    acc_ref[...] = jnp.zeros_like(acc_ref)
    first = (i % max(seq // tm, 1)) == 0
    top = lax.broadcasted_iota(I32, (SUBLANES, tn), 0)
    rc = min(tm, FFN_ROW_CHUNK)

    def up(cols, r):
        return _dot(xn_ref[r * rc:(r + 1) * rc, :], wu_ref[:, cols])

    def conv_slab(cols, u, halo):
        r1, r2 = pltpu.roll(u, 1, 0), pltpu.roll(u, 2, 0)
        top1 = jnp.where(top == 0, halo[1:2, :], r1[:SUBLANES, :])
        top2 = jnp.where(top == 0, halo[0:1, :], jnp.where(top == 1, halo[1:2, :], r2[:SUBLANES, :]))
        u1 = jnp.concatenate([top1, r1[SUBLANES:, :]], axis=0)
        u2 = jnp.concatenate([top2, r2[SUBLANES:, :]], axis=0)
        cw = cw_ref[:, cols]
        return cb_ref[:, cols] + (cw[0:1, :] * u2 + cw[1:2, :] * u1 + cw[2:3, :] * u), u[u.shape[0] - 2:, :]

    period = min(seq, tm)
    sl = min(rc, FFN_SLAB_ROWS, period)

    def keep_tail(cols, slot, half, end, k):
        cn_ref[k, :, cols] = u_ref[slot, half, end - (CONV_W - 1):end, :]
        if period == tm:
            car_ref[:, cols] = u_ref[slot, half, end - SUBLANES:end, :]

    def start_halo(cols, k):
        if period == tm:
            return jnp.where(first, pv_ref[0, :, cols], car_ref[SUBLANES - 2:SUBLANES, cols])
        return pv_ref[k, :, cols]

    nft = f // tn
    nrc = tm // rc

    def tile(j, aslot, with_down):
        cols_a = pl.ds(pl.multiple_of(j * tn, tn), tn)
        cols_b = pl.ds(pl.multiple_of(f + j * tn, tn), tn)
        halo_a = halo_b = None

        def up_to(slot, r):
            u_ref[slot, 0] = up(cols_a, r)
            u_ref[slot, 1] = up(cols_b, r)

        up_to(0, 0)
        for r in range(nrc):
            rows = slice(r * rc, (r + 1) * rc)
            slot = r % 2
            if r + 1 < nrc:
                up_to(1 - slot, r + 1)
            if with_down:
                project_down(j - 1, 1 - aslot, rows)
            for s in range(0, rc, sl):
                row0 = r * rc + s
                if row0 % period == 0:
                    halo_a, halo_b = start_halo(cols_a, row0 // period), start_halo(cols_b, row0 // period)
                a, halo_a = conv_slab(cols_a, u_ref[slot, 0, s:s + sl, :], halo_a)
                b, halo_b = conv_slab(cols_b, u_ref[slot, 1, s:s + sl, :], halo_b)
                act = (a * (1.0 + jnp.tanh(a * (GELU_C0 + GELU_C1 * (a * a))))) * b
                act_ref[aslot, row0:row0 + sl, :] = act.astype(BF16)
                if (row0 + sl) % period == 0:
                    keep_tail(cols_a, slot, 0, s + sl, row0 // period)
                    keep_tail(cols_b, slot, 1, s + sl, row0 // period)

    def project_down(j, aslot, rows):
        acc_ref[rows, :] += _dot(act_ref[aslot, rows, :], wdh_ref[pl.ds(pl.multiple_of(j * tn, tn), tn), :])

    assert nft % 2 == 1
    tile(0, 0, False)

    def trip(p, carry):
        tile(2 * p + 1, 1, True)
        tile(2 * p + 2, 0, True)
        return carry

    lax.fori_loop(0, nft // 2, trip, 0)
    for r in range(nrc):
        project_down(nft - 1, 0, slice(r * rc, (r + 1) * rc))
    y_ref[...] = x_ref[...] + _rmsnorm(acc_ref[...], gpost_ref[...])


def _ffn_call(x, gpre, w_up, conv_w, conv_b, conv_prev, w_down_half, gpost, *, tm, seq, tn):
    t, d = x.shape
    f = w_down_half.shape[0]
    nseq = t // seq
    assert f % tn == 0 and (seq % tm == 0 or tm % seq == 0) and t % tm == 0
    bps = max(seq // tm, 1)
    spb = max(tm // seq, 1)
    xrow = pl.BlockSpec((tm, d), lambda i: (i, 0))
    const = lambda a: pl.BlockSpec(a.shape, lambda i: (0, 0), pipeline_mode=pl.Buffered(1))
    state = lambda step: pl.BlockSpec((spb, CONV_W - 1, 2 * f), lambda i: (step(i), 0, 0))
    y, cn = pl.pallas_call(
        functools.partial(_ffn_kernel, seq=seq, tn=tn),
        grid=(t // tm,),
        in_specs=[xrow, const(gpre), const(w_up), const(conv_w), const(conv_b),
                  state(lambda i: i // bps), const(w_down_half), const(gpost)],
        out_specs=[xrow, state(lambda i: i)],
        out_shape=[jax.ShapeDtypeStruct((t, d), F32),
                   jax.ShapeDtypeStruct((t // tm * spb, CONV_W - 1, 2 * f), F32)],
        scratch_shapes=[pltpu.VMEM((tm, d), BF16), pltpu.VMEM((tm, d), F32),
                        pltpu.VMEM((SUBLANES, 2 * f), F32), pltpu.VMEM((2, tm, tn), BF16),
                        pltpu.VMEM((2, 2, min(tm, FFN_ROW_CHUNK), tn), F32)],
        compiler_params=_params("arbitrary"),
        name="ffn",
    )(x, gpre, w_up, conv_w, conv_b, conv_prev, w_down_half, gpost)
    return y, cn.reshape(nseq, bps, CONV_W - 1, 2 * f)[:, bps - 1]


def _prep_weights(w_in, w_gate_up, d_model):
    sizes = (GLA_HEADS * GLA_DK, GLA_HEADS * GLA_DK, GLA_HEADS * GLA_DV, GLA_RANK, GLA_HEADS * GLA_DV,
             DSA_HEADS * DSA_HEAD_DIM, DSA_KV_HEADS * DSA_HEAD_DIM, DSA_KV_HEADS * DSA_HEAD_DIM,
             IDX_HEADS * IDX_DIM, IDX_DIM, IDX_HEADS, 2 * d_model)
    offs = np.concatenate([[0], np.cumsum(sizes)])
    g_q, g_k, g_v, g_low, g_r, a_q, a_k, a_v, i_q, i_k, i_w, gate = (
        w_in[:, int(offs[n]):int(offs[n + 1])] for n in range(len(sizes)))
    w_main = jnp.concatenate([g_q, g_k, g_v, g_r, a_q, a_k, a_v, gate], axis=1).astype(BF16)
    zeros = lambda n: jnp.zeros((w_in.shape[0], n), w_in.dtype)
    misc = jnp.concatenate([g_low, i_w, zeros(LANES - GLA_RANK - IDX_HEADS)], axis=1)
    assert MISC_GLOW == 0 and MISC_IW == GLA_RANK
    w_hp = jnp.concatenate([i_q, i_k, zeros(LANES - IDX_DIM), misc], axis=1)
    assert w_hp.shape[1] == HP_WIDTH
    w_hp_hi, w_hp_lo = _split_bf16(w_hp)
    wup = jnp.concatenate([w_gate_up, jnp.zeros((LANES - GLA_RANK, w_gate_up.shape[1]), w_gate_up.dtype)], axis=0)
    wup_hi, wup_lo = _split_bf16(wup)
    return w_main, w_hp_hi, w_hp_lo, jnp.concatenate([wup_hi, wup_hi, wup_lo], axis=0)


def _run_layer(x, k_past, v_past, kidx_past, gla_s0, conv_prev, w):
    bsz, seq, d = x.shape
    past = k_past.shape[1]
    t = bsz * seq
    tm, gla_tb, dsa_tq, ffn_tm = min(t, PROJ_ROWS), min(seq, GLA_ROWS), min(seq, DSA_QUERIES), min(t, FFN_ROWS)
    x2 = x.reshape(t, d)
    gq, gk, gv, gr, aq, ak, av, gate, iq, ik, misc, ikcat, kb, vaug = _proj_call(
        x2, w["attn_pre_norm"], w["w_main"], w["w_hp_hi"], w["w_hp_lo"], tm)
    per_seq = lambda a: a.reshape(bsz, seq, a.shape[-1])

    gla_o, gla_s = _gla_call(per_seq(gq), per_seq(gk), per_seq(gv), per_seq(gr), per_seq(misc),
                             w["wup"], w["b_gla_gate"], w["gla_head_norm"], gla_s0, gla_tb)

    n_keys = past + seq
    key_span = KEY_UNROLL * KEY_CHUNK
    lk = -(-n_keys // key_span) * key_span
    packed = [per_seq(a) for a in (ikcat, kb, vaug)]
    if past:
        flat = lambda a: a.reshape((bsz * past,) + a.shape[2:]).astype(F32)
        packed_past = _pack_call(flat(kidx_past), flat(k_past), flat(v_past), min(bsz * past, PACK_ROWS))
        packed = [jnp.concatenate([p.reshape(bsz, past, -1), n], axis=1) for p, n in zip(packed_past, packed)]
    if lk > n_keys:
        packed = [jnp.pad(a, ((0, 0), (0, lk - n_keys), (0, 0))) for a in packed]

    dsa_o = _dsa_call(per_seq(iq), per_seq(misc), per_seq(aq), *packed,
                      q_start=past, n_keys=n_keys, tq=dsa_tq)

    x1 = _merge_call(x2, gla_o.reshape(t, -1), dsa_o.reshape(t, -1), gate,
                     w["w_branch_gla"], w["w_branch_dsa"], w["w_out"], w["attn_post_norm"], min(t, MERGE_ROWS))
    y, conv_new = _ffn_call(x1, w["ffn_pre_norm"], w["w_up"], w["conv_w"], w["conv_b"], conv_prev,
                            w["w_down_half"], w["ffn_post_norm"], tm=ffn_tm, seq=seq, tn=FFN_TILE)
    k_new = ak.reshape(bsz, seq, DSA_KV_HEADS, DSA_HEAD_DIM)
    v_new = av.reshape(bsz, seq, DSA_KV_HEADS, DSA_HEAD_DIM)
    return y.reshape(bsz, seq, d), k_new, v_new, per_seq(ik), gla_s, conv_new


def kernel(x_prompt, x_sample, cache_k, cache_v, cache_k_idx, state_gla, state_ffn_conv, attn_pre_norm, w_in, w_gla_gate_up, b_gla_gate, gla_head_norm, w_branch_gla, w_branch_dsa, w_out, attn_post_norm, ffn_pre_norm, w_up, conv_w, conv_b, w_down, ffn_post_norm):
    depth = w_in.shape[0]
    bsz, _, d = x_prompt.shape
    dt = x_prompt.dtype
    y_prompt, y_sample = x_prompt, x_sample
    prompt_new, sample_new = [], []
    row = lambda a: a.reshape(1, -1)
    for l in range(depth):
        w_main, w_hp_hi, w_hp_lo, wup = _prep_weights(w_in[l], w_gla_gate_up[l], d)
        w = dict(attn_pre_norm=row(attn_pre_norm[l]), w_main=w_main, w_hp_hi=w_hp_hi, w_hp_lo=w_hp_lo, wup=wup,
                 b_gla_gate=row(b_gla_gate[l]), gla_head_norm=row(gla_head_norm[l]),
                 w_branch_gla=w_branch_gla[l].astype(BF16), w_branch_dsa=w_branch_dsa[l].astype(BF16),
                 w_out=w_out[l].astype(BF16), attn_post_norm=row(attn_post_norm[l]),
                 ffn_pre_norm=row(ffn_pre_norm[l]), w_up=w_up[l].astype(BF16), conv_w=conv_w[l],
                 conv_b=row(conv_b[l]), w_down_half=(0.5 * w_down[l]).astype(BF16),
                 ffn_post_norm=row(ffn_post_norm[l]))
        ffn2 = conv_w.shape[-1]
        y_prompt, *new = _run_layer(
            y_prompt,
            jnp.zeros((bsz, 0, DSA_KV_HEADS * DSA_HEAD_DIM), dt),
            jnp.zeros((bsz, 0, DSA_KV_HEADS * DSA_HEAD_DIM), dt),
            jnp.zeros((bsz, 0, IDX_DIM), dt),
            jnp.zeros((bsz, GLA_HEADS, GLA_DK, GLA_DV), F32),
            jnp.zeros((bsz, CONV_W - 1, ffn2), dt),
            w)
        prompt_new.append(new)
        y_sample, *new = _run_layer(
            y_sample, cache_k[l], cache_v[l], cache_k_idx[l], state_gla[l], state_ffn_conv[l],
            w)
        sample_new.append(new)
    k_p, v_p, kidx_p, gla_p, conv_p = (jnp.stack(t, axis=0) for t in zip(*prompt_new))
    k_s, v_s, kidx_s, gla_s, conv_s = (jnp.stack(t, axis=0) for t in zip(*sample_new))
    return (y_prompt, y_sample, k_p, v_p, kidx_p, gla_p, conv_p, k_s, v_s, kidx_s, gla_s, conv_s)
```

```python
import functools

import numpy as np
import jax
import jax.numpy as jnp
from jax import lax
from jax.experimental import pallas as pl
from jax.experimental.pallas import tpu as pltpu

F32 = jnp.float32
BF16 = jnp.bfloat16
I32 = jnp.int32
I16 = jnp.int16
HIGHEST = lax.Precision.HIGHEST

CHUNK = 64
GLA_HEADS = 4
GLA_DK = 128
GLA_DV = 256
GLA_RANK = 16
GLA_GATE_TAU = 16.0
DSA_HEADS = 8
DSA_KV_HEADS = 2
DSA_HEAD_DIM = 64
IDX_HEADS = 4
IDX_DIM = 64
IDX_SCALE = (IDX_HEADS * IDX_DIM) ** -0.5
TOPK_MAX = 256
CONV_W = 3
EPS = 1e-6

LANES = 128
SUBLANES = 8
VMEM_LIMIT = 56 * 1024 * 1024

MISC_GLOW = 0
MISC_IW = 16

KEY_CHUNK = 256
KEY_UNROLL = 2
GLA_UNROLL = 8
INT16_MIN = -(2 ** 15)
LOG2_E = float(np.log2(np.e))
ATTN_Q_SCALE = DSA_HEAD_DIM ** -0.5 * LOG2_E
MASKED = -1e30
NT_DIMS = (((1,), (1,)), ((), ()))
TN_DIMS = (((0,), (0,)), ((), ()))


def _dot(a, b, **kw):
    return jnp.dot(a, b, preferred_element_type=F32, **kw)


def _dot_nt(a, b):
    return lax.dot_general(a, b, NT_DIMS, preferred_element_type=F32)


def _split_bf16(x):
    hi = x.astype(BF16)
    lo = (x - hi.astype(F32)).astype(BF16)
    return hi, lo


def _rmsnorm(x, g):
    return x * lax.rsqrt(jnp.mean(x * x, axis=-1, keepdims=True) + EPS) * g


def _params(*sem):
    return pltpu.CompilerParams(dimension_semantics=sem, vmem_limit_bytes=VMEM_LIMIT)


MAIN_SIZES = (GLA_HEADS * GLA_DK, GLA_HEADS * GLA_DK, GLA_HEADS * GLA_DV, GLA_HEADS * GLA_DV,
              DSA_HEADS * DSA_HEAD_DIM, DSA_KV_HEADS * DSA_HEAD_DIM, DSA_KV_HEADS * DSA_HEAD_DIM)
HP_WIDTH = 512
PROJ_ROWS = 256
PACK_ROWS = 1024
GLA_ROWS = 512
DSA_QUERIES = 256
MERGE_ROWS = 512
FFN_ROWS = 1024
FFN_TILE = 256
PACK_WIDTHS = (3 * IDX_DIM, DSA_KV_HEADS * DSA_HEAD_DIM, 2 * DSA_KV_HEADS * DSA_HEAD_DIM)


def _pack_keys(ik, k, v):
    hi, lo = _split_bf16(ik)
    ikcat = jnp.concatenate([hi, hi, lo], axis=1)
    vb = v.astype(BF16)
    ones = jnp.ones((v.shape[0], DSA_HEAD_DIM), BF16)
    parts = []
    for g in range(DSA_KV_HEADS):
        parts += [vb[:, g * DSA_HEAD_DIM:(g + 1) * DSA_HEAD_DIM], ones]
    return ikcat, k.astype(BF16), jnp.concatenate(parts, axis=1)


def _store_heads(o_ref, x):
    dim = o_ref.shape[-1]
    for g in range(o_ref.shape[1]):
        o_ref[:, g, :] = x[:, g * dim:(g + 1) * dim].astype(o_ref.dtype)


def _load_heads(x_ref):
    return jnp.concatenate([x_ref[:, g, :] for g in range(x_ref.shape[1])], axis=1)


def _proj_kernel(x_ref, g_ref, wm_ref, whi_ref, wlo_ref,
                 gq_ref, gk_ref, gv_ref, gr_ref, aq_ref, ak_ref, av_ref, gate_ref, iq_ref, ik_ref, misc_ref,
                 ikcat_ref, kb_ref, vaug_ref):
    xn = _rmsnorm(x_ref[...], g_ref[...])
    xh, xl = _split_bf16(xn)
    outs = (gq_ref, gk_ref, gv_ref, gr_ref, aq_ref, ak_ref, av_ref, gate_ref)
    off = 0
    vals = []
    for o_ref, n in zip(outs, MAIN_SIZES + (gate_ref.shape[-1],)):
        vals.append(_dot(xh, wm_ref[:, off:off + n]))
        if o_ref is aq_ref:
            vals[-1] = vals[-1] * ATTN_Q_SCALE
        if o_ref.ndim == 2:
            o_ref[...] = vals[-1].astype(o_ref.dtype)
        else:
            _store_heads(o_ref, vals[-1])
        off += n
    ak, av = vals[5], vals[6]
    whi = whi_ref[...]
    hp = _dot(xh, whi) + _dot(xl, whi) + _dot(xh, wlo_ref[...])
    iq_ref[...] = hp[:, 0:256]
    ik_ref[...] = hp[:, 256:320]
    misc_ref[...] = hp[:, 384:512]
    ikcat_ref[...], kb_ref[...], vaug_ref[...] = _pack_keys(hp[:, 256:320], ak, av)


def _proj_call(x, g, wm, whi, wlo, tm):
    t, d = x.shape
    gate_w = wm.shape[1] - sum(MAIN_SIZES)
    widths = MAIN_SIZES + (gate_w,)
    dtypes = (BF16, BF16, BF16, BF16, BF16, F32, F32, BF16)
    out_shape = [jax.ShapeDtypeStruct((t, n), dt) for n, dt in zip(widths, dtypes)]
    for n in (5, 6):
        out_shape[n] = jax.ShapeDtypeStruct((t, DSA_KV_HEADS, DSA_HEAD_DIM), F32)
    out_shape += [jax.ShapeDtypeStruct((t, n), F32) for n in (256, 64, 128)]
    out_shape += [jax.ShapeDtypeStruct((t, n), BF16) for n in PACK_WIDTHS]
    row = lambda n: pl.BlockSpec((tm, n), lambda i: (i, 0))
    rows = lambda s: pl.BlockSpec((tm,) + s.shape[1:], lambda i: (i,) + (0,) * (len(s.shape) - 1))
    full = lambda a: pl.BlockSpec(a.shape, lambda i: (0, 0))
    return pl.pallas_call(
        _proj_kernel,
        grid=(t // tm,),
        in_specs=[row(d), full(g), full(wm), full(whi), full(wlo)],
        out_specs=[rows(s) for s in out_shape],
        out_shape=out_shape,
        compiler_params=_params("parallel"),
        name="proj",
    )(x, g, wm, whi, wlo)


def _pack_kernel(ik_ref, k_ref, v_ref, ikcat_ref, kb_ref, vaug_ref):
    ikcat_ref[...], kb_ref[...], vaug_ref[...] = _pack_keys(ik_ref[...], _load_heads(k_ref), _load_heads(v_ref))


def _pack_call(ik, k, v, tm):
    t = ik.shape[0]
    row = lambda n: pl.BlockSpec((tm, n), lambda i: (i, 0))
    heads = pl.BlockSpec((tm,) + k.shape[1:], lambda i: (i, 0, 0))
    return pl.pallas_call(
        _pack_kernel,
        grid=(t // tm,),
        in_specs=[row(ik.shape[1]), heads, heads],
        out_specs=[row(n) for n in PACK_WIDTHS],
        out_shape=[jax.ShapeDtypeStruct((t, n), BF16) for n in PACK_WIDTHS],
        compiler_params=_params("parallel"),
        name="pack",
    )(ik, k, v)


def _gla_kernel(gq_ref, gk_ref, gv_ref, gr_ref, misc_ref, wup_ref, bup_ref, hn_ref, s0_ref,
                o_ref, sfin_ref, st_ref, qd_ref, ki_ref, kr_ref, dl_ref, *, nchunks):
    j = pl.program_id(1)

    @pl.when(j == 0)
    def _():
        for h in range(GLA_HEADS):
            st_ref[h] = s0_ref[h].T

    row = lax.broadcasted_iota(I32, (CHUNK, CHUNK), 0)
    col = lax.broadcasted_iota(I32, (CHUNK, CHUNK), 1)
    causal = row >= col
    tri = causal.astype(BF16)

    m_hi, m_lo = _split_bf16(misc_ref[...])
    z = _dot(jnp.concatenate([m_hi, m_lo, m_hi], axis=1), wup_ref[...]) + bup_ref[...]
    log_a = (jnp.minimum(z, 0.0) - jnp.log(1.0 + jnp.exp(-jnp.abs(z)))) * (LOG2_E / GLA_GATE_TAU)
    la_hi, la_lo = _split_bf16(log_a)
    tri2 = jnp.concatenate([tri, tri], axis=1)
    for c in range(nchunks):
        rows = slice(c * CHUNK, (c + 1) * CHUNK)
        b = _dot(tri2, jnp.concatenate([la_hi[rows, :], la_lo[rows, :]], axis=0))
        b_last = b[CHUNK - 1:CHUNK, :]
        q = gq_ref[rows, :].astype(F32)
        k = gk_ref[rows, :].astype(F32)
        qd_ref[rows, :] = (q * (GLA_DK ** -0.5) * jnp.exp2(b)).astype(BF16)
        ki_ref[rows, :] = (k * jnp.exp2(-b)).astype(BF16)
        kr_ref[rows, :] = (k * jnp.exp2(b_last - b)).astype(BF16)
        dl_ref[c] = jnp.broadcast_to(jnp.exp2(b_last), dl_ref.shape[1:])

    def chunk(c, carry):
        rows = pl.ds(pl.multiple_of(c * CHUNK, CHUNK), CHUNK)
        q_dec, k_inv, k_rem = qd_ref[rows, :], ki_ref[rows, :], kr_ref[rows, :]
        d_last = dl_ref[c][0:1, :]
        v = gv_ref[rows, :]
        gr = gr_ref[rows, :].astype(F32)
        hn = hn_ref[...]
        for h in range(GLA_HEADS):
            ks = slice(h * GLA_DK, (h + 1) * GLA_DK)
            vs = slice(h * GLA_DV, (h + 1) * GLA_DV)
            qh, vh = q_dec[:, ks], v[:, vs]
            scores = jnp.where(causal, _dot_nt(qh, k_inv[:, ks]), 0.0)
            st = st_ref[h]
            o = _dot(scores.astype(BF16), vh) + _dot_nt(qh, st.astype(BF16))
            ut = lax.dot_general(vh, k_rem[:, ks], TN_DIMS, preferred_element_type=F32)
            st_ref[h] = st * d_last[:, ks] + ut
            y = _rmsnorm(o, hn)
            g = gr[:, vs]
            o_ref[rows, vs] = (y * (g * jax.nn.sigmoid(g))).astype(o_ref.dtype)
        return carry

    lax.fori_loop(0, nchunks, chunk, 0, unroll=min(nchunks, GLA_UNROLL))

    @pl.when(j == pl.num_programs(1) - 1)
    def _():
        for h in range(GLA_HEADS):
            sfin_ref[h] = st_ref[h].T


def _gla_call(gq, gk, gv, gr, misc, wup, bup, hn, s0, tb):
    bsz, seq, _ = gq.shape
    blk = lambda n: pl.BlockSpec((None, tb, n), lambda b, j: (b, j, 0))
    full2 = lambda a: pl.BlockSpec(a.shape, lambda b, j: (0, 0))
    st_spec = pl.BlockSpec((None, GLA_HEADS, GLA_DK, GLA_DV), lambda b, j: (b, 0, 0, 0))
    return pl.pallas_call(
        functools.partial(_gla_kernel, nchunks=tb // CHUNK),
        grid=(bsz, seq // tb),
        in_specs=[blk(gq.shape[2]), blk(gk.shape[2]), blk(gv.shape[2]), blk(gr.shape[2]), blk(misc.shape[2]),
                  full2(wup), full2(bup), full2(hn), st_spec],
        out_specs=[blk(gv.shape[2]), st_spec],
        out_shape=[jax.ShapeDtypeStruct((bsz, seq, gv.shape[2]), BF16),
                   jax.ShapeDtypeStruct(s0.shape, F32)],
        scratch_shapes=[pltpu.VMEM((GLA_HEADS, GLA_DV, GLA_DK), F32),
                        pltpu.VMEM((tb, gq.shape[2]), BF16), pltpu.VMEM((tb, gq.shape[2]), BF16),
                        pltpu.VMEM((tb, gq.shape[2]), BF16),
                        pltpu.VMEM((tb // CHUNK, SUBLANES, gq.shape[2]), F32)],
        compiler_params=_params("parallel", "arbitrary"),
        name="gla",
    )(gq, gk, gv, gr, misc, wup, bup, hn, s0)


def _dsa_kernel(iq_ref, misc_ref, aq_ref, ikcat_ref, kb_ref, vaug_ref, o_ref,
                keys_ref, hi_ref, lo_ref, m_ref, acc_ref,
                *, tq, tqp, q_start, n_keys, topk, single_block):
    qi = 0 if single_block else pl.program_id(1)
    kc = KEY_CHUNK
    hd = DSA_HEAD_DIM
    hpg = DSA_HEADS // DSA_KV_HEADS

    first_q = q_start + qi * tq
    last_q = first_q + tq - 1
    span = KEY_UNROLL * kc
    lowest = min if single_block else jnp.minimum
    nt = (lowest(n_keys, ((last_q >> 6) + 1) * CHUNK) + span - 1) // span
    nfull = lowest(n_keys, ((first_q >> 6) + 1) * CHUNK) // span

    def loop(lo, hi, trip, carry):
        if not single_block:
            return lax.fori_loop(lo, hi, trip, carry)
        for t in range(lo, hi):
            carry = trip(t, carry)
        return carry

    def trips(body, masked):
        def trip(t, carry):
            for u in range(KEY_UNROLL):
                carry = body(pl.multiple_of((t * KEY_UNROLL + u) * kc, kc), masked, carry)
            return carry
        return trip

    def for_chunks(body, init):
        return loop(nfull, nt, trips(body, True), loop(0, nfull, trips(body, False), init))

    def for_all_chunks(body, init):
        return loop(0, nt, trips(lambda r0, _, carry: body(r0, carry), None), init)

    def pad_rows(x):
        if tqp == tq:
            return x
        return jnp.concatenate([x, jnp.zeros((tqp - tq, x.shape[1]), x.dtype)], axis=0)

    iq_hi, iq_lo = _split_bf16(pad_rows(iq_ref[...]))
    iq_cat = []
    for h in range(IDX_HEADS):
        s = slice(h * IDX_DIM, (h + 1) * IDX_DIM)
        iq_cat.append(jnp.concatenate([iq_hi[:, s], iq_lo[:, s], iq_hi[:, s]], axis=1))
    wts = pad_rows(misc_ref[...]).T[MISC_IW:MISC_IW + IDX_HEADS, :] * IDX_SCALE
    q_chunk = (first_q + lax.broadcasted_iota(I32, (1, tqp), 1)) >> 6

    def admissible(r0):
        k_pos = r0 + lax.broadcasted_iota(I32, (kc, tqp), 0)
        return ((k_pos >> 6) <= q_chunk) & (k_pos < n_keys)

    def score_body(r0, masked, carry):
        rows = pl.ds(r0, kc)
        ik = ikcat_ref[rows, :]
        acc = jnp.zeros((kc, tqp), F32)
        for h in range(IDX_HEADS):
            acc = acc + wts[h:h + 1, :] * jnp.maximum(_dot_nt(ik, iq_cat[h]), 0.0)
        if masked:
            acc = jnp.where(admissible(r0), acc, -jnp.inf)
        bits = lax.bitcast_convert_type(acc, I32)
        key = jnp.where(bits < 0, bits ^ jnp.int32(0x7FFFFFFF), bits)
        keys_ref[rows, :] = key
        hi_ref[rows, :] = (key >> 16).astype(I16)
        lo_ref[rows, :] = ((key & 0xFFFF) + INT16_MIN).astype(I16)
        return carry

    for_chunks(score_body, 0)

    grp = 32
    one, zero = jnp.asarray(1, BF16), jnp.asarray(0, BF16)

    def count16(ref, pred):
        def body(r0, acc):
            m = jnp.where(pred(ref[pl.ds(r0, kc), :]), one, zero)
            for i in range(kc // grp):
                acc = acc + m[i * grp:(i + 1) * grp, :]
            return acc
        acc = for_all_chunks(body, jnp.zeros((grp, tqp), BF16))
        return jnp.sum(acc.astype(F32), axis=0, keepdims=True)

    def search16(ref, kth):
        def bit_body(i, carry):
            prefix, above = carry
            cand = prefix | (jnp.int32(1) << (15 - i))
            cand16 = (cand + INT16_MIN).astype(I16)
            tot = count16(ref, lambda ch: ch >= cand16)
            keep = tot >= kth
            return jnp.where(keep, cand, prefix), jnp.where(keep, above, tot)
        return lax.fori_loop(0, 16, bit_body, (jnp.zeros((1, tqp), I32), jnp.zeros((1, tqp), F32)))

    kth = jnp.full((1, tqp), topk, F32)
    a_u, above_a = search16(hi_ref, kth)
    a16 = (a_u + INT16_MIN).astype(I16)
    kth_lo = kth - above_a

    def bucket_body(r0, carry):
        rows = pl.ds(r0, kc)
        lo_ref[rows, :] = jnp.where(hi_ref[rows, :] == a16, lo_ref[rows, :], jnp.asarray(INT16_MIN, I16))
        return carry

    for_all_chunks(bucket_body, 0)
    b_u, above_b = search16(lo_ref, kth_lo)
    thr = ((a_u + INT16_MIN) << 16) | b_u
    need = kth_lo - above_b

    tri = (lax.broadcasted_iota(I32, (kc, kc), 0) >= lax.broadcasted_iota(I32, (kc, kc), 1)).astype(BF16)

    def sel_bias(r0, masked, running):
        ch = keys_ref[pl.ds(r0, kc), :]
        eq = ch == thr
        rank = _dot(tri, jnp.where(eq, 1.0, 0.0).astype(BF16)) + running
        sel = (ch > thr) | (eq & (rank <= need))
        if masked:
            sel = sel & admissible(r0)
        return jnp.where(sel, 0.0, MASKED).T[:tq, :], rank[kc - 1:kc, :]

    m_ref[...] = jnp.full(m_ref.shape, MASKED, F32)
    acc_ref[...] = jnp.zeros(acc_ref.shape, F32)
    q_groups = []
    for g in range(DSA_KV_HEADS):
        qg = jnp.concatenate([aq_ref[:, h * hd:(h + 1) * hd] for h in range(g * hpg, (g + 1) * hpg)], axis=0)
        q_groups.append(qg)

    def attend(t, masked, running):
        r0 = pl.multiple_of(t * span, span)
        cols = pl.ds(r0, span)
        biases = []
        for u in range(KEY_UNROLL):
            bias, running = sel_bias(pl.multiple_of(r0 + u * kc, kc), masked, running)
            biases.append(bias)
        bias = jnp.concatenate(biases, axis=1)
        for g in range(DSA_KV_HEADS):
            s = _dot_nt(q_groups[g], kb_ref[cols, g * hd:(g + 1) * hd])
            s = jnp.concatenate([s[r * tq:(r + 1) * tq, :] + bias for r in range(hpg)], axis=0)
            tiles = [s[:, i * LANES:(i + 1) * LANES] for i in range(span // LANES)]
            m_old = m_ref[g]
            m_new = jnp.maximum(m_old, jnp.max(functools.reduce(jnp.maximum, tiles), axis=1, keepdims=True))
            p = jnp.concatenate([jnp.exp2(t - m_new) for t in tiles], axis=1)
            pv = _dot(p.astype(BF16), vaug_ref[cols, 2 * g * hd:2 * (g + 1) * hd])
            acc_ref[g] = acc_ref[g] * jnp.exp2(m_old - m_new) + pv
            m_ref[g] = m_new
        return running

    no_ties = jnp.zeros((1, tqp), F32)
    loop(nfull, nt, lambda t, c: attend(t, True, c), loop(0, nfull, lambda t, c: attend(t, False, c), no_ties))

    for g in range(DSA_KV_HEADS):
        acc = acc_ref[g]
        out = acc[:, :hd] / acc[:, hd:hd + 1]
        for r in range(hpg):
            h = g * hpg + r
            o_ref[:, h * hd:(h + 1) * hd] = out[r * tq:(r + 1) * tq, :].astype(o_ref.dtype)


def _dsa_call(iq, misc, aq, ikcat, kb, vaug, *, q_start, n_keys, tq):
    bsz, lq, _ = iq.shape
    lk = ikcat.shape[1]
    assert lk % (KEY_UNROLL * KEY_CHUNK) == 0 and lq % tq == 0 and lk // 32 <= 256
    tqp = max(tq, LANES)
    topk = min(TOPK_MAX, n_keys // 4)
    rows_g = (DSA_HEADS // DSA_KV_HEADS) * tq
    qblk = lambda n: pl.BlockSpec((None, tq, n), lambda b, q: (b, q, 0))
    kblk = lambda n: pl.BlockSpec((None, lk, n), lambda b, q: (b, 0, 0))
    kern = functools.partial(_dsa_kernel, tq=tq, tqp=tqp, q_start=q_start, n_keys=n_keys, topk=topk,
                             single_block=(lq == tq))
    return pl.pallas_call(
        kern,
        grid=(bsz, lq // tq),
        in_specs=[qblk(iq.shape[2]), qblk(misc.shape[2]), qblk(aq.shape[2]),
                  kblk(ikcat.shape[2]), kblk(kb.shape[2]), kblk(vaug.shape[2])],
        out_specs=qblk(aq.shape[2]),
        out_shape=jax.ShapeDtypeStruct(aq.shape, BF16),
        scratch_shapes=[pltpu.VMEM((lk, tqp), I32),
                        pltpu.VMEM((lk, tqp), I16),
                        pltpu.VMEM((lk, tqp), I16),
                        pltpu.VMEM((DSA_KV_HEADS, rows_g, LANES), F32),
                        pltpu.VMEM((DSA_KV_HEADS, rows_g, 2 * DSA_HEAD_DIM), F32)],
        compiler_params=_params("parallel", "arbitrary"),
        name="dsa",
    )(iq, misc, aq, ikcat, kb, vaug)


def _merge_kernel(x_ref, go_ref, do_ref, gate_ref, wa_ref, wb_ref, wo_ref, g_ref, y_ref):
    d = x_ref.shape[-1]
    a = _dot(go_ref[...], wa_ref[...])
    b = _dot(do_ref[...], wb_ref[...])
    gate = gate_ref[...].astype(F32)
    mixed = jax.nn.sigmoid(gate[:, :d]) * a + jax.nn.sigmoid(gate[:, d:]) * b
    y_ref[...] = x_ref[...] + _rmsnorm(_dot(mixed.astype(BF16), wo_ref[...]), g_ref[...])


def _merge_call(x, go, do, gate, wa, wb, wo, g, tm):
    t, d = x.shape
    row = lambda n: pl.BlockSpec((tm, n), lambda i: (i, 0))
    full = lambda a: pl.BlockSpec(a.shape, lambda i: (0, 0))
    return pl.pallas_call(
        _merge_kernel,
        grid=(t // tm,),
        in_specs=[row(d), row(go.shape[1]), row(do.shape[1]), row(gate.shape[1]),
                  full(wa), full(wb), full(wo), full(g)],
        out_specs=row(d),
        out_shape=jax.ShapeDtypeStruct((t, d), F32),
        compiler_params=_params("parallel"),
        name="merge",
    )(x, go, do, gate, wa, wb, wo, g)


FFN_ROW_CHUNK = 256
FFN_SLAB_ROWS = 64
GELU_C0 = float(np.sqrt(2.0 / np.pi))
GELU_C1 = 0.044715 * GELU_C0


def _ffn_kernel(x_ref, gpre_ref, wu_ref, cw_ref, cb_ref, pv_ref, wdh_ref, gpost_ref,
                y_ref, cn_ref, xn_ref, acc_ref, car_ref, act_ref, u_ref, *, seq, tn):
    i = pl.program_id(0)
    tm = x_ref.shape[0]
    f = wdh_ref.shape[0]
    xn_ref[...] = _rmsnorm(x_ref[...], gpre_ref[...]).astype(BF16)
    acc_ref[...] = jnp.zeros_like(acc_ref)
    first = (i % max(seq // tm, 1)) == 0
    top = lax.broadcasted_iota(I32, (SUBLANES, tn), 0)
    rc = min(tm, FFN_ROW_CHUNK)

    def up(cols, r):
        return _dot(xn_ref[r * rc:(r + 1) * rc, :], wu_ref[:, cols])

    def conv_slab(cols, u, halo):
        r1, r2 = pltpu.roll(u, 1, 0), pltpu.roll(u, 2, 0)
        top1 = jnp.where(top == 0, halo[1:2, :], r1[:SUBLANES, :])
        top2 = jnp.where(top == 0, halo[0:1, :], jnp.where(top == 1, halo[1:2, :], r2[:SUBLANES, :]))
        u1 = jnp.concatenate([top1, r1[SUBLANES:, :]], axis=0)
        u2 = jnp.concatenate([top2, r2[SUBLANES:, :]], axis=0)
        cw = cw_ref[:, cols]
        return cb_ref[:, cols] + (cw[0:1, :] * u2 + cw[1:2, :] * u1 + cw[2:3, :] * u), u[u.shape[0] - 2:, :]

    period = min(seq, tm)
    sl = min(rc, FFN_SLAB_ROWS, period)

    def keep_tail(cols, slot, half, end, k):
        cn_ref[k, :, cols] = u_ref[slot, half, end - (CONV_W - 1):end, :]
        if period == tm:
            car_ref[:, cols] = u_ref[slot, half, end - SUBLANES:end, :]

    def start_halo(cols, k):
        if period == tm:
            return jnp.where(first, pv_ref[0, :, cols], car_ref[SUBLANES - 2:SUBLANES, cols])
        return pv_ref[k, :, cols]

    nft = f // tn
    nrc = tm // rc

    def tile(j, aslot, with_down):
        cols_a = pl.ds(pl.multiple_of(j * tn, tn), tn)
        cols_b = pl.ds(pl.multiple_of(f + j * tn, tn), tn)
        halo_a = halo_b = None

        def up_to(slot, r):
            u_ref[slot, 0] = up(cols_a, r)
            u_ref[slot, 1] = up(cols_b, r)

        up_to(0, 0)
        for r in range(nrc):
            rows = slice(r * rc, (r + 1) * rc)
            slot = r % 2
            if r + 1 < nrc:
                up_to(1 - slot, r + 1)
            if with_down:
                project_down(j - 1, 1 - aslot, rows)
            for s in range(0, rc, sl):
                row0 = r * rc + s
                if row0 % period == 0:
                    halo_a, halo_b = start_halo(cols_a, row0 // period), start_halo(cols_b, row0 // period)
                a, halo_a = conv_slab(cols_a, u_ref[slot, 0, s:s + sl, :], halo_a)
                b, halo_b = conv_slab(cols_b, u_ref[slot, 1, s:s + sl, :], halo_b)
                act = (a * (1.0 + jnp.tanh(a * (GELU_C0 + GELU_C1 * (a * a))))) * b
                act_ref[aslot, row0:row0 + sl, :] = act.astype(BF16)
                if (row0 + sl) % period == 0:
                    keep_tail(cols_a, slot, 0, s + sl, row0 // period)
                    keep_tail(cols_b, slot, 1, s + sl, row0 // period)

    def project_down(j, aslot, rows):
        acc_ref[rows, :] += _dot(act_ref[aslot, rows, :], wdh_ref[pl.ds(pl.multiple_of(j * tn, tn), tn), :])

    assert nft % 2 == 1
    tile(0, 0, False)

    def trip(p, carry):
        tile(2 * p + 1, 1, True)
        tile(2 * p + 2, 0, True)
        return carry

    lax.fori_loop(0, nft // 2, trip, 0)
    for r in range(nrc):
        project_down(nft - 1, 0, slice(r * rc, (r + 1) * rc))
    y_ref[...] = x_ref[...] + _rmsnorm(acc_ref[...], gpost_ref[...])


def _ffn_call(x, gpre, w_up, conv_w, conv_b, conv_prev, w_down_half, gpost, *, tm, seq, tn):
    t, d = x.shape
    f = w_down_half.shape[0]
    nseq = t // seq
    assert f % tn == 0 and (seq % tm == 0 or tm % seq == 0) and t % tm == 0
    bps = max(seq // tm, 1)
    spb = max(tm // seq, 1)
    xrow = pl.BlockSpec((tm, d), lambda i: (i, 0))
    const = lambda a: pl.BlockSpec(a.shape, lambda i: (0, 0), pipeline_mode=pl.Buffered(1))
    state = lambda step: pl.BlockSpec((spb, CONV_W - 1, 2 * f), lambda i: (step(i), 0, 0))
    y, cn = pl.pallas_call(
        functools.partial(_ffn_kernel, seq=seq, tn=tn),
        grid=(t // tm,),
        in_specs=[xrow, const(gpre), const(w_up), const(conv_w), const(conv_b),
                  state(lambda i: i // bps), const(w_down_half), const(gpost)],
        out_specs=[xrow, state(lambda i: i)],
        out_shape=[jax.ShapeDtypeStruct((t, d), F32),
                   jax.ShapeDtypeStruct((t // tm * spb, CONV_W - 1, 2 * f), F32)],
        scratch_shapes=[pltpu.VMEM((tm, d), BF16), pltpu.VMEM((tm, d), F32),
                        pltpu.VMEM((SUBLANES, 2 * f), F32), pltpu.VMEM((2, tm, tn), BF16),
                        pltpu.VMEM((2, 2, min(tm, FFN_ROW_CHUNK), tn), F32)],
        compiler_params=_params("arbitrary"),
        name="ffn",
    )(x, gpre, w_up, conv_w, conv_b, conv_prev, w_down_half, gpost)
    return y, cn.reshape(nseq, bps, CONV_W - 1, 2 * f)[:, bps - 1]


def _prep_weights(w_in, w_gate_up, d_model):
    sizes = (GLA_HEADS * GLA_DK, GLA_HEADS * GLA_DK, GLA_HEADS * GLA_DV, GLA_RANK, GLA_HEADS * GLA_DV,
             DSA_HEADS * DSA_HEAD_DIM, DSA_KV_HEADS * DSA_HEAD_DIM, DSA_KV_HEADS * DSA_HEAD_DIM,
             IDX_HEADS * IDX_DIM, IDX_DIM, IDX_HEADS, 2 * d_model)
    offs = np.concatenate([[0], np.cumsum(sizes)])
    g_q, g_k, g_v, g_low, g_r, a_q, a_k, a_v, i_q, i_k, i_w, gate = (
        w_in[:, int(offs[n]):int(offs[n + 1])] for n in range(len(sizes)))
    w_main = jnp.concatenate([g_q, g_k, g_v, g_r, a_q, a_k, a_v, gate], axis=1).astype(BF16)
    zeros = lambda n: jnp.zeros((w_in.shape[0], n), w_in.dtype)
    misc = jnp.concatenate([g_low, i_w, zeros(LANES - GLA_RANK - IDX_HEADS)], axis=1)
    assert MISC_GLOW == 0 and MISC_IW == GLA_RANK
    w_hp = jnp.concatenate([i_q, i_k, zeros(LANES - IDX_DIM), misc], axis=1)
    assert w_hp.shape[1] == HP_WIDTH
    w_hp_hi, w_hp_lo = _split_bf16(w_hp)
    wup = jnp.concatenate([w_gate_up, jnp.zeros((LANES - GLA_RANK, w_gate_up.shape[1]), w_gate_up.dtype)], axis=0)
    wup_hi, wup_lo = _split_bf16(wup)
    return w_main, w_hp_hi, w_hp_lo, jnp.concatenate([wup_hi, wup_hi, wup_lo], axis=0)


def _run_layer(x, k_past, v_past, kidx_past, gla_s0, conv_prev, w):
    bsz, seq, d = x.shape
    past = k_past.shape[1]
    t = bsz * seq
    tm, gla_tb, dsa_tq, ffn_tm = min(t, PROJ_ROWS), min(seq, GLA_ROWS), min(seq, DSA_QUERIES), min(t, FFN_ROWS)
    x2 = x.reshape(t, d)
    gq, gk, gv, gr, aq, ak, av, gate, iq, ik, misc, ikcat, kb, vaug = _proj_call(
        x2, w["attn_pre_norm"], w["w_main"], w["w_hp_hi"], w["w_hp_lo"], tm)
    per_seq = lambda a: a.reshape(bsz, seq, a.shape[-1])

    gla_o, gla_s = _gla_call(per_seq(gq), per_seq(gk), per_seq(gv), per_seq(gr), per_seq(misc),
                             w["wup"], w["b_gla_gate"], w["gla_head_norm"], gla_s0, gla_tb)

    n_keys = past + seq
    key_span = KEY_UNROLL * KEY_CHUNK
    lk = -(-n_keys // key_span) * key_span
    packed = [per_seq(a) for a in (ikcat, kb, vaug)]
    if past:
        flat = lambda a: a.reshape((bsz * past,) + a.shape[2:]).astype(F32)
        packed_past = _pack_call(flat(kidx_past), flat(k_past), flat(v_past), min(bsz * past, PACK_ROWS))
        packed = [jnp.concatenate([p.reshape(bsz, past, -1), n], axis=1) for p, n in zip(packed_past, packed)]
    if lk > n_keys:
        packed = [jnp.pad(a, ((0, 0), (0, lk - n_keys), (0, 0))) for a in packed]

    dsa_o = _dsa_call(per_seq(iq), per_seq(misc), per_seq(aq), *packed,
                      q_start=past, n_keys=n_keys, tq=dsa_tq)

    x1 = _merge_call(x2, gla_o.reshape(t, -1), dsa_o.reshape(t, -1), gate,
                     w["w_branch_gla"], w["w_branch_dsa"], w["w_out"], w["attn_post_norm"], min(t, MERGE_ROWS))
    y, conv_new = _ffn_call(x1, w["ffn_pre_norm"], w["w_up"], w["conv_w"], w["conv_b"], conv_prev,
                            w["w_down_half"], w["ffn_post_norm"], tm=ffn_tm, seq=seq, tn=FFN_TILE)
    k_new = ak.reshape(bsz, seq, DSA_KV_HEADS, DSA_HEAD_DIM)
    v_new = av.reshape(bsz, seq, DSA_KV_HEADS, DSA_HEAD_DIM)
    return y.reshape(bsz, seq, d), k_new, v_new, per_seq(ik), gla_s, conv_new


def kernel(x_prompt, x_sample, cache_k, cache_v, cache_k_idx, state_gla, state_ffn_conv, attn_pre_norm, w_in, w_gla_gate_up, b_gla_gate, gla_head_norm, w_branch_gla, w_branch_dsa, w_out, attn_post_norm, ffn_pre_norm, w_up, conv_w, conv_b, w_down, ffn_post_norm):
    depth = w_in.shape[0]
    bsz, _, d = x_prompt.shape
    dt = x_prompt.dtype
    y_prompt, y_sample = x_prompt, x_sample
    prompt_new, sample_new = [], []
    row = lambda a: a.reshape(1, -1)
    for l in range(depth):
        w_main, w_hp_hi, w_hp_lo, wup = _prep_weights(w_in[l], w_gla_gate_up[l], d)
        w = dict(attn_pre_norm=row(attn_pre_norm[l]), w_main=w_main, w_hp_hi=w_hp_hi, w_hp_lo=w_hp_lo, wup=wup,
                 b_gla_gate=row(b_gla_gate[l]), gla_head_norm=row(gla_head_norm[l]),
                 w_branch_gla=w_branch_gla[l].astype(BF16), w_branch_dsa=w_branch_dsa[l].astype(BF16),
                 w_out=w_out[l].astype(BF16), attn_post_norm=row(attn_post_norm[l]),
                 ffn_pre_norm=row(ffn_pre_norm[l]), w_up=w_up[l].astype(BF16), conv_w=conv_w[l],
                 conv_b=row(conv_b[l]), w_down_half=(0.5 * w_down[l]).astype(BF16),
                 ffn_post_norm=row(ffn_post_norm[l]))
        ffn2 = conv_w.shape[-1]
        y_prompt, *new = _run_layer(
            y_prompt,
            jnp.zeros((bsz, 0, DSA_KV_HEADS * DSA_HEAD_DIM), dt),
            jnp.zeros((bsz, 0, DSA_KV_HEADS * DSA_HEAD_DIM), dt),
            jnp.zeros((bsz, 0, IDX_DIM), dt),
            jnp.zeros((bsz, GLA_HEADS, GLA_DK, GLA_DV), F32),
            jnp.zeros((bsz, CONV_W - 1, ffn2), dt),
            w)
        prompt_new.append(new)
        y_sample, *new = _run_layer(
            y_sample, cache_k[l], cache_v[l], cache_k_idx[l], state_gla[l], state_ffn_conv[l],
            w)
        sample_new.append(new)
    k_p, v_p, kidx_p, gla_p, conv_p = (jnp.stack(t, axis=0) for t in zip(*prompt_new))
    k_s, v_s, kidx_s, gla_s, conv_s = (jnp.stack(t, axis=0) for t in zip(*sample_new))
    return (y_prompt, y_sample, k_p, v_p, kidx_p, gla_p, conv_p, k_s, v_s, kidx_s, gla_s, conv_s)
```

```python
import functools

import numpy as np
import jax
import jax.numpy as jnp
from jax import lax
from jax.experimental import pallas as pl
from jax.experimental.pallas import tpu as pltpu

F32 = jnp.float32
BF16 = jnp.bfloat16
I32 = jnp.int32
I16 = jnp.int16
HIGHEST = lax.Precision.HIGHEST

CHUNK = 64
GLA_HEADS = 4
GLA_DK = 128
GLA_DV = 256
GLA_RANK = 16
GLA_GATE_TAU = 16.0
DSA_HEADS = 8
DSA_KV_HEADS = 2
DSA_HEAD_DIM = 64
IDX_HEADS = 4
IDX_DIM = 64
IDX_SCALE = (IDX_HEADS * IDX_DIM) ** -0.5
TOPK_MAX = 256
CONV_W = 3
EPS = 1e-6

LANES = 128
SUBLANES = 8
VMEM_LIMIT = 56 * 1024 * 1024

MISC_GLOW = 0
MISC_IW = 16

KEY_CHUNK = 256
KEY_UNROLL = 2
GLA_UNROLL = 8
INT16_MIN = -(2 ** 15)
LOG2_E = float(np.log2(np.e))
ATTN_Q_SCALE = DSA_HEAD_DIM ** -0.5 * LOG2_E
MASKED = -1e30
NT_DIMS = (((1,), (1,)), ((), ()))
TN_DIMS = (((0,), (0,)), ((), ()))


def _dot(a, b, **kw):
    return jnp.dot(a, b, preferred_element_type=F32, **kw)


def _dot_nt(a, b):
    return lax.dot_general(a, b, NT_DIMS, preferred_element_type=F32)


def _split_bf16(x):
    hi = x.astype(BF16)
    lo = (x - hi.astype(F32)).astype(BF16)
    return hi, lo


def _rmsnorm(x, g):
    return x * lax.rsqrt(jnp.mean(x * x, axis=-1, keepdims=True) + EPS) * g


def _params(*sem):
    return pltpu.CompilerParams(dimension_semantics=sem, vmem_limit_bytes=VMEM_LIMIT)


MAIN_SIZES = (GLA_HEADS * GLA_DK, GLA_HEADS * GLA_DK, GLA_HEADS * GLA_DV, GLA_HEADS * GLA_DV,
              DSA_HEADS * DSA_HEAD_DIM, DSA_KV_HEADS * DSA_HEAD_DIM, DSA_KV_HEADS * DSA_HEAD_DIM)
HP_WIDTH = 512
PROJ_ROWS = 256
PACK_ROWS = 1024
GLA_ROWS = 512
DSA_QUERIES = 256
MERGE_ROWS = 512
FFN_ROWS = 1024
FFN_TILE = 256
PACK_WIDTHS = (3 * IDX_DIM, DSA_KV_HEADS * DSA_HEAD_DIM, 2 * DSA_KV_HEADS * DSA_HEAD_DIM)


def _pack_keys(ik, k, v):
    hi, lo = _split_bf16(ik)
    ikcat = jnp.concatenate([hi, hi, lo], axis=1)
    vb = v.astype(BF16)
    ones = jnp.ones((v.shape[0], DSA_HEAD_DIM), BF16)
    parts = []
    for g in range(DSA_KV_HEADS):
        parts += [vb[:, g * DSA_HEAD_DIM:(g + 1) * DSA_HEAD_DIM], ones]
    return ikcat, k.astype(BF16), jnp.concatenate(parts, axis=1)


def _store_heads(o_ref, x):
    dim = o_ref.shape[-1]
    for g in range(o_ref.shape[1]):
        o_ref[:, g, :] = x[:, g * dim:(g + 1) * dim].astype(o_ref.dtype)


def _load_heads(x_ref):
    return jnp.concatenate([x_ref[:, g, :] for g in range(x_ref.shape[1])], axis=1)


def _proj_kernel(x_ref, g_ref, wm_ref, whi_ref, wlo_ref,
                 gq_ref, gk_ref, gv_ref, gr_ref, aq_ref, ak_ref, av_ref, gate_ref, iq_ref, ik_ref, misc_ref,
                 ikcat_ref, kb_ref, vaug_ref):
    xn = _rmsnorm(x_ref[...], g_ref[...])
    xh, xl = _split_bf16(xn)
    outs = (gq_ref, gk_ref, gv_ref, gr_ref, aq_ref, ak_ref, av_ref, gate_ref)
    off = 0
    vals = []
    for o_ref, n in zip(outs, MAIN_SIZES + (gate_ref.shape[-1],)):
        vals.append(_dot(xh, wm_ref[:, off:off + n]))
        if o_ref is aq_ref:
            vals[-1] = vals[-1] * ATTN_Q_SCALE
        if o_ref.ndim == 2:
            o_ref[...] = vals[-1].astype(o_ref.dtype)
        else:
            _store_heads(o_ref, vals[-1])
        off += n
    ak, av = vals[5], vals[6]
    whi = whi_ref[...]
    hp = _dot(xh, whi) + _dot(xl, whi) + _dot(xh, wlo_ref[...])
    iq_ref[...] = hp[:, 0:256]
    ik_ref[...] = hp[:, 256:320]
    misc_ref[...] = hp[:, 384:512]
    ikcat_ref[...], kb_ref[...], vaug_ref[...] = _pack_keys(hp[:, 256:320], ak, av)


def _proj_call(x, g, wm, whi, wlo, tm):
    t, d = x.shape
    gate_w = wm.shape[1] - sum(MAIN_SIZES)
    widths = MAIN_SIZES + (gate_w,)
    dtypes = (BF16, BF16, BF16, BF16, BF16, F32, F32, BF16)
    out_shape = [jax.ShapeDtypeStruct((t, n), dt) for n, dt in zip(widths, dtypes)]
    for n in (5, 6):
        out_shape[n] = jax.ShapeDtypeStruct((t, DSA_KV_HEADS, DSA_HEAD_DIM), F32)
    out_shape += [jax.ShapeDtypeStruct((t, n), F32) for n in (256, 64, 128)]
    out_shape += [jax.ShapeDtypeStruct((t, n), BF16) for n in PACK_WIDTHS]
    row = lambda n: pl.BlockSpec((tm, n), lambda i: (i, 0))
    rows = lambda s: pl.BlockSpec((tm,) + s.shape[1:], lambda i: (i,) + (0,) * (len(s.shape) - 1))
    full = lambda a: pl.BlockSpec(a.shape, lambda i: (0, 0))
    return pl.pallas_call(
        _proj_kernel,
        grid=(t // tm,),
        in_specs=[row(d), full(g), full(wm), full(whi), full(wlo)],
        out_specs=[rows(s) for s in out_shape],
        out_shape=out_shape,
        compiler_params=_params("parallel"),
        name="proj",
    )(x, g, wm, whi, wlo)


def _pack_kernel(ik_ref, k_ref, v_ref, ikcat_ref, kb_ref, vaug_ref):
    ikcat_ref[...], kb_ref[...], vaug_ref[...] = _pack_keys(ik_ref[...], _load_heads(k_ref), _load_heads(v_ref))


def _pack_call(ik, k, v, tm):
    t = ik.shape[0]
    row = lambda n: pl.BlockSpec((tm, n), lambda i: (i, 0))
    heads = pl.BlockSpec((tm,) + k.shape[1:], lambda i: (i, 0, 0))
    return pl.pallas_call(
        _pack_kernel,
        grid=(t // tm,),
        in_specs=[row(ik.shape[1]), heads, heads],
        out_specs=[row(n) for n in PACK_WIDTHS],
        out_shape=[jax.ShapeDtypeStruct((t, n), BF16) for n in PACK_WIDTHS],
        compiler_params=_params("parallel"),
        name="pack",
    )(ik, k, v)


def _gla_kernel(gq_ref, gk_ref, gv_ref, gr_ref, misc_ref, wup_ref, bup_ref, hn_ref, s0_ref,
                o_ref, sfin_ref, st_ref, qd_ref, ki_ref, kr_ref, dl_ref, *, nchunks):
    j = pl.program_id(1)

    @pl.when(j == 0)
    def _():
        for h in range(GLA_HEADS):
            st_ref[h] = s0_ref[h].T

    row = lax.broadcasted_iota(I32, (CHUNK, CHUNK), 0)
    col = lax.broadcasted_iota(I32, (CHUNK, CHUNK), 1)
    causal = row >= col
    tri = causal.astype(BF16)

    m_hi, m_lo = _split_bf16(misc_ref[...])
    z = _dot(jnp.concatenate([m_hi, m_lo, m_hi], axis=1), wup_ref[...]) + bup_ref[...]
    log_a = (jnp.minimum(z, 0.0) - jnp.log(1.0 + jnp.exp(-jnp.abs(z)))) * (LOG2_E / GLA_GATE_TAU)
    la_hi, la_lo = _split_bf16(log_a)
    tri2 = jnp.concatenate([tri, tri], axis=1)
    for c in range(nchunks):
        rows = slice(c * CHUNK, (c + 1) * CHUNK)
        b = _dot(tri2, jnp.concatenate([la_hi[rows, :], la_lo[rows, :]], axis=0))
        b_last = b[CHUNK - 1:CHUNK, :]
        q = gq_ref[rows, :].astype(F32)
        k = gk_ref[rows, :].astype(F32)
        qd_ref[rows, :] = (q * (GLA_DK ** -0.5) * jnp.exp2(b)).astype(BF16)
        ki_ref[rows, :] = (k * jnp.exp2(-b)).astype(BF16)
        kr_ref[rows, :] = (k * jnp.exp2(b_last - b)).astype(BF16)
        dl_ref[c] = jnp.broadcast_to(jnp.exp2(b_last), dl_ref.shape[1:])

    def chunk(c, carry):
        rows = pl.ds(pl.multiple_of(c * CHUNK, CHUNK), CHUNK)
        q_dec, k_inv, k_rem = qd_ref[rows, :], ki_ref[rows, :], kr_ref[rows, :]
        d_last = dl_ref[c][0:1, :]
        v = gv_ref[rows, :]
        gr = gr_ref[rows, :].astype(F32)
        hn = hn_ref[...]
        for h in range(GLA_HEADS):
            ks = slice(h * GLA_DK, (h + 1) * GLA_DK)
            vs = slice(h * GLA_DV, (h + 1) * GLA_DV)
            qh, vh = q_dec[:, ks], v[:, vs]
            scores = jnp.where(causal, _dot_nt(qh, k_inv[:, ks]), 0.0)
            st = st_ref[h]
            o = _dot(scores.astype(BF16), vh) + _dot_nt(qh, st.astype(BF16))
            ut = lax.dot_general(vh, k_rem[:, ks], TN_DIMS, preferred_element_type=F32)
            st_ref[h] = st * d_last[:, ks] + ut
            y = _rmsnorm(o, hn)
            g = gr[:, vs]
            o_ref[rows, vs] = (y * (g * jax.nn.sigmoid(g))).astype(o_ref.dtype)
        return carry

    lax.fori_loop(0, nchunks, chunk, 0, unroll=min(nchunks, GLA_UNROLL))

    @pl.when(j == pl.num_programs(1) - 1)
    def _():
        for h in range(GLA_HEADS):
            sfin_ref[h] = st_ref[h].T


def _gla_call(gq, gk, gv, gr, misc, wup, bup, hn, s0, tb):
    bsz, seq, _ = gq.shape
    blk = lambda n: pl.BlockSpec((None, tb, n), lambda b, j: (b, j, 0))
    full2 = lambda a: pl.BlockSpec(a.shape, lambda b, j: (0, 0))
    st_spec = pl.BlockSpec((None, GLA_HEADS, GLA_DK, GLA_DV), lambda b, j: (b, 0, 0, 0))
    return pl.pallas_call(
        functools.partial(_gla_kernel, nchunks=tb // CHUNK),
        grid=(bsz, seq // tb),
        in_specs=[blk(gq.shape[2]), blk(gk.shape[2]), blk(gv.shape[2]), blk(gr.shape[2]), blk(misc.shape[2]),
                  full2(wup), full2(bup), full2(hn), st_spec],
        out_specs=[blk(gv.shape[2]), st_spec],
        out_shape=[jax.ShapeDtypeStruct((bsz, seq, gv.shape[2]), BF16),
                   jax.ShapeDtypeStruct(s0.shape, F32)],
        scratch_shapes=[pltpu.VMEM((GLA_HEADS, GLA_DV, GLA_DK), F32),
                        pltpu.VMEM((tb, gq.shape[2]), BF16), pltpu.VMEM((tb, gq.shape[2]), BF16),
                        pltpu.VMEM((tb, gq.shape[2]), BF16),
                        pltpu.VMEM((tb // CHUNK, SUBLANES, gq.shape[2]), F32)],
        compiler_params=_params("parallel", "arbitrary"),
        name="gla",
    )(gq, gk, gv, gr, misc, wup, bup, hn, s0)


def _dsa_kernel(iq_ref, misc_ref, aq_ref, ikcat_ref, kb_ref, vaug_ref, o_ref,
                keys_ref, hi_ref, lo_ref, m_ref, acc_ref,
                *, tq, tqp, q_start, n_keys, topk, single_block):
    qi = 0 if single_block else pl.program_id(1)
    kc = KEY_CHUNK
    hd = DSA_HEAD_DIM
    hpg = DSA_HEADS // DSA_KV_HEADS

    first_q = q_start + qi * tq
    last_q = first_q + tq - 1
    span = KEY_UNROLL * kc
    lowest = min if single_block else jnp.minimum
    nt = (lowest(n_keys, ((last_q >> 6) + 1) * CHUNK) + span - 1) // span
    nfull = lowest(n_keys, ((first_q >> 6) + 1) * CHUNK) // span

    def loop(lo, hi, trip, carry):
        if not single_block:
            return lax.fori_loop(lo, hi, trip, carry)
        for t in range(lo, hi):
            carry = trip(t, carry)
        return carry

    def trips(body, masked):
        def trip(t, carry):
            for u in range(KEY_UNROLL):
                carry = body(pl.multiple_of((t * KEY_UNROLL + u) * kc, kc), masked, carry)
            return carry
        return trip

    def for_chunks(body, init):
        return loop(nfull, nt, trips(body, True), loop(0, nfull, trips(body, False), init))

    def for_all_chunks(body, init):
        return loop(0, nt, trips(lambda r0, _, carry: body(r0, carry), None), init)

    def pad_rows(x):
        if tqp == tq:
            return x
        return jnp.concatenate([x, jnp.zeros((tqp - tq, x.shape[1]), x.dtype)], axis=0)

    iq_hi, iq_lo = _split_bf16(pad_rows(iq_ref[...]))
    iq_cat = []
    for h in range(IDX_HEADS):
        s = slice(h * IDX_DIM, (h + 1) * IDX_DIM)
        iq_cat.append(jnp.concatenate([iq_hi[:, s], iq_lo[:, s], iq_hi[:, s]], axis=1))
    wts = pad_rows(misc_ref[...]).T[MISC_IW:MISC_IW + IDX_HEADS, :] * IDX_SCALE
    q_chunk = (first_q + lax.broadcasted_iota(I32, (1, tqp), 1)) >> 6

    def admissible(r0):
        k_pos = r0 + lax.broadcasted_iota(I32, (kc, tqp), 0)
        return ((k_pos >> 6) <= q_chunk) & (k_pos < n_keys)

    def score_body(r0, masked, carry):
        rows = pl.ds(r0, kc)
        ik = ikcat_ref[rows, :]
        acc = jnp.zeros((kc, tqp), F32)
        for h in range(IDX_HEADS):
            acc = acc + wts[h:h + 1, :] * jnp.maximum(_dot_nt(ik, iq_cat[h]), 0.0)
        if masked:
            acc = jnp.where(admissible(r0), acc, -jnp.inf)
        bits = lax.bitcast_convert_type(acc, I32)
        key = jnp.where(bits < 0, bits ^ jnp.int32(0x7FFFFFFF), bits)
        keys_ref[rows, :] = key
        hi_ref[rows, :] = (key >> 16).astype(I16)
        lo_ref[rows, :] = ((key & 0xFFFF) + INT16_MIN).astype(I16)
        return carry

    for_chunks(score_body, 0)

    grp = 32
    one, zero = jnp.asarray(1, BF16), jnp.asarray(0, BF16)

    def count16(ref, pred):
        def body(r0, acc):
            m = jnp.where(pred(ref[pl.ds(r0, kc), :]), one, zero)
            for i in range(kc // grp):
                acc = acc + m[i * grp:(i + 1) * grp, :]
            return acc
        acc = for_all_chunks(body, jnp.zeros((grp, tqp), BF16))
        return jnp.sum(acc.astype(F32), axis=0, keepdims=True)

    def search16(ref, kth):
        def bit_body(i, carry):
            prefix, above = carry
            cand = prefix | (jnp.int32(1) << (15 - i))
            cand16 = (cand + INT16_MIN).astype(I16)
            tot = count16(ref, lambda ch: ch >= cand16)
            keep = tot >= kth
            return jnp.where(keep, cand, prefix), jnp.where(keep, above, tot)
        return lax.fori_loop(0, 16, bit_body, (jnp.zeros((1, tqp), I32), jnp.zeros((1, tqp), F32)))

    kth = jnp.full((1, tqp), topk, F32)
    a_u, above_a = search16(hi_ref, kth)
    a16 = (a_u + INT16_MIN).astype(I16)
    kth_lo = kth - above_a

    def bucket_body(r0, carry):
        rows = pl.ds(r0, kc)
        lo_ref[rows, :] = jnp.where(hi_ref[rows, :] == a16, lo_ref[rows, :], jnp.asarray(INT16_MIN, I16))
        return carry

    for_all_chunks(bucket_body, 0)
    b_u, above_b = search16(lo_ref, kth_lo)
    thr = ((a_u + INT16_MIN) << 16) | b_u
    need = kth_lo - above_b

    tri = (lax.broadcasted_iota(I32, (kc, kc), 0) >= lax.broadcasted_iota(I32, (kc, kc), 1)).astype(BF16)

    def sel_bias(r0, masked, running):
        ch = keys_ref[pl.ds(r0, kc), :]
        eq = ch == thr
        rank = _dot(tri, jnp.where(eq, 1.0, 0.0).astype(BF16)) + running
        sel = (ch > thr) | (eq & (rank <= need))
        if masked:
            sel = sel & admissible(r0)
        return jnp.where(sel, 0.0, MASKED).T[:tq, :], rank[kc - 1:kc, :]

    m_ref[...] = jnp.full(m_ref.shape, MASKED, F32)
    acc_ref[...] = jnp.zeros(acc_ref.shape, F32)
    q_groups = []
    for g in range(DSA_KV_HEADS):
        qg = jnp.concatenate([aq_ref[:, h * hd:(h + 1) * hd] for h in range(g * hpg, (g + 1) * hpg)], axis=0)
        q_groups.append(qg)

    def attend(t, masked, running):
        r0 = pl.multiple_of(t * span, span)
        cols = pl.ds(r0, span)
        biases = []
        for u in range(KEY_UNROLL):
            bias, running = sel_bias(pl.multiple_of(r0 + u * kc, kc), masked, running)
            biases.append(bias)
        bias = jnp.concatenate(biases, axis=1)
        for g in range(DSA_KV_HEADS):
            s = _dot_nt(q_groups[g], kb_ref[cols, g * hd:(g + 1) * hd])
            s = jnp.concatenate([s[r * tq:(r + 1) * tq, :] + bias for r in range(hpg)], axis=0)
            tiles = [s[:, i * LANES:(i + 1) * LANES] for i in range(span // LANES)]
            m_old = m_ref[g]
            m_new = jnp.maximum(m_old, jnp.max(functools.reduce(jnp.maximum, tiles), axis=1, keepdims=True))
            p = jnp.concatenate([jnp.exp2(t - m_new) for t in tiles], axis=1)
            pv = _dot(p.astype(BF16), vaug_ref[cols, 2 * g * hd:2 * (g + 1) * hd])
            acc_ref[g] = acc_ref[g] * jnp.exp2(m_old - m_new) + pv
            m_ref[g] = m_new
        return running

    no_ties = jnp.zeros((1, tqp), F32)
    loop(nfull, nt, lambda t, c: attend(t, True, c), loop(0, nfull, lambda t, c: attend(t, False, c), no_ties))

    for g in range(DSA_KV_HEADS):
        acc = acc_ref[g]
        out = acc[:, :hd] / acc[:, hd:hd + 1]
        for r in range(hpg):
            h = g * hpg + r
            o_ref[:, h * hd:(h + 1) * hd] = out[r * tq:(r + 1) * tq, :].astype(o_ref.dtype)


def _dsa_call(iq, misc, aq, ikcat, kb, vaug, *, q_start, n_keys, tq):
    bsz, lq, _ = iq.shape
    lk = ikcat.shape[1]
    assert lk % (KEY_UNROLL * KEY_CHUNK) == 0 and lq % tq == 0 and lk // 32 <= 256
    tqp = max(tq, LANES)
    topk = min(TOPK_MAX, n_keys // 4)
    rows_g = (DSA_HEADS // DSA_KV_HEADS) * tq
    qblk = lambda n: pl.BlockSpec((None, tq, n), lambda b, q: (b, q, 0))
    kblk = lambda n: pl.BlockSpec((None, lk, n), lambda b, q: (b, 0, 0))
    kern = functools.partial(_dsa_kernel, tq=tq, tqp=tqp, q_start=q_start, n_keys=n_keys, topk=topk,
                             single_block=(lq == tq))
    return pl.pallas_call(
        kern,
        grid=(bsz, lq // tq),
        in_specs=[qblk(iq.shape[2]), qblk(misc.shape[2]), qblk(aq.shape[2]),
                  kblk(ikcat.shape[2]), kblk(kb.shape[2]), kblk(vaug.shape[2])],
        out_specs=qblk(aq.shape[2]),
        out_shape=jax.ShapeDtypeStruct(aq.shape, BF16),
        scratch_shapes=[pltpu.VMEM((lk, tqp), I32),
                        pltpu.VMEM((lk, tqp), I16),
                        pltpu.VMEM((lk, tqp), I16),
                        pltpu.VMEM((DSA_KV_HEADS, rows_g, LANES), F32),
                        pltpu.VMEM((DSA_KV_HEADS, rows_g, 2 * DSA_HEAD_DIM), F32)],
        compiler_params=_params("parallel", "arbitrary"),
        name="dsa",
    )(iq, misc, aq, ikcat, kb, vaug)


def _merge_kernel(x_ref, go_ref, do_ref, gate_ref, wa_ref, wb_ref, wo_ref, g_ref, y_ref):
    d = x_ref.shape[-1]
    a = _dot(go_ref[...], wa_ref[...])
    b = _dot(do_ref[...], wb_ref[...])
    gate = gate_ref[...].astype(F32)
    mixed = jax.nn.sigmoid(gate[:, :d]) * a + jax.nn.sigmoid(gate[:, d:]) * b
    y_ref[...] = x_ref[...] + _rmsnorm(_dot(mixed.astype(BF16), wo_ref[...]), g_ref[...])


def _merge_call(x, go, do, gate, wa, wb, wo, g, tm):
    t, d = x.shape
    row = lambda n: pl.BlockSpec((tm, n), lambda i: (i, 0))
    full = lambda a: pl.BlockSpec(a.shape, lambda i: (0, 0))
    return pl.pallas_call(
        _merge_kernel,
        grid=(t // tm,),
        in_specs=[row(d), row(go.shape[1]), row(do.shape[1]), row(gate.shape[1]),
                  full(wa), full(wb), full(wo), full(g)],
        out_specs=row(d),
        out_shape=jax.ShapeDtypeStruct((t, d), F32),
        compiler_params=_params("parallel"),
        name="merge",
    )(x, go, do, gate, wa, wb, wo, g)


FFN_ROW_CHUNK = 512
FFN_SLAB_ROWS = 64
GELU_C0 = float(np.sqrt(2.0 / np.pi))
GELU_C1 = 0.044715 * GELU_C0


def _ffn_kernel(x_ref, gpre_ref, wu_ref, cw_ref, cb_ref, pv_ref, wdh_ref, gpost_ref,
                y_ref, cn_ref, xn_ref, acc_ref, car_ref, act_ref, u_ref, *, seq, tn):
    i = pl.program_id(0)
    tm = x_ref.shape[0]
    f = wdh_ref.shape[0]
    xn_ref[...] = _rmsnorm(x_ref[...], gpre_ref[...]).astype(BF16)
    acc_ref[...] = jnp.zeros_like(acc_ref)
    first = (i % max(seq // tm, 1)) == 0
    top = lax.broadcasted_iota(I32, (SUBLANES, tn), 0)
    rc = min(tm, FFN_ROW_CHUNK)

    def up(cols, r):
        return _dot(xn_ref[r * rc:(r + 1) * rc, :], wu_ref[:, cols])

    def conv_slab(cols, u, halo):
        r1, r2 = pltpu.roll(u, 1, 0), pltpu.roll(u, 2, 0)
        top1 = jnp.where(top == 0, halo[1:2, :], r1[:SUBLANES, :])
        top2 = jnp.where(top == 0, halo[0:1, :], jnp.where(top == 1, halo[1:2, :], r2[:SUBLANES, :]))
        u1 = jnp.concatenate([top1, r1[SUBLANES:, :]], axis=0)
        u2 = jnp.concatenate([top2, r2[SUBLANES:, :]], axis=0)
        cw = cw_ref[:, cols]
        return cb_ref[:, cols] + (cw[0:1, :] * u2 + cw[1:2, :] * u1 + cw[2:3, :] * u), u[u.shape[0] - 2:, :]

    period = min(seq, tm)
    sl = min(rc, FFN_SLAB_ROWS, period)

    def keep_tail(cols, slot, half, end, k):
        cn_ref[k, :, cols] = u_ref[slot, half, end - (CONV_W - 1):end, :]
        if period == tm:
            car_ref[:, cols] = u_ref[slot, half, end - SUBLANES:end, :]

    def start_halo(cols, k):
        if period == tm:
            return jnp.where(first, pv_ref[0, :, cols], car_ref[SUBLANES - 2:SUBLANES, cols])
        return pv_ref[k, :, cols]

    nft = f // tn
    nrc = tm // rc

    def tile(j, aslot, with_down):
        cols_a = pl.ds(pl.multiple_of(j * tn, tn), tn)
        cols_b = pl.ds(pl.multiple_of(f + j * tn, tn), tn)
        halo_a = halo_b = None

        def up_to(slot, r):
            u_ref[slot, 0] = up(cols_a, r)
            u_ref[slot, 1] = up(cols_b, r)

        up_to(0, 0)
        for r in range(nrc):
            rows = slice(r * rc, (r + 1) * rc)
            slot = r % 2
            if r + 1 < nrc:
                up_to(1 - slot, r + 1)
            if with_down:
                project_down(j - 1, 1 - aslot, rows)
            for s in range(0, rc, sl):
                row0 = r * rc + s
                if row0 % period == 0:
                    halo_a, halo_b = start_halo(cols_a, row0 // period), start_halo(cols_b, row0 // period)
                a, halo_a = conv_slab(cols_a, u_ref[slot, 0, s:s + sl, :], halo_a)
                b, halo_b = conv_slab(cols_b, u_ref[slot, 1, s:s + sl, :], halo_b)
                act = (a * (1.0 + jnp.tanh(a * (GELU_C0 + GELU_C1 * (a * a))))) * b
                act_ref[aslot, row0:row0 + sl, :] = act.astype(BF16)
                if (row0 + sl) % period == 0:
                    keep_tail(cols_a, slot, 0, s + sl, row0 // period)
                    keep_tail(cols_b, slot, 1, s + sl, row0 // period)

    def project_down(j, aslot, rows):
        acc_ref[rows, :] += _dot(act_ref[aslot, rows, :], wdh_ref[pl.ds(pl.multiple_of(j * tn, tn), tn), :])

    assert nft % 2 == 1
    tile(0, 0, False)

    def trip(p, carry):
        tile(2 * p + 1, 1, True)
        tile(2 * p + 2, 0, True)
        return carry

    lax.fori_loop(0, nft // 2, trip, 0)
    for r in range(nrc):
        project_down(nft - 1, 0, slice(r * rc, (r + 1) * rc))
    y_ref[...] = x_ref[...] + _rmsnorm(acc_ref[...], gpost_ref[...])


def _ffn_call(x, gpre, w_up, conv_w, conv_b, conv_prev, w_down_half, gpost, *, tm, seq, tn):
    t, d = x.shape
    f = w_down_half.shape[0]
    nseq = t // seq
    assert f % tn == 0 and (seq % tm == 0 or tm % seq == 0) and t % tm == 0
    bps = max(seq // tm, 1)
    spb = max(tm // seq, 1)
    xrow = pl.BlockSpec((tm, d), lambda i: (i, 0))
    const = lambda a: pl.BlockSpec(a.shape, lambda i: (0, 0), pipeline_mode=pl.Buffered(1))
    state = lambda step: pl.BlockSpec((spb, CONV_W - 1, 2 * f), lambda i: (step(i), 0, 0))
    y, cn = pl.pallas_call(
        functools.partial(_ffn_kernel, seq=seq, tn=tn),
        grid=(t // tm,),
        in_specs=[xrow, const(gpre), const(w_up), const(conv_w), const(conv_b),
                  state(lambda i: i // bps), const(w_down_half), const(gpost)],
        out_specs=[xrow, state(lambda i: i)],
        out_shape=[jax.ShapeDtypeStruct((t, d), F32),
                   jax.ShapeDtypeStruct((t // tm * spb, CONV_W - 1, 2 * f), F32)],
        scratch_shapes=[pltpu.VMEM((tm, d), BF16), pltpu.VMEM((tm, d), F32),
                        pltpu.VMEM((SUBLANES, 2 * f), F32), pltpu.VMEM((2, tm, tn), BF16),
                        pltpu.VMEM((2, 2, min(tm, FFN_ROW_CHUNK), tn), F32)],
        compiler_params=_params("arbitrary"),
        name="ffn",
    )(x, gpre, w_up, conv_w, conv_b, conv_prev, w_down_half, gpost)
    return y, cn.reshape(nseq, bps, CONV_W - 1, 2 * f)[:, bps - 1]


def _prep_weights(w_in, w_gate_up, d_model):
    sizes = (GLA_HEADS * GLA_DK, GLA_HEADS * GLA_DK, GLA_HEADS * GLA_DV, GLA_RANK, GLA_HEADS * GLA_DV,
             DSA_HEADS * DSA_HEAD_DIM, DSA_KV_HEADS * DSA_HEAD_DIM, DSA_KV_HEADS * DSA_HEAD_DIM,
             IDX_HEADS * IDX_DIM, IDX_DIM, IDX_HEADS, 2 * d_model)
    offs = np.concatenate([[0], np.cumsum(sizes)])
    g_q, g_k, g_v, g_low, g_r, a_q, a_k, a_v, i_q, i_k, i_w, gate = (
        w_in[:, int(offs[n]):int(offs[n + 1])] for n in range(len(sizes)))
    w_main = jnp.concatenate([g_q, g_k, g_v, g_r, a_q, a_k, a_v, gate], axis=1).astype(BF16)
    zeros = lambda n: jnp.zeros((w_in.shape[0], n), w_in.dtype)
    misc = jnp.concatenate([g_low, i_w, zeros(LANES - GLA_RANK - IDX_HEADS)], axis=1)
    assert MISC_GLOW == 0 and MISC_IW == GLA_RANK
    w_hp = jnp.concatenate([i_q, i_k, zeros(LANES - IDX_DIM), misc], axis=1)
    assert w_hp.shape[1] == HP_WIDTH
    w_hp_hi, w_hp_lo = _split_bf16(w_hp)
    wup = jnp.concatenate([w_gate_up, jnp.zeros((LANES - GLA_RANK, w_gate_up.shape[1]), w_gate_up.dtype)], axis=0)
    wup_hi, wup_lo = _split_bf16(wup)
    return w_main, w_hp_hi, w_hp_lo, jnp.concatenate([wup_hi, wup_hi, wup_lo], axis=0)


def _run_layer(x, k_past, v_past, kidx_past, gla_s0, conv_prev, w):
    bsz, seq, d = x.shape
    past = k_past.shape[1]
    t = bsz * seq
    tm, gla_tb, dsa_tq, ffn_tm = min(t, PROJ_ROWS), min(seq, GLA_ROWS), min(seq, DSA_QUERIES), min(t, FFN_ROWS)
    x2 = x.reshape(t, d)
    gq, gk, gv, gr, aq, ak, av, gate, iq, ik, misc, ikcat, kb, vaug = _proj_call(
        x2, w["attn_pre_norm"], w["w_main"], w["w_hp_hi"], w["w_hp_lo"], tm)
    per_seq = lambda a: a.reshape(bsz, seq, a.shape[-1])

    gla_o, gla_s = _gla_call(per_seq(gq), per_seq(gk), per_seq(gv), per_seq(gr), per_seq(misc),
                             w["wup"], w["b_gla_gate"], w["gla_head_norm"], gla_s0, gla_tb)

    n_keys = past + seq
    key_span = KEY_UNROLL * KEY_CHUNK
    lk = -(-n_keys // key_span) * key_span
    packed = [per_seq(a) for a in (ikcat, kb, vaug)]
    if past:
        flat = lambda a: a.reshape((bsz * past,) + a.shape[2:]).astype(F32)
        packed_past = _pack_call(flat(kidx_past), flat(k_past), flat(v_past), min(bsz * past, PACK_ROWS))
        packed = [jnp.concatenate([p.reshape(bsz, past, -1), n], axis=1) for p, n in zip(packed_past, packed)]
    if lk > n_keys:
        packed = [jnp.pad(a, ((0, 0), (0, lk - n_keys), (0, 0))) for a in packed]

    dsa_o = _dsa_call(per_seq(iq), per_seq(misc), per_seq(aq), *packed,
                      q_start=past, n_keys=n_keys, tq=dsa_tq)

    x1 = _merge_call(x2, gla_o.reshape(t, -1), dsa_o.reshape(t, -1), gate,
                     w["w_branch_gla"], w["w_branch_dsa"], w["w_out"], w["attn_post_norm"], min(t, MERGE_ROWS))
    y, conv_new = _ffn_call(x1, w["ffn_pre_norm"], w["w_up"], w["conv_w"], w["conv_b"], conv_prev,
                            w["w_down_half"], w["ffn_post_norm"], tm=ffn_tm, seq=seq, tn=FFN_TILE)
    k_new = ak.reshape(bsz, seq, DSA_KV_HEADS, DSA_HEAD_DIM)
    v_new = av.reshape(bsz, seq, DSA_KV_HEADS, DSA_HEAD_DIM)
    return y.reshape(bsz, seq, d), k_new, v_new, per_seq(ik), gla_s, conv_new


def kernel(x_prompt, x_sample, cache_k, cache_v, cache_k_idx, state_gla, state_ffn_conv, attn_pre_norm, w_in, w_gla_gate_up, b_gla_gate, gla_head_norm, w_branch_gla, w_branch_dsa, w_out, attn_post_norm, ffn_pre_norm, w_up, conv_w, conv_b, w_down, ffn_post_norm):
    depth = w_in.shape[0]
    bsz, _, d = x_prompt.shape
    dt = x_prompt.dtype
    y_prompt, y_sample = x_prompt, x_sample
    prompt_new, sample_new = [], []
    row = lambda a: a.reshape(1, -1)
    for l in range(depth):
        w_main, w_hp_hi, w_hp_lo, wup = _prep_weights(w_in[l], w_gla_gate_up[l], d)
        w = dict(attn_pre_norm=row(attn_pre_norm[l]), w_main=w_main, w_hp_hi=w_hp_hi, w_hp_lo=w_hp_lo, wup=wup,
                 b_gla_gate=row(b_gla_gate[l]), gla_head_norm=row(gla_head_norm[l]),
                 w_branch_gla=w_branch_gla[l].astype(BF16), w_branch_dsa=w_branch_dsa[l].astype(BF16),
                 w_out=w_out[l].astype(BF16), attn_post_norm=row(attn_post_norm[l]),
                 ffn_pre_norm=row(ffn_pre_norm[l]), w_up=w_up[l].astype(BF16), conv_w=conv_w[l],
                 conv_b=row(conv_b[l]), w_down_half=(0.5 * w_down[l]).astype(BF16),
                 ffn_post_norm=row(ffn_post_norm[l]))
        ffn2 = conv_w.shape[-1]
        y_prompt, *new = _run_layer(
            y_prompt,
            jnp.zeros((bsz, 0, DSA_KV_HEADS * DSA_HEAD_DIM), dt),
            jnp.zeros((bsz, 0, DSA_KV_HEADS * DSA_HEAD_DIM), dt),
            jnp.zeros((bsz, 0, IDX_DIM), dt),
            jnp.zeros((bsz, GLA_HEADS, GLA_DK, GLA_DV), F32),
            jnp.zeros((bsz, CONV_W - 1, ffn2), dt),
            w)
        prompt_new.append(new)
        y_sample, *new = _run_layer(
            y_sample, cache_k[l], cache_v[l], cache_k_idx[l], state_gla[l], state_ffn_conv[l],
            w)
        sample_new.append(new)
    k_p, v_p, kidx_p, gla_p, conv_p = (jnp.stack(t, axis=0) for t in zip(*prompt_new))
    k_s, v_s, kidx_s, gla_s, conv_s = (jnp.stack(t, axis=0) for t in zip(*sample_new))
    return (y_prompt, y_sample, k_p, v_p, kidx_p, gla_p, conv_p, k_s, v_s, kidx_s, gla_s, conv_s)
```
